```python
import math
import jax
import jax.numpy as jnp
from jax import lax
import numpy as np

D_MODEL = 1024
BATCH = 32
SEQ = 256
DEPTH = 1
DEC_BATCH = 4
DEC_SEQ = 4096
PAST_LEN = 512

GRID_W = 64
MLA_HEADS = 8
QK_NOPE = 64
QK_ROPE = 32
V_DIM = 64
Q_LORA = 384
KV_LORA = 256
ROPE_BASE = 10000.0
Q_BLOCK = 128
HG_HEADS = 4
HG_DK = 128
HG_DV = 128
HG_WIDTH = HG_HEADS * HG_DV
HG_CHUNK = 16
D_FF = 2816
CONV_W = 3
EPS = 1e-6
IN_SPLITS = [Q_LORA, KV_LORA, QK_ROPE,
             HG_HEADS * HG_DK,
             HG_HEADS * HG_DK,
             HG_HEADS * HG_DK,
             HG_WIDTH,
             HG_WIDTH,
             D_MODEL,
             D_MODEL]
D_IN = sum(IN_SPLITS)

kernel_name = 'mla_hgrn2_convglu_prefix_diffusion_step'


def rmsnorm(x, g):
    xf = x.astype(jnp.float32)
    y = xf * lax.rsqrt(jnp.mean(xf * xf, axis=-1, keepdims=True) + EPS)
    return (y * g).astype(x.dtype)


def split_cols(p):
    offs = [int(o) for o in np.cumsum(IN_SPLITS)[:-1]]
    return jnp.split(p, offs, axis=-1)


def axial_rope_tables(T):
    rows = T // GRID_W
    rr, cc = jnp.meshgrid(jnp.arange(rows, dtype=jnp.float32),
                          jnp.arange(GRID_W, dtype=jnp.float32), indexing='ij')
    r_pos, c_pos = rr.reshape(-1), cc.reshape(-1)
    n_freq = QK_ROPE // 4
    inv = jnp.power(ROPE_BASE, -jnp.arange(n_freq, dtype=jnp.float32) / n_freq)
    ang = jnp.concatenate([r_pos[:, None] * inv, c_pos[:, None] * inv], axis=-1)
    ang = jnp.concatenate([ang, ang], axis=-1)
    return jnp.cos(ang), jnp.sin(ang)


def rotate_half(x):
    x1, x2 = jnp.split(x, 2, axis=-1)
    return jnp.concatenate([-x2, x1], axis=-1)


def apply_rope(x, cos, sin):
    xf = x.astype(jnp.float32)
    return (xf * cos + rotate_half(xf) * sin).astype(x.dtype)


def mla_attention(q_nope, q_rope, k_nope, k_rope, v):
    B, T = q_nope.shape[:2]
    nb = T // Q_BLOCK
    scale = (QK_NOPE + QK_ROPE) ** -0.5
    qn = jnp.moveaxis(q_nope.reshape(B, nb, Q_BLOCK, MLA_HEADS, QK_NOPE), 1, 0)
    qr = jnp.moveaxis(q_rope.reshape(B, nb, Q_BLOCK, MLA_HEADS, QK_ROPE), 1, 0)

    def block(args):
        qn_b, qr_b = args
        s = (jnp.einsum('bqhd,bkhd->bhqk', qn_b, k_nope)
             + jnp.einsum('bqhr,bkr->bhqk', qr_b, k_rope)).astype(jnp.float32)
        p = jax.nn.softmax(s * scale, axis=-1).astype(v.dtype)
        return jnp.einsum('bhqk,bkhd->bqhd', p, v)

    o = lax.map(block, (qn, qr))
    return jnp.moveaxis(o, 0, 1).reshape(B, T, MLA_HEADS * V_DIM)


def hgrn2_chunk_scan(q, k, v, logf, s0):
    B, T = q.shape[:2]
    n = T // HG_CHUNK

    def chunks(a):
        return a.astype(jnp.float32).reshape(B, n, HG_CHUNK, HG_HEADS, a.shape[-1])

    qc, kc, vc, gc = chunks(q), chunks(k), chunks(v), chunks(logf)
    G = jnp.cumsum(gc, axis=2)
    G_last = G[:, :, -1]
    causal = jnp.tril(jnp.ones((HG_CHUNK, HG_CHUNK), dtype=bool))[None, None, :, :, None, None]
    diff = G[:, :, :, None] - G[:, :, None, :]
    decay = jnp.exp(jnp.where(causal, diff, -jnp.inf))
    A = jnp.einsum('bntshk,bnthk,bnshk->bnhts', decay, qc, kc)
    o_intra = jnp.einsum('bnhts,bnshv->bnthv', A, vc)
    U = jnp.einsum('bnshk,bnshv->bnhkv', kc * jnp.exp(G_last[:, :, None] - G), vc)

    def step(S, inp):
        dec, u = inp
        return dec[..., None] * S + u, S

    S_final, S_prev = lax.scan(step, s0.astype(jnp.float32),
                               (jnp.moveaxis(jnp.exp(G_last), 1, 0), jnp.moveaxis(U, 1, 0)))
    S_prev = jnp.moveaxis(S_prev, 0, 1)
    o_inter = jnp.einsum('bnthk,bnhkv->bnthv', qc * jnp.exp(G), S_prev)
    return (o_intra + o_inter).reshape(B, T, HG_HEADS, HG_DV), S_final


def hgrn2_direction(q, v, z_f, lb, s0, reverse):
    B, T = q.shape[:2]
    f = lb + (1.0 - lb) * jax.nn.sigmoid(z_f.astype(jnp.float32))
    f = f.reshape(B, T, HG_HEADS, HG_DK)
    logf = jnp.log(f)
    k = 1.0 - f
    if reverse:
        q, k, v, logf = jnp.flip(q, 1), jnp.flip(k, 1), jnp.flip(v, 1), jnp.flip(logf, 1)
    o, s = hgrn2_chunk_scan(q, k, v, logf, s0)
    if reverse:
        o = jnp.flip(o, 1)
    return o, s


def hgrn2_branch(hq, hf_f, hf_b, hi, hg, lb_f, lb_b, s0_f, s0_b, g_hg):
    B, T = hq.shape[:2]
    q = jax.nn.silu(hq).reshape(B, T, HG_HEADS, HG_DK)
    v = hi.reshape(B, T, HG_HEADS, HG_DV)
    o_f, s_f = hgrn2_direction(q, v, hf_f, lb_f, s0_f, False)
    o_b, s_b = hgrn2_direction(q, v, hf_b, lb_b, s0_b, True)
    o = rmsnorm(o_f + o_b, g_hg.reshape(HG_HEADS, HG_DV))
    o = o.reshape(B, T, HG_WIDTH).astype(hq.dtype) * jax.nn.silu(hg)
    return o, s_f, s_b


def conv_glu(h, w_ffn_in, conv_w, conv_b, w_ffn_out):
    T = h.shape[1]
    gate, val = jnp.split(h @ w_ffn_in, 2, axis=-1)
    pad = CONV_W // 2
    gp = jnp.pad(gate, ((0, 0), (pad, pad), (0, 0)))
    gconv = conv_b + sum(gp[:, j:j + T] * conv_w[j] for j in range(CONV_W))
    return (jax.nn.gelu(gconv) * val) @ w_ffn_out


def trunk_layer(x, mod, rope, cache, lw):
    B, T, _ = x.shape
    sh1, sc1, gt1, sh2, sc2, gt2 = jnp.split(mod, 6, axis=-1)
    h = rmsnorm(x, lw['g_mix']) * (1.0 + sc1) + sh1
    pq, pkv, pkr, hq, hf_f, hf_b, hi, hg, ga, gb = split_cols(h @ lw['w_in'])
    q = (rmsnorm(pq, lw['g_q']) @ lw['w_uq']).reshape(B, T, MLA_HEADS, QK_NOPE + QK_ROPE)
    q_nope, q_rope = q[..., :QK_NOPE], q[..., QK_NOPE:]
    ckv = rmsnorm(pkv, lw['g_kv'])
    k_rope = pkr
    if rope is None:
        ckv_keys, krope_keys = ckv, k_rope
        s0_f = jnp.zeros((B, HG_HEADS, HG_DK, HG_DV), jnp.float32)
        s0_b = s0_f
    else:
        cos, sin = rope
        q_rope = apply_rope(q_rope, cos[:, None, :], sin[:, None, :])
        ckv_ctx, krope_ctx, s0_f, s0_b = cache
        ckv_keys = jnp.concatenate([ckv, ckv_ctx], axis=1)
        krope_keys = jnp.concatenate([apply_rope(k_rope, cos, sin), krope_ctx], axis=1)
    K = ckv_keys.shape[1]
    kv = (ckv_keys @ lw['w_ukv']).reshape(B, K, MLA_HEADS, QK_NOPE + V_DIM)
    o_mla = mla_attention(q_nope, q_rope, kv[..., :QK_NOPE], krope_keys, kv[..., QK_NOPE:])
    o_hg, s_f, s_b = hgrn2_branch(hq, hf_f, hf_b, hi, hg, lw['lb_f'], lw['lb_b'], s0_f, s0_b, lw['g_hg'])
    mix = jax.nn.sigmoid(ga) * (o_mla @ lw['w_mla_o']) + jax.nn.sigmoid(gb) * (o_hg @ lw['w_hg_o'])
    x = x + gt1 * (mix @ lw['w_out'])
    h2 = rmsnorm(x, lw['g_ffn']) * (1.0 + sc2) + sh2
    x = x + gt2 * conv_glu(h2, lw['w_ffn_in'], lw['conv_w'], lw['conv_b'], lw['w_ffn_out'])
    return x, ckv, k_rope, s_f, s_b


def setup_inputs(seed: int = 0) -> dict:
    key = jax.random.key(seed)
    ks = iter(jax.random.split(key, 40))

    def nrm(shape, scale):
        return jax.random.normal(next(ks), shape, jnp.float32) * scale

    def gain(shape):
        return 1.0 + nrm(shape, 0.02)

    D = D_MODEL
    return {
        'x_prompt': nrm((BATCH, SEQ, D), 1.0),
        'x_sample': nrm((DEC_BATCH, DEC_SEQ, D), 1.0),
        'cache_ckv': nrm((DEC_BATCH, DEPTH, PAST_LEN, KV_LORA), 1.0),
        'cache_krope': nrm((DEC_BATCH, DEPTH, PAST_LEN, QK_ROPE), 1.0),
        'state_hgrn_fwd': nrm((DEC_BATCH, DEPTH, HG_HEADS, HG_DK, HG_DV), 0.3),
        'state_hgrn_bwd': nrm((DEC_BATCH, DEPTH, HG_HEADS, HG_DK, HG_DV), 0.3),
        'c': nrm((DEC_BATCH, D), 1.0),
        'c_ctx': nrm((D,), 1.0),
        'w_mod': nrm((DEPTH, D, 6 * D), D ** -0.5),
        'b_mod': nrm((DEPTH, 6 * D), 0.01),
        'g_mix': gain((DEPTH, D)),
        'w_in': nrm((DEPTH, D, D_IN), D ** -0.5),
        'g_q': gain((DEPTH, Q_LORA)),
        'w_uq': nrm((DEPTH, Q_LORA, MLA_HEADS * (QK_NOPE + QK_ROPE)), Q_LORA ** -0.5),
        'g_kv': gain((DEPTH, KV_LORA)),
        'w_ukv': nrm((DEPTH, KV_LORA, MLA_HEADS * (QK_NOPE + V_DIM)), KV_LORA ** -0.5),
        'w_mla_o': nrm((DEPTH, MLA_HEADS * V_DIM, D), (MLA_HEADS * V_DIM) ** -0.5),
        'lb_param': nrm((DEPTH + 1, 2, HG_HEADS * HG_DK), 0.5),
        'g_hg': gain((DEPTH, HG_WIDTH)),
        'w_hg_o': nrm((DEPTH, HG_WIDTH, D), HG_WIDTH ** -0.5),
        'w_out': nrm((DEPTH, D, D), D ** -0.5),
        'g_ffn': gain((DEPTH, D)),
        'w_ffn_in': nrm((DEPTH, D, 2 * D_FF), D ** -0.5),
        'conv_w': nrm((DEPTH, CONV_W, D_FF), CONV_W ** -0.5),
        'conv_b': nrm((DEPTH, D_FF), 0.01),
        'w_ffn_out': nrm((DEPTH, D_FF, D), D_FF ** -0.5),
        'g_final': gain((D,)),
    }


def reference(x_prompt, x_sample, cache_ckv, cache_krope, state_hgrn_fwd, state_hgrn_bwd,
              c, c_ctx, w_mod, b_mod, g_mix, w_in, g_q, w_uq, g_kv, w_ukv, w_mla_o,
              lb_param, g_hg, w_hg_o, w_out, g_ffn, w_ffn_in, conv_w, conv_b, w_ffn_out, g_final):
    lb_all = jnp.cumsum(jax.nn.softmax(lb_param.astype(jnp.float32), axis=0), axis=0)
    rope = axial_rope_tables(x_sample.shape[1])
    x_ctx, x_lat = x_prompt, x_sample
    new_ckv, new_krope, new_sf, new_sb = [], [], [], []
    for l in range(DEPTH):
        lw = {'g_mix': g_mix[l], 'w_in': w_in[l], 'g_q': g_q[l], 'w_uq': w_uq[l], 'g_kv': g_kv[l],
              'w_ukv': w_ukv[l], 'w_mla_o': w_mla_o[l], 'lb_f': lb_all[l, 0], 'lb_b': lb_all[l, 1],
              'g_hg': g_hg[l], 'w_hg_o': w_hg_o[l], 'w_out': w_out[l], 'g_ffn': g_ffn[l],
              'w_ffn_in': w_ffn_in[l], 'conv_w': conv_w[l], 'conv_b': conv_b[l], 'w_ffn_out': w_ffn_out[l]}
        mod_ctx = (jax.nn.silu(c_ctx) @ w_mod[l] + b_mod[l])[None, None, :]
        mod_lat = (jax.nn.silu(c) @ w_mod[l] + b_mod[l])[:, None, :]
        x_ctx, ckv_l, krope_l, sf_l, sb_l = trunk_layer(x_ctx, mod_ctx, None, None, lw)
        new_ckv.append(ckv_l)
        new_krope.append(krope_l)
        new_sf.append(sf_l)
        new_sb.append(sb_l)
        cache_l = (cache_ckv[:, l], cache_krope[:, l], state_hgrn_fwd[:, l], state_hgrn_bwd[:, l])
        x_lat, _, _, _, _ = trunk_layer(x_lat, mod_lat, rope, cache_l, lw)
    y_prompt = rmsnorm(x_ctx, g_final)
    y_sample = rmsnorm(x_lat, g_final)
    ckv_out = jnp.stack(new_ckv, axis=1)
    krope_out = jnp.stack(new_krope, axis=1)
    sf_out = jnp.stack(new_sf, axis=1).astype(x_prompt.dtype)
    sb_out = jnp.stack(new_sb, axis=1).astype(x_prompt.dtype)
    return (y_prompt, y_sample, ckv_out, krope_out, sf_out, sb_out)
```

```python
import functools

import jax
import jax.numpy as jnp
from jax import lax
from jax.experimental import pallas as pl
from jax.experimental.pallas import tpu as pltpu

F32 = jnp.float32
BF16 = jnp.bfloat16

D_MODEL = 1024
GRID_W = 64
MLA_HEADS = 8
QK_NOPE = 64
QK_ROPE = 32
V_DIM = 64
Q_LORA = 384
KV_LORA = 256
ROPE_BASE = 10000.0
HG_HEADS = 4
HG_DK = 128
HG_DV = 128
HG_WIDTH = HG_HEADS * HG_DV
D_FF = 2816
EPS = 1e-6

LANES = 128
BF16_SUBLANES = 16
HG_CHUNK = 128
HG_LEVELS = 7
FF_CHUNK = 256
VMEM_LIMIT = 56 * 1024 * 1024

C_PQ = 0
C_PKV = Q_LORA
C_PKR = C_PKV + KV_LORA
C_HQ = C_PKR + LANES
C_ZF = C_HQ + HG_WIDTH
C_HG = C_ZF + 3 * HG_WIDTH
C_GA = C_HG + HG_WIDTH
C_END = C_GA + 2 * D_MODEL
KR_LANE0 = LANES - QK_ROPE

ATTN_SCALE = (QK_NOPE + QK_ROPE) ** -0.5


def _rms(x, g):
    return x * lax.rsqrt(jnp.mean(x * x, axis=-1, keepdims=True) + EPS) * g


def _const_spec(shape):
    n = len(shape)
    return pl.BlockSpec(shape, lambda *_: (0,) * n, pipeline_mode=pl.Buffered(1))


def _params(sem):
    return pltpu.CompilerParams(dimension_semantics=sem, vmem_limit_bytes=VMEM_LIMIT)


def _mod_kernel(c_ref, w_ref, b_ref, o_ref):
    c = c_ref[...]
    a = c * jax.nn.sigmoid(c)
    o_ref[...] = jnp.dot(a, w_ref[...], precision=lax.Precision.HIGHEST,
                         preferred_element_type=F32) + b_ref[...]


def _mod_call(cvec, w_mod, b_mod):
    n = w_mod.shape[1]
    tn = 1024
    return pl.pallas_call(
        _mod_kernel,
        grid=(n // tn,),
        in_specs=[pl.BlockSpec((8, D_MODEL), lambda j: (0, 0)),
                  pl.BlockSpec((D_MODEL, tn), lambda j: (0, j)),
                  pl.BlockSpec((1, tn), lambda j: (0, j))],
        out_specs=pl.BlockSpec((8, tn), lambda j: (0, j)),
        out_shape=jax.ShapeDtypeStruct((8, n), F32),
        compiler_params=_params(("arbitrary",)),
        name="mod",
    )(cvec, w_mod, b_mod.reshape(1, n))


def _proj_kernel(*refs, rope, emit_cache):
    it = iter(refs)
    x_ref = next(it)
    mod_ref = next(it)
    tab_ref = next(it) if rope else None
    gmix_ref, win_ref, gq_ref, wuq_ref, gkv_ref, wk_ref, wv_ref = (next(it) for _ in range(7))
    q_ref, k_ref, v_ref, hg_ref, gate_ref = (next(it) for _ in range(5))
    if emit_cache:
        ckv_ref, kr_ref = next(it), next(it)

    sh1 = mod_ref[0, 0:1, :]
    sc1 = mod_ref[0, 1:2, :]
    hb = (_rms(x_ref[...], gmix_ref[...]) * (1.0 + sc1) + sh1).astype(BF16)

    def proj(c0, c1):
        return jnp.dot(hb, win_ref[:, c0:c1], preferred_element_type=F32)

    if rope:
        cs1, sa1, sb1 = tab_ref[0], tab_ref[1], tab_ref[2]

        def rot(x, reps):
            w = x.shape[1]
            cs = jnp.concatenate([cs1] * reps, axis=1) if reps > 1 else cs1
            sa = jnp.concatenate([sa1] * reps, axis=1) if reps > 1 else sa1
            sb = jnp.concatenate([sb1] * reps, axis=1) if reps > 1 else sb1
            half = QK_ROPE // 2
            return x * cs + pltpu.roll(x, w - half, 1) * sa + pltpu.roll(x, half, 1) * sb

    qn = _rms(proj(C_PQ, C_PKV), gq_ref[...]).astype(BF16)
    q = jnp.dot(qn, wuq_ref[...], preferred_element_type=F32)
    q = rot(q, MLA_HEADS) if rope else q * ATTN_SCALE
    q_ref[...] = q.astype(BF16)

    ckv = _rms(proj(C_PKV, C_PKR), gkv_ref[...])
    pkr = proj(C_PKR, C_HQ)
    if emit_cache:
        ckv_ref[...] = ckv
        kr_ref[...] = pkr[:, KR_LANE0:]
    ckv_b = ckv.astype(BF16)
    kr = rot(pkr, 1) if rope else pkr
    kin = jnp.concatenate([ckv_b, kr.astype(BF16)], axis=1)
    k_ref[...] = jnp.dot(kin, wk_ref[...], preferred_element_type=F32).astype(BF16)
    lane = lax.broadcasted_iota(jnp.int32, (1, MLA_HEADS * LANES), 1)
    ones = jnp.where((lane & (LANES - 1)) >= V_DIM, 1.0, 0.0)
    v_ref[...] = (jnp.dot(ckv_b, wv_ref[...], preferred_element_type=F32) + ones).astype(BF16)

    hq = proj(C_HQ, C_ZF)
    hg_ref[:, 0:HG_WIDTH] = hq * jax.nn.sigmoid(hq)
    hg_ref[:, HG_WIDTH:4 * HG_WIDTH] = proj(C_ZF, C_HG)
    og = proj(C_HG, C_GA)
    hg_ref[:, 4 * HG_WIDTH:5 * HG_WIDTH] = og * jax.nn.sigmoid(og)

    gate_ref[...] = jax.nn.sigmoid(proj(C_GA, C_END))


def _proj_call(x, mod, tabs, pw, *, seq_len, mod_row0, per_seq_mod, emit_cache, tm):
    n_tok = x.shape[0]
    tps = seq_len // tm
    rope = tabs is not None
    if per_seq_mod:
        mod_map = lambda t: (mod_row0 + t // tps, 0, 0)
    else:
        mod_map = lambda t: (mod_row0, 0, 0)
    row = lambda w: pl.BlockSpec((tm, w), lambda t: (t, 0))
    in_specs = [row(D_MODEL), pl.BlockSpec((1, 6, D_MODEL), mod_map)]
    args = [x, mod]
    if rope:
        in_specs.append(pl.BlockSpec((3, tm, LANES), lambda t: (0, t % tps, 0)))
        args.append(tabs)
    for name in ("g_mix", "w_in", "g_q", "w_uq", "g_kv", "w_k", "w_v"):
        in_specs.append(_const_spec(pw[name].shape))
        args.append(pw[name])
    hw = MLA_HEADS * LANES
    out_shape = [jax.ShapeDtypeStruct((n_tok, hw), BF16),
                 jax.ShapeDtypeStruct((n_tok, hw), BF16),
                 jax.ShapeDtypeStruct((n_tok, hw), BF16),
                 jax.ShapeDtypeStruct((n_tok, 5 * HG_WIDTH), F32),
                 jax.ShapeDtypeStruct((n_tok, 2 * D_MODEL), F32)]
    out_specs = [row(hw), row(hw), row(hw), row(5 * HG_WIDTH), row(2 * D_MODEL)]
    if emit_cache:
        out_shape += [jax.ShapeDtypeStruct((n_tok, KV_LORA), F32),
                      jax.ShapeDtypeStruct((n_tok, QK_ROPE), F32)]
        out_specs += [row(KV_LORA), row(QK_ROPE)]
    return pl.pallas_call(
        functools.partial(_proj_kernel, rope=rope, emit_cache=emit_cache),
        grid=(n_tok // tm,),
        in_specs=in_specs,
        out_specs=out_specs,
        out_shape=out_shape,
        compiler_params=_params(("arbitrary",)),
        name="proj_lat" if rope else "proj_ctx",
    )(*args)


def _cachekv_kernel(ckv_ref, kr_ref, wk_ref, wv_ref, k_ref, v_ref):
    ckv_b = ckv_ref[...].astype(BF16)
    tm = ckv_b.shape[0]
    kr = jnp.concatenate([jnp.zeros((tm, KR_LANE0), BF16), kr_ref[...].astype(BF16)], axis=1)
    kin = jnp.concatenate([ckv_b, kr], axis=1)
    k_ref[...] = jnp.dot(kin, wk_ref[...], preferred_element_type=F32).astype(BF16)
    lane = lax.broadcasted_iota(jnp.int32, (1, MLA_HEADS * LANES), 1)
    ones = jnp.where((lane & (LANES - 1)) >= V_DIM, 1.0, 0.0)
    v_ref[...] = (jnp.dot(ckv_b, wv_ref[...], preferred_element_type=F32) + ones).astype(BF16)


def _cachekv_call(ckv, kr, pw, tm):
    n_tok = ckv.shape[0]
    hw = MLA_HEADS * LANES
    row = lambda w: pl.BlockSpec((tm, w), lambda t: (t, 0))
    return pl.pallas_call(
        _cachekv_kernel,
        grid=(n_tok // tm,),
        in_specs=[row(KV_LORA), row(QK_ROPE), _const_spec(pw["w_k"].shape), _const_spec(pw["w_v"].shape)],
        out_specs=[row(hw), row(hw)],
        out_shape=[jax.ShapeDtypeStruct((n_tok, hw), BF16)] * 2,
        compiler_params=_params(("arbitrary",)),
        name="cache_kv",
    )(ckv, kr, pw["w_k"], pw["w_v"])


def _attn_kernel(*refs, nseg, tk):
    q_ref = refs[0]
    kv = refs[1:1 + 2 * nseg]
    o_ref = refs[1 + 2 * nseg]
    tq = q_ref.shape[1]
    lane = lax.broadcasted_iota(jnp.int32, (1, LANES), 1)
    outs = []
    for h in range(MLA_HEADS):
        hs = slice(h * LANES, (h + 1) * LANES)
        qh = q_ref[0, :, hs]
        m = jnp.full((tq, 1), -jnp.inf, F32)
        acc = jnp.zeros((tq, LANES), F32)
        for seg in range(nseg):
            k_ref, v_ref = kv[2 * seg], kv[2 * seg + 1]

            def body(c, carry, k_ref=k_ref, v_ref=v_ref):
                m, acc = carry
                off = pl.multiple_of(c * tk, tk)
                kc = k_ref[0, pl.ds(off, tk), hs]
                vc = v_ref[0, pl.ds(off, tk), hs]
                s = lax.dot_general(qh, kc, (((1,), (1,)), ((), ())), preferred_element_type=F32)
                m_new = jnp.maximum(m, jnp.max(s, axis=1, keepdims=True))
                alpha = jnp.exp(m - m_new)
                p = jnp.exp(s - m_new).astype(BF16)
                acc = alpha * acc + jnp.dot(p, vc, preferred_element_type=F32)
                return m_new, acc

            m, acc = lax.fori_loop(0, k_ref.shape[1] // tk, body, (m, acc))
        outs.append(acc / pltpu.roll(acc, V_DIM, 1))
    for pair in range(MLA_HEADS // 2):
        a = outs[2 * pair]
        b = pltpu.roll(outs[2 * pair + 1], V_DIM, 1)
        o_ref[0, :, pair * LANES:(pair + 1) * LANES] = jnp.where(lane < V_DIM, a, b).astype(BF16)


def _attn_call(q, kvs, *, tq, tk, name):
    b, t, hw = q.shape
    in_specs = [pl.BlockSpec((1, tq, hw), lambda i, j: (i, j, 0))]
    args = [q]
    for k, v in kvs:
        for a in (k, v):
            in_specs.append(pl.BlockSpec((1, a.shape[1], hw), lambda i, j: (i, 0, 0)))
            args.append(a)
    return pl.pallas_call(
        functools.partial(_attn_kernel, nseg=len(kvs), tk=tk),
        grid=(b, t // tq),
        in_specs=in_specs,
        out_specs=pl.BlockSpec((1, tq, MLA_HEADS * V_DIM), lambda i, j: (i, j, 0)),
        out_shape=jax.ShapeDtypeStruct((b, t, MLA_HEADS * V_DIM), BF16),
        compiler_params=_params(("arbitrary", "arbitrary")),
        name=name,
    )(*args)


def _hgrn_kernel(*refs, layer, has_s0):
    it = iter(refs)
    q_ref, zf_ref, zb_ref, v_ref, g_ref, lbp_ref, ghg_ref = (next(it) for _ in range(7))
    if has_s0:
        s0f_ref, s0b_ref = next(it), next(it)
    o_ref, sf_ref, sb_ref, sbst_ref = (next(it) for _ in range(4))

    seq = q_ref.shape[1]
    c_len = HG_CHUNK
    n_chunks = seq // c_len

    lbp = lbp_ref[...]
    e = jnp.exp(lbp - jnp.max(lbp, axis=0, keepdims=True))
    lb = jnp.sum(e[:layer + 1], axis=0) / jnp.sum(e, axis=0)
    lb_f, lb_b = lb[0:1, :], lb[1:2, :]

    row = lax.broadcasted_iota(jnp.int32, (c_len, LANES), 0)
    ti = lax.broadcasted_iota(jnp.int32, (c_len, c_len), 0)
    si = lax.broadcasted_iota(jnp.int32, (c_len, c_len), 1)
    txs = ti ^ si
    lvl = jnp.full((c_len, c_len), -1, jnp.int32)
    for j in range(HG_LEVELS):
        lvl = lvl + jnp.where(txs >= (1 << j), 1, 0)

    def gates(z, lbv):
        f = lbv + (1.0 - lbv) * jax.nn.sigmoid(z)
        return jnp.log(f), 1.0 - f

    def upsweep(x):
        pre, tot = [x], [x]
        for j in range(HG_LEVELS):
            h = 1 << j
            upper = (row & h) != 0
            t = tot[-1]
            sib = jnp.where(upper, pltpu.roll(t, h, 0), pltpu.roll(t, c_len - h, 0))
            tot.append(t + sib)
            pre.append(pre[-1] + jnp.where(upper, sib, 0.0))
        return pre, tot

    nt = (((1,), (1,)), ((), ()))
    tn = (((0,), (0,)), ((), ()))

    def chunk(ref, c):
        return ref[0, pl.ds(pl.multiple_of(c * c_len, c_len), c_len), :]

    def bwd_body(i, st):
        c = n_chunks - 1 - i
        sbst_ref[c] = st
        lg, kb = gates(chunk(zb_ref, c), lb_b)
        pre, tot = upsweep(lg)
        khat = (kb * jnp.exp(pre[HG_LEVELS] - lg)).astype(BF16)
        ut = lax.dot_general(chunk(v_ref, c).astype(BF16), khat, tn, preferred_element_type=F32)
        return jnp.exp(tot[HG_LEVELS][0:1, :]) * st + ut

    st_b0 = s0b_ref[0, 0].T if has_s0 else jnp.zeros((HG_DV, HG_DK), F32)
    st_b = lax.fori_loop(0, n_chunks, bwd_body, st_b0)
    sb_ref[0, 0] = st_b.T

    ghg = ghg_ref[...]

    def fwd_body(c, st):
        q = chunk(q_ref, c)
        vb = chunk(v_ref, c).astype(BF16)
        lgf, kf = gates(chunk(zf_ref, c), lb_f)
        lgb, kb = gates(chunk(zb_ref, c), lb_b)
        pf, tf = upsweep(lgf)
        pb, tb = upsweep(lgb)
        qb = q.astype(BF16)
        a = lax.dot_general(qb, (kf + kb).astype(BF16), nt, preferred_element_type=F32)
        a = jnp.where(lvl == -1, a, 0.0)
        for j in range(HG_LEVELS):
            upper = (row & (1 << j)) != 0
            eq = jnp.exp(jnp.where(upper, pf[j], tb[j] - pb[j] + lgb))
            ek = jnp.exp(jnp.where(upper, pb[j] - lgb, tf[j] - pf[j]))
            qs = (q * eq).astype(BF16)
            ks = (jnp.where(upper, kb, kf) * ek).astype(BF16)
            aj = lax.dot_general(qs, ks, nt, preferred_element_type=F32)
            a = jnp.where(lvl == j, aj, a)
        o = jnp.dot(a.astype(BF16), vb, preferred_element_type=F32)
        qhf = (q * jnp.exp(pf[HG_LEVELS])).astype(BF16)
        qhb = (q * jnp.exp(tb[HG_LEVELS] - pb[HG_LEVELS] + lgb)).astype(BF16)
        o = o + lax.dot_general(qhf, st.astype(BF16), nt, preferred_element_type=F32)
        o = o + lax.dot_general(qhb, sbst_ref[c].astype(BF16), nt, preferred_element_type=F32)
        on = _rms(o, ghg) * chunk(g_ref, c)
        o_ref[0, pl.ds(pl.multiple_of(c * c_len, c_len), c_len), :] = on.astype(BF16)
        khat = (kf * jnp.exp(tf[HG_LEVELS] - pf[HG_LEVELS])).astype(BF16)
        ut = lax.dot_general(vb, khat, tn, preferred_element_type=F32)
        return jnp.exp(tf[HG_LEVELS][0:1, :]) * st + ut

    st_f0 = s0f_ref[0, 0].T if has_s0 else jnp.zeros((HG_DV, HG_DK), F32)
    st_f = lax.fori_loop(0, n_chunks, fwd_body, st_f0)
    sf_ref[0, 0] = st_f.T


def _hgrn_call(hgin, lb_param, g_hg, s0, *, layer, name):
    b, t, _ = hgin.shape
    has_s0 = s0 is not None
    col = lambda g: pl.BlockSpec((1, t, LANES), lambda i, h, g=g: (i, 0, g * HG_HEADS + h))
    in_specs = [col(0), col(1), col(2), col(3), col(4),
                pl.BlockSpec((lb_param.shape[0], 2, LANES), lambda i, h: (0, 0, h)),
                pl.BlockSpec((1, LANES), lambda i, h: (0, h))]
    args = [hgin] * 5 + [lb_param, g_hg]
    st_spec = pl.BlockSpec((1, 1, HG_DK, HG_DV), lambda i, h: (i, h, 0, 0))
    if has_s0:
        in_specs += [st_spec, st_spec]
        args += list(s0)
    st_shape = jax.ShapeDtypeStruct((b, HG_HEADS, HG_DK, HG_DV), F32)
    return pl.pallas_call(
        functools.partial(_hgrn_kernel, layer=layer, has_s0=has_s0),
        grid=(b, HG_HEADS),
        in_specs=in_specs,
        out_specs=[pl.BlockSpec((1, t, LANES), lambda i, h: (i, 0, h)), st_spec, st_spec],
        out_shape=[jax.ShapeDtypeStruct((b, t, HG_WIDTH), BF16), st_shape, st_shape],
        scratch_shapes=[pltpu.VMEM((t // HG_CHUNK, HG_DV, HG_DK), F32)],
        compiler_params=_params(("arbitrary", "arbitrary")),
        name=name,
    )(*args)


def _mix_kernel(om_ref, oh_ref, gate_ref, x_ref, mod_ref, wmo_ref, who_ref, wout_ref, gffn_ref,
                x1_ref, h2_ref):
    a = jnp.dot(om_ref[...], wmo_ref[...], preferred_element_type=F32)
    b = jnp.dot(oh_ref[...], who_ref[...], preferred_element_type=F32)
    mix = gate_ref[:, 0:D_MODEL] * a + gate_ref[:, D_MODEL:2 * D_MODEL] * b
    y = jnp.dot(mix.astype(BF16), wout_ref[...], preferred_element_type=F32)
    gt1 = mod_ref[0, 2:3, :]
    sh2 = mod_ref[0, 3:4, :]
    sc2 = mod_ref[0, 4:5, :]
    x1 = x_ref[...] + gt1 * y
    x1_ref[...] = x1
    h2_ref[...] = (_rms(x1, gffn_ref[...]) * (1.0 + sc2) + sh2).astype(BF16)


def _mix_call(om, oh, gate, x, mod, pw, *, seq_len, mod_row0, per_seq_mod, tm, name):
    n_tok = x.shape[0]
    tps = seq_len // tm
    if per_seq_mod:
        mod_map = lambda t: (mod_row0 + t // tps, 0, 0)
    else:
        mod_map = lambda t: (mod_row0, 0, 0)
    row = lambda w: pl.BlockSpec((tm, w), lambda t: (t, 0))
    ws = [pw[k] for k in ("w_mla_o", "w_hg_o", "w_out", "g_ffn")]
    return pl.pallas_call(
        _mix_kernel,
        grid=(n_tok // tm,),
        in_specs=[row(MLA_HEADS * V_DIM), row(HG_WIDTH), row(2 * D_MODEL), row(D_MODEL),
                  pl.BlockSpec((1, 6, D_MODEL), mod_map)] + [_const_spec(w.shape) for w in ws],
        out_specs=[row(D_MODEL), row(D_MODEL)],
        out_shape=[jax.ShapeDtypeStruct((n_tok, D_MODEL), F32),
                   jax.ShapeDtypeStruct((n_tok, D_MODEL), BF16)],
        compiler_params=_params(("arbitrary",)),
        name=name,
    )(om, oh, gate, x, mod, *ws)


def _gelu_tanh(x):
    return 0.5 * x * (1.0 + jnp.tanh(0.7978845608028654 * (x + 0.044715 * (x * x * x))))


def _ffn_kernel(h2_ref, hp_ref, hn_ref, x1_ref, mod_ref, wg_ref, wv_ref, cw_ref, cb_ref, wo_ref,
                gfin_ref, y_ref, ext_ref, gsc_ref, u_ref, *, tm, seq_len):
    halo = BF16_SUBLANES
    t = pl.program_id(0)
    ext_ref[0:halo, :] = hp_ref[...]
    ext_ref[halo:halo + tm, :] = h2_ref[...]
    ext_ref[halo + tm:2 * halo + tm, :] = hn_ref[...]
    pos = (lax.broadcasted_iota(jnp.int32, (tm, 1), 0) + t * tm) & (seq_len - 1)
    has_prev = pos != 0
    has_next = pos != seq_len - 1
    for j in range(D_FF // FF_CHUNK):
        cs = slice(j * FF_CHUNK, (j + 1) * FF_CHUNK)
        gsc_ref[...] = jnp.dot(ext_ref[...], wg_ref[:, cs], preferred_element_type=F32)
        g_prev = jnp.where(has_prev, gsc_ref[halo - 1:halo - 1 + tm, :], 0.0)
        g_cur = gsc_ref[halo:halo + tm, :]
        g_next = jnp.where(has_next, gsc_ref[halo + 1:halo + 1 + tm, :], 0.0)
        gconv = cb_ref[:, cs] + g_prev * cw_ref[0:1, cs] + g_cur * cw_ref[1:2, cs] + g_next * cw_ref[2:3, cs]
        val = jnp.dot(h2_ref[...], wv_ref[:, cs], preferred_element_type=F32)
        u_ref[:, cs] = (_gelu_tanh(gconv) * val).astype(BF16)
    f = jnp.dot(u_ref[...], wo_ref[...], preferred_element_type=F32)
    gt2 = mod_ref[0, 5:6, :]
    y_ref[...] = _rms(x1_ref[...] + gt2 * f, gfin_ref[...])


def _ffn_call(h2, x1, mod, pw, *, seq_len, mod_row0, per_seq_mod, tm, name):
    n_tok = x1.shape[0]
    tps = seq_len // tm
    halo = BF16_SUBLANES
    n_hblk = n_tok // halo
    hpt = tm // halo
    if per_seq_mod:
        mod_map = lambda t: (mod_row0 + t // tps, 0, 0)
    else:
        mod_map = lambda t: (mod_row0, 0, 0)
    row = lambda w: pl.BlockSpec((tm, w), lambda t: (t, 0))
    ws = [pw[k] for k in ("w_ffn_g", "w_ffn_v", "conv_w", "conv_b", "w_ffn_out", "g_final")]
    return pl.pallas_call(
        functools.partial(_ffn_kernel, tm=tm, seq_len=seq_len),
        grid=(n_tok // tm,),
        in_specs=[row(D_MODEL),
                  pl.BlockSpec((halo, D_MODEL), lambda t: (jnp.maximum(t * hpt - 1, 0), 0)),
                  pl.BlockSpec((halo, D_MODEL), lambda t: (jnp.minimum((t + 1) * hpt, n_hblk - 1), 0)),
                  row(D_MODEL),
                  pl.BlockSpec((1, 6, D_MODEL), mod_map)] + [_const_spec(w.shape) for w in ws],
        out_specs=row(D_MODEL),
        out_shape=jax.ShapeDtypeStruct((n_tok, D_MODEL), F32),
        scratch_shapes=[pltpu.VMEM((tm + 2 * halo, D_MODEL), BF16),
                        pltpu.VMEM((tm + 2 * halo, FF_CHUNK), F32),
                        pltpu.VMEM((tm, D_FF), BF16)],
        compiler_params=_params(("arbitrary",)),
        name=name,
    )(h2, h2, h2, x1, mod, *ws)


def _rope_tables(seq_len):
    rows = seq_len // GRID_W
    rr, cc = jnp.meshgrid(jnp.arange(rows, dtype=F32), jnp.arange(GRID_W, dtype=F32), indexing="ij")
    n_freq = QK_ROPE // 4
    inv = jnp.power(ROPE_BASE, -jnp.arange(n_freq, dtype=F32) / n_freq)
    ang = jnp.concatenate([rr.reshape(-1)[:, None] * inv, cc.reshape(-1)[:, None] * inv], axis=-1)
    cos, sin = jnp.cos(ang), jnp.sin(ang)
    z16 = jnp.zeros_like(sin)
    ones = jnp.ones((seq_len, QK_NOPE), F32)
    zeros = jnp.zeros((seq_len, QK_NOPE), F32)
    s = ATTN_SCALE
    cs = jnp.concatenate([ones * s, cos * s, cos * s, cos, cos], axis=1)
    sa = jnp.concatenate([zeros, -sin * s, z16, -sin, z16], axis=1)
    sb = jnp.concatenate([zeros, z16, sin * s, z16, sin], axis=1)
    return jnp.stack([cs, sa, sb], axis=0)


def _prep_weights(l, g_mix, w_in, g_q, w_uq, g_kv, w_ukv, w_mla_o, g_hg, w_hg_o, w_out, g_ffn,
                  w_ffn_in, conv_w, conv_b, w_ffn_out, g_final):
    d = D_MODEL
    wi = w_in[l]
    c_kr = Q_LORA + KV_LORA
    w_in_p = jnp.concatenate([wi[:, :c_kr], jnp.zeros((d, KR_LANE0), F32),
                              wi[:, c_kr:c_kr + QK_ROPE], wi[:, c_kr + QK_ROPE:]], axis=1)
    hd = QK_NOPE + QK_ROPE
    w_uq_p = jnp.concatenate([w_uq[l].reshape(Q_LORA, MLA_HEADS, hd),
                              jnp.zeros((Q_LORA, MLA_HEADS, LANES - hd), F32)], axis=-1)
    wkv = w_ukv[l].reshape(KV_LORA, MLA_HEADS, QK_NOPE + V_DIM)
    zpad = jnp.zeros((KV_LORA, MLA_HEADS, LANES - QK_NOPE), F32)
    w_k_top = jnp.concatenate([wkv[..., :QK_NOPE], zpad], axis=-1).reshape(KV_LORA, MLA_HEADS * LANES)
    place = jnp.zeros((LANES, LANES), F32).at[
        KR_LANE0 + jnp.arange(QK_ROPE), QK_NOPE + jnp.arange(QK_ROPE)].set(1.0)
    w_k_bot = jnp.tile(place, (1, MLA_HEADS))
    w_k = jnp.concatenate([w_k_top, w_k_bot], axis=0)
    w_v = jnp.concatenate([wkv[..., QK_NOPE:], jnp.zeros((KV_LORA, MLA_HEADS, LANES - V_DIM), F32)],
                          axis=-1).reshape(KV_LORA, MLA_HEADS * LANES)
    return {
        "g_mix": g_mix[l].reshape(1, d), "w_in": w_in_p.astype(BF16),
        "g_q": g_q[l].reshape(1, Q_LORA), "w_uq": w_uq_p.reshape(Q_LORA, MLA_HEADS * LANES).astype(BF16),
        "g_kv": g_kv[l].reshape(1, KV_LORA), "w_k": w_k.astype(BF16), "w_v": w_v.astype(BF16),
        "w_mla_o": w_mla_o[l].astype(BF16), "w_hg_o": w_hg_o[l].astype(BF16),
        "w_out": w_out[l].astype(BF16), "g_ffn": g_ffn[l].reshape(1, d),
        "w_ffn_g": w_ffn_in[l][:, :D_FF].astype(BF16), "w_ffn_v": w_ffn_in[l][:, D_FF:].astype(BF16),
        "conv_w": conv_w[l], "conv_b": conv_b[l].reshape(1, D_FF),
        "w_ffn_out": w_ffn_out[l].astype(BF16), "g_final": g_final.reshape(1, d),
        "g_hg": g_hg[l].reshape(1, HG_WIDTH),
    }


def _trunk(x, mod, tabs, cache, pw, lb_param, *, layer, batch, seq_len, mod_row0, per_seq_mod,
           emit_cache, tm, tq, tag):
    n_tok = batch * seq_len
    hw = MLA_HEADS * LANES
    outs = _proj_call(x, mod, tabs, pw, seq_len=seq_len, mod_row0=mod_row0, per_seq_mod=per_seq_mod,
                      emit_cache=emit_cache, tm=tm)
    q, k, v, hgin, gate = outs[:5]
    kvs = [(k.reshape(batch, seq_len, hw), v.reshape(batch, seq_len, hw))]
    s0 = None
    if cache is not None:
        ckv_c, kr_c, s0f, s0b = cache
        past = ckv_c.shape[1]
        kc, vc = _cachekv_call(ckv_c.reshape(batch * past, KV_LORA), kr_c.reshape(batch * past, QK_ROPE),
                               pw, tm=min(512, batch * past))
        kvs.append((kc.reshape(batch, past, hw), vc.reshape(batch, past, hw)))
        s0 = (s0f, s0b)
    o_mla = _attn_call(q.reshape(batch, seq_len, hw), kvs, tq=tq, tk=min(512, seq_len), name="attn_" + tag)
    o_hg, s_f, s_b = _hgrn_call(hgin.reshape(batch, seq_len, 5 * HG_WIDTH), lb_param, pw["g_hg"], s0,
                                layer=layer, name="hgrn_" + tag)
    common = dict(seq_len=seq_len, mod_row0=mod_row0, per_seq_mod=per_seq_mod, tm=tm)
    x1, h2 = _mix_call(o_mla.reshape(n_tok, MLA_HEADS * V_DIM), o_hg.reshape(n_tok, HG_WIDTH), gate, x,
                       mod, pw, name="mix_" + tag, **common)
    y = _ffn_call(h2, x1, mod, pw, name="ffn_" + tag, **common)
    extra = outs[5:] if emit_cache else None
    return y, extra, s_f, s_b


def kernel(x_prompt, x_sample, cache_ckv, cache_krope, state_hgrn_fwd, state_hgrn_bwd, c, c_ctx, w_mod,
           b_mod, g_mix, w_in, g_q, w_uq, g_kv, w_ukv, w_mla_o, lb_param, g_hg, w_hg_o, w_out, g_ffn,
           w_ffn_in, conv_w, conv_b, w_ffn_out, g_final):
    batch, seq, d = x_prompt.shape
    dec_batch, dec_seq, _ = x_sample.shape
    depth = w_in.shape[0]
    assert depth == 1, "single trunk layer"
    l = 0
    cvec = jnp.concatenate([c_ctx[None, :], c, jnp.zeros((8 - 1 - dec_batch, d), F32)], axis=0)
    mod = _mod_call(cvec, w_mod[l], b_mod[l]).reshape(8, 6, d)
    pw = _prep_weights(l, g_mix, w_in, g_q, w_uq, g_kv, w_ukv, w_mla_o, g_hg, w_hg_o, w_out, g_ffn,
                       w_ffn_in, conv_w, conv_b, w_ffn_out, g_final)
    tabs = _rope_tables(dec_seq)

    y_ctx, cache_out, s_f, s_b = _trunk(
        x_prompt.reshape(batch * seq, d), mod, None, None, pw, lb_param, layer=l, batch=batch,
        seq_len=seq, mod_row0=0, per_seq_mod=False, emit_cache=True, tm=256, tq=256, tag="ctx")
    cache_l = (cache_ckv[:, l], cache_krope[:, l], state_hgrn_fwd[:, l], state_hgrn_bwd[:, l])
    y_lat, _, _, _ = _trunk(
        x_sample.reshape(dec_batch * dec_seq, d), mod, tabs, cache_l, pw, lb_param, layer=l,
        batch=dec_batch, seq_len=dec_seq, mod_row0=1, per_seq_mod=True, emit_cache=False, tm=256,
        tq=256, tag="lat")

    ckv_new, kr_new = cache_out
    return (y_ctx.reshape(batch, seq, d), y_lat.reshape(dec_batch, dec_seq, d),
            ckv_new.reshape(batch, 1, seq, KV_LORA), kr_new.reshape(batch, 1, seq, QK_ROPE),
            s_f.reshape(batch, 1, HG_HEADS, HG_DK, HG_DV), s_b.reshape(batch, 1, HG_HEADS, HG_DK, HG_DV))
```

```python
import functools

import jax
import jax.numpy as jnp
from jax import lax
from jax.experimental import pallas as pl
from jax.experimental.pallas import tpu as pltpu

F32 = jnp.float32
BF16 = jnp.bfloat16

D_MODEL = 1024
GRID_W = 64
MLA_HEADS = 8
QK_NOPE = 64
QK_ROPE = 32
V_DIM = 64
Q_LORA = 384
KV_LORA = 256
ROPE_BASE = 10000.0
HG_HEADS = 4
HG_DK = 128
HG_DV = 128
HG_WIDTH = HG_HEADS * HG_DV
D_FF = 2816
EPS = 1e-6

LANES = 128
BF16_SUBLANES = 16
HG_CHUNK = 128
HG_LEVELS = 7
FF_CHUNK = 256
VMEM_LIMIT = 56 * 1024 * 1024

C_PQ = 0
C_PKV = Q_LORA
C_PKR = C_PKV + KV_LORA
C_HQ = C_PKR + LANES
C_ZF = C_HQ + HG_WIDTH
C_HG = C_ZF + 3 * HG_WIDTH
C_GA = C_HG + HG_WIDTH
C_END = C_GA + 2 * D_MODEL
KR_LANE0 = LANES - QK_ROPE

ATTN_SCALE = (QK_NOPE + QK_ROPE) ** -0.5 * 1.4426950408889634


def _rms(x, g):
    return x * lax.rsqrt(jnp.mean(x * x, axis=-1, keepdims=True) + EPS) * g


def _const_spec(shape):
    n = len(shape)
    return pl.BlockSpec(shape, lambda *_: (0,) * n, pipeline_mode=pl.Buffered(1))


def _params(sem):
    return pltpu.CompilerParams(dimension_semantics=sem, vmem_limit_bytes=VMEM_LIMIT)


def _mod_kernel(c_ref, w_ref, b_ref, o_ref):
    c = c_ref[...]
    a = c * jax.nn.sigmoid(c)
    o_ref[...] = jnp.dot(a, w_ref[...], precision=lax.Precision.HIGHEST,
                         preferred_element_type=F32) + b_ref[...]


def _mod_call(cvec, w_mod, b_mod):
    n = w_mod.shape[1]
    tn = 1024
    return pl.pallas_call(
        _mod_kernel,
        grid=(n // tn,),
        in_specs=[pl.BlockSpec((8, D_MODEL), lambda j: (0, 0)),
                  pl.BlockSpec((D_MODEL, tn), lambda j: (0, j)),
                  pl.BlockSpec((1, tn), lambda j: (0, j))],
        out_specs=pl.BlockSpec((8, tn), lambda j: (0, j)),
        out_shape=jax.ShapeDtypeStruct((8, n), F32),
        compiler_params=_params(("arbitrary",)),
        name="mod",
    )(cvec, w_mod, b_mod.reshape(1, n))


def _proj_kernel(*refs, rope, emit_cache):
    it = iter(refs)
    x_ref = next(it)
    mod_ref = next(it)
    tab_ref = next(it) if rope else None
    gmix_ref, win_ref, gq_ref, wuq_ref, gkv_ref, wk_ref, wv_ref = (next(it) for _ in range(7))
    q_ref, k_ref, v_ref, hg_ref, gate_ref = (next(it) for _ in range(5))
    if emit_cache:
        ckv_ref, kr_ref = next(it), next(it)

    sh1 = mod_ref[0, 0:1, :]
    sc1 = mod_ref[0, 1:2, :]
    hb = (_rms(x_ref[...], gmix_ref[...]) * (1.0 + sc1) + sh1).astype(BF16)

    def proj(c0, c1):
        return jnp.dot(hb, win_ref[:, c0:c1], preferred_element_type=F32)

    if rope:
        cs1, sa1, sb1 = tab_ref[0], tab_ref[1], tab_ref[2]

        def rot(x, reps):
            w = x.shape[1]
            cs = jnp.concatenate([cs1] * reps, axis=1) if reps > 1 else cs1
            sa = jnp.concatenate([sa1] * reps, axis=1) if reps > 1 else sa1
            sb = jnp.concatenate([sb1] * reps, axis=1) if reps > 1 else sb1
            half = QK_ROPE // 2
            return x * cs + pltpu.roll(x, w - half, 1) * sa + pltpu.roll(x, half, 1) * sb

    qn = _rms(proj(C_PQ, C_PKV), gq_ref[...]).astype(BF16)
    q = jnp.dot(qn, wuq_ref[...], preferred_element_type=F32)
    q = rot(q, MLA_HEADS) if rope else q * ATTN_SCALE
    q_ref[...] = q.astype(BF16)

    ckv = _rms(proj(C_PKV, C_PKR), gkv_ref[...])
    pkr = proj(C_PKR, C_HQ)
    if emit_cache:
        ckv_ref[...] = ckv
        kr_ref[...] = pkr[:, KR_LANE0:]
    ckv_b = ckv.astype(BF16)
    kr = rot(pkr, 1) if rope else pkr
    kin = jnp.concatenate([ckv_b, kr.astype(BF16)], axis=1)
    k_ref[...] = jnp.dot(kin, wk_ref[...], preferred_element_type=F32).astype(BF16)
    lane = lax.broadcasted_iota(jnp.int32, (1, MLA_HEADS * LANES), 1)
    ones = jnp.where((lane & (LANES - 1)) >= V_DIM, 1.0, 0.0)
    v_ref[...] = (jnp.dot(ckv_b, wv_ref[...], preferred_element_type=F32) + ones).astype(BF16)

    hq = proj(C_HQ, C_ZF)
    hg_ref[:, 0:HG_WIDTH] = hq * jax.nn.sigmoid(hq)
    hg_ref[:, HG_WIDTH:4 * HG_WIDTH] = proj(C_ZF, C_HG)
    og = proj(C_HG, C_GA)
    hg_ref[:, 4 * HG_WIDTH:5 * HG_WIDTH] = og * jax.nn.sigmoid(og)

    gate_ref[...] = jax.nn.sigmoid(proj(C_GA, C_END))


def _proj_call(x, mod, tabs, pw, *, seq_len, mod_row0, per_seq_mod, emit_cache, tm):
    n_tok = x.shape[0]
    tps = seq_len // tm
    rope = tabs is not None
    if per_seq_mod:
        mod_map = lambda t: (mod_row0 + t // tps, 0, 0)
    else:
        mod_map = lambda t: (mod_row0, 0, 0)
    row = lambda w: pl.BlockSpec((tm, w), lambda t: (t, 0))
    in_specs = [row(D_MODEL), pl.BlockSpec((1, 6, D_MODEL), mod_map)]
    args = [x, mod]
    if rope:
        in_specs.append(pl.BlockSpec((3, tm, LANES), lambda t: (0, t % tps, 0)))
        args.append(tabs)
    for name in ("g_mix", "w_in", "g_q", "w_uq", "g_kv", "w_k", "w_v"):
        in_specs.append(_const_spec(pw[name].shape))
        args.append(pw[name])
    hw = MLA_HEADS * LANES
    out_shape = [jax.ShapeDtypeStruct((n_tok, hw), BF16),
                 jax.ShapeDtypeStruct((n_tok, hw), BF16),
                 jax.ShapeDtypeStruct((n_tok, hw), BF16),
                 jax.ShapeDtypeStruct((n_tok, 5 * HG_WIDTH), F32),
                 jax.ShapeDtypeStruct((n_tok, 2 * D_MODEL), F32)]
    out_specs = [row(hw), row(hw), row(hw), row(5 * HG_WIDTH), row(2 * D_MODEL)]
    if emit_cache:
        out_shape += [jax.ShapeDtypeStruct((n_tok, KV_LORA), F32),
                      jax.ShapeDtypeStruct((n_tok, QK_ROPE), F32)]
        out_specs += [row(KV_LORA), row(QK_ROPE)]
    return pl.pallas_call(
        functools.partial(_proj_kernel, rope=rope, emit_cache=emit_cache),
        grid=(n_tok // tm,),
        in_specs=in_specs,
        out_specs=out_specs,
        out_shape=out_shape,
        compiler_params=_params(("arbitrary",)),
        name="proj_lat" if rope else "proj_ctx",
    )(*args)


def _cachekv_kernel(ckv_ref, kr_ref, wk_ref, wv_ref, k_ref, v_ref):
    ckv_b = ckv_ref[...].astype(BF16)
    tm = ckv_b.shape[0]
    kr = jnp.concatenate([jnp.zeros((tm, KR_LANE0), BF16), kr_ref[...].astype(BF16)], axis=1)
    kin = jnp.concatenate([ckv_b, kr], axis=1)
    k_ref[...] = jnp.dot(kin, wk_ref[...], preferred_element_type=F32).astype(BF16)
    lane = lax.broadcasted_iota(jnp.int32, (1, MLA_HEADS * LANES), 1)
    ones = jnp.where((lane & (LANES - 1)) >= V_DIM, 1.0, 0.0)
    v_ref[...] = (jnp.dot(ckv_b, wv_ref[...], preferred_element_type=F32) + ones).astype(BF16)


def _cachekv_call(ckv, kr, pw, tm):
    n_tok = ckv.shape[0]
    hw = MLA_HEADS * LANES
    row = lambda w: pl.BlockSpec((tm, w), lambda t: (t, 0))
    return pl.pallas_call(
        _cachekv_kernel,
        grid=(n_tok // tm,),
        in_specs=[row(KV_LORA), row(QK_ROPE), _const_spec(pw["w_k"].shape), _const_spec(pw["w_v"].shape)],
        out_specs=[row(hw), row(hw)],
        out_shape=[jax.ShapeDtypeStruct((n_tok, hw), BF16)] * 2,
        compiler_params=_params(("arbitrary",)),
        name="cache_kv",
    )(ckv, kr, pw["w_k"], pw["w_v"])


def _attn_kernel(*refs, nseg, tk):
    q_ref = refs[0]
    kv = refs[1:1 + 2 * nseg]
    o_ref, s_ref = refs[1 + 2 * nseg:]
    tq = q_ref.shape[1]
    nt = (((1,), (1,)), ((), ()))
    lane = lax.broadcasted_iota(jnp.int32, (1, LANES), 1)
    chunks = [(kv[2 * seg], kv[2 * seg + 1], c * tk)
              for seg in range(nseg) for c in range(kv[2 * seg].shape[1] // tk)]
    outs = []
    for h in range(MLA_HEADS):
        hs = slice(h * LANES, (h + 1) * LANES)
        qh = q_ref[0, :, hs]
        mx = jnp.full((tq, LANES), -jnp.inf, F32)
        for i, (k_ref, _, off) in enumerate(chunks):
            s = lax.dot_general(qh, k_ref[0, off:off + tk, hs], nt, preferred_element_type=F32)
            s_ref[:, i * tk:(i + 1) * tk] = s
            for w in range(tk // LANES):
                mx = jnp.maximum(mx, s[:, w * LANES:(w + 1) * LANES])
        m = jnp.broadcast_to(jnp.max(mx, axis=1, keepdims=True), (tq, LANES))
        mb = jnp.concatenate([m] * (tk // LANES), axis=1)
        acc = jnp.zeros((tq, LANES), F32)
        for i, (_, v_ref, off) in enumerate(chunks):
            p = jnp.exp2(s_ref[:, i * tk:(i + 1) * tk] - mb).astype(BF16)
            acc = acc + jnp.dot(p, v_ref[0, off:off + tk, hs], preferred_element_type=F32)
        outs.append(acc / pltpu.roll(acc, V_DIM, 1))
    for pair in range(MLA_HEADS // 2):
        a = outs[2 * pair]
        b = pltpu.roll(outs[2 * pair + 1], V_DIM, 1)
        o_ref[0, :, pair * LANES:(pair + 1) * LANES] = jnp.where(lane < V_DIM, a, b).astype(BF16)


def _attn_call(q, kvs, *, tq, tk, name):
    b, t, hw = q.shape
    in_specs = [pl.BlockSpec((1, tq, hw), lambda i, j: (i, j, 0))]
    args = [q]
    n_keys = 0
    for k, v in kvs:
        n_keys += k.shape[1]
        for a in (k, v):
            in_specs.append(pl.BlockSpec((1, a.shape[1], hw), lambda i, j: (i, 0, 0),
                                         pipeline_mode=pl.Buffered(1)))
            args.append(a)
    return pl.pallas_call(
        functools.partial(_attn_kernel, nseg=len(kvs), tk=tk),
        grid=(b, t // tq),
        in_specs=in_specs,
        out_specs=pl.BlockSpec((1, tq, MLA_HEADS * V_DIM), lambda i, j: (i, j, 0)),
        out_shape=jax.ShapeDtypeStruct((b, t, MLA_HEADS * V_DIM), BF16),
        scratch_shapes=[pltpu.VMEM((tq, n_keys), F32)],
        compiler_params=_params(("arbitrary", "arbitrary")),
        name=name,
    )(*args)


def _hgrn_kernel(*refs, layer, has_s0):
    it = iter(refs)
    q_ref, zf_ref, zb_ref, v_ref, g_ref, lbp_ref, ghg_ref = (next(it) for _ in range(7))
    if has_s0:
        s0f_ref, s0b_ref = next(it), next(it)
    o_ref, sf_ref, sb_ref, sbst_ref = (next(it) for _ in range(4))

    seq = q_ref.shape[1]
    c_len = HG_CHUNK
    n_chunks = seq // c_len

    lbp = lbp_ref[...]
    e = jnp.exp(lbp - jnp.max(lbp, axis=0, keepdims=True))
    lb = jnp.sum(e[:layer + 1], axis=0) / jnp.sum(e, axis=0)
    lb_f, lb_b = lb[0:1, :], lb[1:2, :]

    row = lax.broadcasted_iota(jnp.int32, (c_len, LANES), 0)
    ti = lax.broadcasted_iota(jnp.int32, (c_len, c_len), 0)
    si = lax.broadcasted_iota(jnp.int32, (c_len, c_len), 1)
    txs = ti ^ si
    lvl = jnp.full((c_len, c_len), -1, jnp.int32)
    for j in range(HG_LEVELS):
        lvl = lvl + jnp.where(txs >= (1 << j), 1, 0)

    def gates(z, lbv):
        f = lbv + (1.0 - lbv) * jax.nn.sigmoid(z)
        return jnp.log(f), 1.0 - f

    def upsweep(x):
        pre, tot = [x], [x]
        for j in range(HG_LEVELS):
            h = 1 << j
            upper = (row & h) != 0
            t = tot[-1]
            sib = jnp.where(upper, pltpu.roll(t, h, 0), pltpu.roll(t, c_len - h, 0))
            tot.append(t + sib)
            pre.append(pre[-1] + jnp.where(upper, sib, 0.0))
        return pre, tot

    nt = (((1,), (1,)), ((), ()))
    tn = (((0,), (0,)), ((), ()))

    def chunk(ref, c):
        return ref[0, pl.ds(pl.multiple_of(c * c_len, c_len), c_len), :]

    def bwd_body(i, st):
        c = n_chunks - 1 - i
        sbst_ref[c] = st
        lg, kb = gates(chunk(zb_ref, c), lb_b)
        pre, tot = upsweep(lg)
        khat = (kb * jnp.exp(pre[HG_LEVELS] - lg)).astype(BF16)
        ut = lax.dot_general(chunk(v_ref, c).astype(BF16), khat, tn, preferred_element_type=F32)
        return jnp.exp(tot[HG_LEVELS][0:1, :]) * st + ut

    st_b0 = s0b_ref[0, 0].T if has_s0 else jnp.zeros((HG_DV, HG_DK), F32)
    st_b = lax.fori_loop(0, n_chunks, bwd_body, st_b0)
    sb_ref[0, 0] = st_b.T

    ghg = ghg_ref[...]

    def fwd_body(c, st):
        q = chunk(q_ref, c)
        vb = chunk(v_ref, c).astype(BF16)
        lgf, kf = gates(chunk(zf_ref, c), lb_f)
        lgb, kb = gates(chunk(zb_ref, c), lb_b)
        pf, tf = upsweep(lgf)
        pb, tb = upsweep(lgb)
        qb = q.astype(BF16)
        a = lax.dot_general(qb, (kf + kb).astype(BF16), nt, preferred_element_type=F32)
        a = jnp.where(lvl == -1, a, 0.0)
        for j in range(HG_LEVELS):
            upper = (row & (1 << j)) != 0
            eq = jnp.exp(jnp.where(upper, pf[j], tb[j] - pb[j] + lgb))
            ek = jnp.exp(jnp.where(upper, pb[j] - lgb, tf[j] - pf[j]))
            qs = (q * eq).astype(BF16)
            ks = (jnp.where(upper, kb, kf) * ek).astype(BF16)
            aj = lax.dot_general(qs, ks, nt, preferred_element_type=F32)
            a = jnp.where(lvl == j, aj, a)
        o = jnp.dot(a.astype(BF16), vb, preferred_element_type=F32)
        qhf = (q * jnp.exp(pf[HG_LEVELS])).astype(BF16)
        qhb = (q * jnp.exp(tb[HG_LEVELS] - pb[HG_LEVELS] + lgb)).astype(BF16)
        o = o + lax.dot_general(qhf, st.astype(BF16), nt, preferred_element_type=F32)
        o = o + lax.dot_general(qhb, sbst_ref[c].astype(BF16), nt, preferred_element_type=F32)
        on = _rms(o, ghg) * chunk(g_ref, c)
        o_ref[0, pl.ds(pl.multiple_of(c * c_len, c_len), c_len), :] = on.astype(BF16)
        khat = (kf * jnp.exp(tf[HG_LEVELS] - pf[HG_LEVELS])).astype(BF16)
        ut = lax.dot_general(vb, khat, tn, preferred_element_type=F32)
        return jnp.exp(tf[HG_LEVELS][0:1, :]) * st + ut

    st_f0 = s0f_ref[0, 0].T if has_s0 else jnp.zeros((HG_DV, HG_DK), F32)
    st_f = lax.fori_loop(0, n_chunks, fwd_body, st_f0)
    sf_ref[0, 0] = st_f.T


def _hgrn_call(hgin, lb_param, g_hg, s0, *, layer, name):
    b, t, _ = hgin.shape
    has_s0 = s0 is not None
    col = lambda g: pl.BlockSpec((1, t, LANES), lambda i, h, g=g: (i, 0, g * HG_HEADS + h))
    in_specs = [col(0), col(1), col(2), col(3), col(4),
                pl.BlockSpec((lb_param.shape[0], 2, LANES), lambda i, h: (0, 0, h)),
                pl.BlockSpec((1, LANES), lambda i, h: (0, h))]
    args = [hgin] * 5 + [lb_param, g_hg]
    st_spec = pl.BlockSpec((1, 1, HG_DK, HG_DV), lambda i, h: (i, h, 0, 0))
    if has_s0:
        in_specs += [st_spec, st_spec]
        args += list(s0)
    st_shape = jax.ShapeDtypeStruct((b, HG_HEADS, HG_DK, HG_DV), F32)
    return pl.pallas_call(
        functools.partial(_hgrn_kernel, layer=layer, has_s0=has_s0),
        grid=(b, HG_HEADS),
        in_specs=in_specs,
        out_specs=[pl.BlockSpec((1, t, LANES), lambda i, h: (i, 0, h)), st_spec, st_spec],
        out_shape=[jax.ShapeDtypeStruct((b, t, HG_WIDTH), BF16), st_shape, st_shape],
        scratch_shapes=[pltpu.VMEM((t // HG_CHUNK, HG_DV, HG_DK), F32)],
        compiler_params=_params(("arbitrary", "arbitrary")),
        name=name,
    )(*args)


def _mix_kernel(om_ref, oh_ref, gate_ref, x_ref, mod_ref, wmo_ref, who_ref, wout_ref, gffn_ref,
                x1_ref, h2_ref):
    a = jnp.dot(om_ref[...], wmo_ref[...], preferred_element_type=F32)
    b = jnp.dot(oh_ref[...], who_ref[...], preferred_element_type=F32)
    mix = gate_ref[:, 0:D_MODEL] * a + gate_ref[:, D_MODEL:2 * D_MODEL] * b
    y = jnp.dot(mix.astype(BF16), wout_ref[...], preferred_element_type=F32)
    gt1 = mod_ref[0, 2:3, :]
    sh2 = mod_ref[0, 3:4, :]
    sc2 = mod_ref[0, 4:5, :]
    x1 = x_ref[...] + gt1 * y
    x1_ref[...] = x1
    h2_ref[...] = (_rms(x1, gffn_ref[...]) * (1.0 + sc2) + sh2).astype(BF16)


def _mix_call(om, oh, gate, x, mod, pw, *, seq_len, mod_row0, per_seq_mod, tm, name):
    n_tok = x.shape[0]
    tps = seq_len // tm
    if per_seq_mod:
        mod_map = lambda t: (mod_row0 + t // tps, 0, 0)
    else:
        mod_map = lambda t: (mod_row0, 0, 0)
    row = lambda w: pl.BlockSpec((tm, w), lambda t: (t, 0))
    ws = [pw[k] for k in ("w_mla_o", "w_hg_o", "w_out", "g_ffn")]
    return pl.pallas_call(
        _mix_kernel,
        grid=(n_tok // tm,),
        in_specs=[row(MLA_HEADS * V_DIM), row(HG_WIDTH), row(2 * D_MODEL), row(D_MODEL),
                  pl.BlockSpec((1, 6, D_MODEL), mod_map)] + [_const_spec(w.shape) for w in ws],
        out_specs=[row(D_MODEL), row(D_MODEL)],
        out_shape=[jax.ShapeDtypeStruct((n_tok, D_MODEL), F32),
                   jax.ShapeDtypeStruct((n_tok, D_MODEL), BF16)],
        compiler_params=_params(("arbitrary",)),
        name=name,
    )(om, oh, gate, x, mod, *ws)


def _gelu_tanh(x):
    return 0.5 * x * (1.0 + jnp.tanh(0.7978845608028654 * (x + 0.044715 * (x * x * x))))


def _ffn_kernel(h2_ref, hp_ref, hn_ref, x1_ref, mod_ref, wg_ref, wv_ref, cw_ref, cb_ref, wo_ref,
                gfin_ref, y_ref, ext_ref, gsc_ref, u_ref, *, tm, seq_len):
    halo = BF16_SUBLANES
    t = pl.program_id(0)
    ext_ref[0:halo, :] = hp_ref[...]
    ext_ref[halo:halo + tm, :] = h2_ref[...]
    ext_ref[halo + tm:2 * halo + tm, :] = hn_ref[...]
    pos = (lax.broadcasted_iota(jnp.int32, (tm, 1), 0) + t * tm) & (seq_len - 1)
    has_prev = pos != 0
    has_next = pos != seq_len - 1
    for j in range(D_FF // FF_CHUNK):
        cs = slice(j * FF_CHUNK, (j + 1) * FF_CHUNK)
        gsc_ref[...] = jnp.dot(ext_ref[...], wg_ref[:, cs], preferred_element_type=F32)
        g_prev = jnp.where(has_prev, gsc_ref[halo - 1:halo - 1 + tm, :], 0.0)
        g_cur = gsc_ref[halo:halo + tm, :]
        g_next = jnp.where(has_next, gsc_ref[halo + 1:halo + 1 + tm, :], 0.0)
        gconv = cb_ref[:, cs] + g_prev * cw_ref[0:1, cs] + g_cur * cw_ref[1:2, cs] + g_next * cw_ref[2:3, cs]
        val = jnp.dot(h2_ref[...], wv_ref[:, cs], preferred_element_type=F32)
        u_ref[:, cs] = (_gelu_tanh(gconv) * val).astype(BF16)
    f = jnp.dot(u_ref[...], wo_ref[...], preferred_element_type=F32)
    gt2 = mod_ref[0, 5:6, :]
    y_ref[...] = _rms(x1_ref[...] + gt2 * f, gfin_ref[...])


def _ffn_call(h2, x1, mod, pw, *, seq_len, mod_row0, per_seq_mod, tm, name):
    n_tok = x1.shape[0]
    tps = seq_len // tm
    halo = BF16_SUBLANES
    n_hblk = n_tok // halo
    hpt = tm // halo
    if per_seq_mod:
        mod_map = lambda t: (mod_row0 + t // tps, 0, 0)
    else:
        mod_map = lambda t: (mod_row0, 0, 0)
    row = lambda w: pl.BlockSpec((tm, w), lambda t: (t, 0))
    ws = [pw[k] for k in ("w_ffn_g", "w_ffn_v", "conv_w", "conv_b", "w_ffn_out", "g_final")]
    return pl.pallas_call(
        functools.partial(_ffn_kernel, tm=tm, seq_len=seq_len),
        grid=(n_tok // tm,),
        in_specs=[row(D_MODEL),
                  pl.BlockSpec((halo, D_MODEL), lambda t: (jnp.maximum(t * hpt - 1, 0), 0)),
                  pl.BlockSpec((halo, D_MODEL), lambda t: (jnp.minimum((t + 1) * hpt, n_hblk - 1), 0)),
                  row(D_MODEL),
                  pl.BlockSpec((1, 6, D_MODEL), mod_map)] + [_const_spec(w.shape) for w in ws],
        out_specs=row(D_MODEL),
        out_shape=jax.ShapeDtypeStruct((n_tok, D_MODEL), F32),
        scratch_shapes=[pltpu.VMEM((tm + 2 * halo, D_MODEL), BF16),
                        pltpu.VMEM((tm + 2 * halo, FF_CHUNK), F32),
                        pltpu.VMEM((tm, D_FF), BF16)],
        compiler_params=_params(("arbitrary",)),
        name=name,
    )(h2, h2, h2, x1, mod, *ws)


def _rope_tables(seq_len):
    rows = seq_len // GRID_W
    rr, cc = jnp.meshgrid(jnp.arange(rows, dtype=F32), jnp.arange(GRID_W, dtype=F32), indexing="ij")
    n_freq = QK_ROPE // 4
    inv = jnp.power(ROPE_BASE, -jnp.arange(n_freq, dtype=F32) / n_freq)
    ang = jnp.concatenate([rr.reshape(-1)[:, None] * inv, cc.reshape(-1)[:, None] * inv], axis=-1)
    cos, sin = jnp.cos(ang), jnp.sin(ang)
    z16 = jnp.zeros_like(sin)
    ones = jnp.ones((seq_len, QK_NOPE), F32)
    zeros = jnp.zeros((seq_len, QK_NOPE), F32)
    s = ATTN_SCALE
    cs = jnp.concatenate([ones * s, cos * s, cos * s, cos, cos], axis=1)
    sa = jnp.concatenate([zeros, -sin * s, z16, -sin, z16], axis=1)
    sb = jnp.concatenate([zeros, z16, sin * s, z16, sin], axis=1)
    return jnp.stack([cs, sa, sb], axis=0)


def _prep_weights(l, g_mix, w_in, g_q, w_uq, g_kv, w_ukv, w_mla_o, g_hg, w_hg_o, w_out, g_ffn,
                  w_ffn_in, conv_w, conv_b, w_ffn_out, g_final):
    d = D_MODEL
    wi = w_in[l]
    c_kr = Q_LORA + KV_LORA
    w_in_p = jnp.concatenate([wi[:, :c_kr], jnp.zeros((d, KR_LANE0), F32),
                              wi[:, c_kr:c_kr + QK_ROPE], wi[:, c_kr + QK_ROPE:]], axis=1)
    hd = QK_NOPE + QK_ROPE
    w_uq_p = jnp.concatenate([w_uq[l].reshape(Q_LORA, MLA_HEADS, hd),
                              jnp.zeros((Q_LORA, MLA_HEADS, LANES - hd), F32)], axis=-1)
    wkv = w_ukv[l].reshape(KV_LORA, MLA_HEADS, QK_NOPE + V_DIM)
    zpad = jnp.zeros((KV_LORA, MLA_HEADS, LANES - QK_NOPE), F32)
    w_k_top = jnp.concatenate([wkv[..., :QK_NOPE], zpad], axis=-1).reshape(KV_LORA, MLA_HEADS * LANES)
    place = jnp.zeros((LANES, LANES), F32).at[
        KR_LANE0 + jnp.arange(QK_ROPE), QK_NOPE + jnp.arange(QK_ROPE)].set(1.0)
    w_k_bot = jnp.tile(place, (1, MLA_HEADS))
    w_k = jnp.concatenate([w_k_top, w_k_bot], axis=0)
    w_v = jnp.concatenate([wkv[..., QK_NOPE:], jnp.zeros((KV_LORA, MLA_HEADS, LANES - V_DIM), F32)],
                          axis=-1).reshape(KV_LORA, MLA_HEADS * LANES)
    return {
        "g_mix": g_mix[l].reshape(1, d), "w_in": w_in_p.astype(BF16),
        "g_q": g_q[l].reshape(1, Q_LORA), "w_uq": w_uq_p.reshape(Q_LORA, MLA_HEADS * LANES).astype(BF16),
        "g_kv": g_kv[l].reshape(1, KV_LORA), "w_k": w_k.astype(BF16), "w_v": w_v.astype(BF16),
        "w_mla_o": w_mla_o[l].astype(BF16), "w_hg_o": w_hg_o[l].astype(BF16),
        "w_out": w_out[l].astype(BF16), "g_ffn": g_ffn[l].reshape(1, d),
        "w_ffn_g": w_ffn_in[l][:, :D_FF].astype(BF16), "w_ffn_v": w_ffn_in[l][:, D_FF:].astype(BF16),
        "conv_w": conv_w[l], "conv_b": conv_b[l].reshape(1, D_FF),
        "w_ffn_out": w_ffn_out[l].astype(BF16), "g_final": g_final.reshape(1, d),
        "g_hg": g_hg[l].reshape(1, HG_WIDTH),
    }


def _trunk(x, mod, tabs, cache, pw, lb_param, *, layer, batch, seq_len, mod_row0, per_seq_mod,
           emit_cache, tm, tq, tag):
    n_tok = batch * seq_len
    hw = MLA_HEADS * LANES
    outs = _proj_call(x, mod, tabs, pw, seq_len=seq_len, mod_row0=mod_row0, per_seq_mod=per_seq_mod,
                      emit_cache=emit_cache, tm=tm)
    q, k, v, hgin, gate = outs[:5]
    kvs = [(k.reshape(batch, seq_len, hw), v.reshape(batch, seq_len, hw))]
    s0 = None
    if cache is not None:
        ckv_c, kr_c, s0f, s0b = cache
        past = ckv_c.shape[1]
        kc, vc = _cachekv_call(ckv_c.reshape(batch * past, KV_LORA), kr_c.reshape(batch * past, QK_ROPE),
                               pw, tm=min(512, batch * past))
        kvs.append((kc.reshape(batch, past, hw), vc.reshape(batch, past, hw)))
        s0 = (s0f, s0b)
    o_mla = _attn_call(q.reshape(batch, seq_len, hw), kvs, tq=tq, tk=min(512, seq_len), name="attn_" + tag)
    o_hg, s_f, s_b = _hgrn_call(hgin.reshape(batch, seq_len, 5 * HG_WIDTH), lb_param, pw["g_hg"], s0,
                                layer=layer, name="hgrn_" + tag)
    common = dict(seq_len=seq_len, mod_row0=mod_row0, per_seq_mod=per_seq_mod, tm=tm)
    x1, h2 = _mix_call(o_mla.reshape(n_tok, MLA_HEADS * V_DIM), o_hg.reshape(n_tok, HG_WIDTH), gate, x,
                       mod, pw, name="mix_" + tag, **common)
    y = _ffn_call(h2, x1, mod, pw, name="ffn_" + tag, **common)
    extra = outs[5:] if emit_cache else None
    return y, extra, s_f, s_b


def kernel(x_prompt, x_sample, cache_ckv, cache_krope, state_hgrn_fwd, state_hgrn_bwd, c, c_ctx, w_mod,
           b_mod, g_mix, w_in, g_q, w_uq, g_kv, w_ukv, w_mla_o, lb_param, g_hg, w_hg_o, w_out, g_ffn,
           w_ffn_in, conv_w, conv_b, w_ffn_out, g_final):
    batch, seq, d = x_prompt.shape
    dec_batch, dec_seq, _ = x_sample.shape
    depth = w_in.shape[0]
    assert depth == 1, "single trunk layer"
    l = 0
    cvec = jnp.concatenate([c_ctx[None, :], c, jnp.zeros((8 - 1 - dec_batch, d), F32)], axis=0)
    mod = _mod_call(cvec, w_mod[l], b_mod[l]).reshape(8, 6, d)
    pw = _prep_weights(l, g_mix, w_in, g_q, w_uq, g_kv, w_ukv, w_mla_o, g_hg, w_hg_o, w_out, g_ffn,
                       w_ffn_in, conv_w, conv_b, w_ffn_out, g_final)
    tabs = _rope_tables(dec_seq)

    y_ctx, cache_out, s_f, s_b = _trunk(
        x_prompt.reshape(batch * seq, d), mod, None, None, pw, lb_param, layer=l, batch=batch,
        seq_len=seq, mod_row0=0, per_seq_mod=False, emit_cache=True, tm=256, tq=256, tag="ctx")
    cache_l = (cache_ckv[:, l], cache_krope[:, l], state_hgrn_fwd[:, l], state_hgrn_bwd[:, l])
    y_lat, _, _, _ = _trunk(
        x_sample.reshape(dec_batch * dec_seq, d), mod, tabs, cache_l, pw, lb_param, layer=l,
        batch=dec_batch, seq_len=dec_seq, mod_row0=1, per_seq_mod=True, emit_cache=False, tm=256,
        tq=256, tag="lat")

    ckv_new, kr_new = cache_out
    return (y_ctx.reshape(batch, seq, d), y_lat.reshape(dec_batch, dec_seq, d),
            ckv_new.reshape(batch, 1, seq, KV_LORA), kr_new.reshape(batch, 1, seq, QK_ROPE),
            s_f.reshape(batch, 1, HG_HEADS, HG_DK, HG_DV), s_b.reshape(batch, 1, HG_HEADS, HG_DK, HG_DV))
```

```python
import functools

import jax
import jax.numpy as jnp
from jax import lax
from jax.experimental import pallas as pl
from jax.experimental.pallas import tpu as pltpu

F32 = jnp.float32
BF16 = jnp.bfloat16

D_MODEL = 1024
GRID_W = 64
MLA_HEADS = 8
QK_NOPE = 64
QK_ROPE = 32
V_DIM = 64
Q_LORA = 384
KV_LORA = 256
ROPE_BASE = 10000.0
HG_HEADS = 4
HG_DK = 128
HG_DV = 128
HG_WIDTH = HG_HEADS * HG_DV
D_FF = 2816
EPS = 1e-6

LANES = 128
SUBLANES = 8
BF16_SUBLANES = 16
HG_CHUNK = 128
HG_LEVELS = 7
HG_UNROLL = 4
FF_CHUNK = 256
TOKEN_TILE = 512
Q_TILE = 512
VMEM_LIMIT = 56 * 1024 * 1024

C_PQ = 0
C_PKV = Q_LORA
C_PKR = C_PKV + KV_LORA
C_HQ = C_PKR + LANES
C_ZF = C_HQ + HG_WIDTH
C_HG = C_ZF + 3 * HG_WIDTH
C_GA = C_HG + HG_WIDTH
C_END = C_GA + 2 * D_MODEL
KR_LANE0 = LANES - QK_ROPE

ATTN_SCALE = (QK_NOPE + QK_ROPE) ** -0.5 * 1.4426950408889634


def _rms(x, g):
    return x * lax.rsqrt(jnp.mean(x * x, axis=-1, keepdims=True) + EPS) * g


def _const_spec(shape):
    n = len(shape)
    return pl.BlockSpec(shape, lambda *_: (0,) * n, pipeline_mode=pl.Buffered(1))


def _params(sem):
    return pltpu.CompilerParams(dimension_semantics=sem, vmem_limit_bytes=VMEM_LIMIT)


def _mod_kernel(c_ref, w_ref, b_ref, o_ref):
    c = c_ref[...]
    a = c * jax.nn.sigmoid(c)
    o_ref[...] = jnp.dot(a, w_ref[...], precision=lax.Precision.HIGHEST,
                         preferred_element_type=F32) + b_ref[...]


def _mod_call(cvec, w_mod, b_mod):
    n = w_mod.shape[1]
    tn = 1024
    return pl.pallas_call(
        _mod_kernel,
        grid=(n // tn,),
        in_specs=[pl.BlockSpec((8, D_MODEL), lambda j: (0, 0)),
                  pl.BlockSpec((D_MODEL, tn), lambda j: (0, j)),
                  pl.BlockSpec((1, tn), lambda j: (0, j))],
        out_specs=pl.BlockSpec((8, tn), lambda j: (0, j)),
        out_shape=jax.ShapeDtypeStruct((8, n), F32),
        compiler_params=_params(("arbitrary",)),
        name="mod",
    )(cvec, w_mod, b_mod.reshape(1, n))


def _proj_kernel(*refs, rope, emit_cache):
    it = iter(refs)
    x_ref = next(it)
    mod_ref = next(it)
    tab_ref = next(it) if rope else None
    gmix_ref, win_ref, gq_ref, wuq_ref, gkv_ref, wk_ref, wv_ref = (next(it) for _ in range(7))
    q_ref, k_ref, v_ref, hg_ref, gate_ref = (next(it) for _ in range(5))
    if emit_cache:
        ckv_ref, kr_ref = next(it), next(it)

    sh1 = mod_ref[0, 0:1, :]
    sc1 = mod_ref[0, 1:2, :]
    hb = (_rms(x_ref[...], gmix_ref[...]) * (1.0 + sc1) + sh1).astype(BF16)

    def proj(c0, c1):
        return jnp.dot(hb, win_ref[:, c0:c1], preferred_element_type=F32)

    if rope:
        cs1, sa1, sb1 = tab_ref[0], tab_ref[1], tab_ref[2]

        def rot(x, reps):
            w = x.shape[1]
            cs = jnp.concatenate([cs1] * reps, axis=1) if reps > 1 else cs1
            sa = jnp.concatenate([sa1] * reps, axis=1) if reps > 1 else sa1
            sb = jnp.concatenate([sb1] * reps, axis=1) if reps > 1 else sb1
            half = QK_ROPE // 2
            return x * cs + pltpu.roll(x, w - half, 1) * sa + pltpu.roll(x, half, 1) * sb

    qn = _rms(proj(C_PQ, C_PKV), gq_ref[...]).astype(BF16)
    q = jnp.dot(qn, wuq_ref[...], preferred_element_type=F32)
    q = rot(q, MLA_HEADS) if rope else q * ATTN_SCALE
    q_ref[...] = q.astype(BF16)

    ckv = _rms(proj(C_PKV, C_PKR), gkv_ref[...])
    pkr = proj(C_PKR, C_HQ)
    if emit_cache:
        ckv_ref[...] = ckv
        kr_ref[...] = pkr[:, KR_LANE0:]
    ckv_b = ckv.astype(BF16)
    kr = rot(pkr, 1) if rope else pkr
    kin = jnp.concatenate([ckv_b, kr.astype(BF16)], axis=1)
    k_ref[...] = jnp.dot(kin, wk_ref[...], preferred_element_type=F32).astype(BF16)
    lane = lax.broadcasted_iota(jnp.int32, (1, MLA_HEADS * LANES), 1)
    ones = jnp.where((lane & (LANES - 1)) >= V_DIM, 1.0, 0.0)
    v_ref[...] = (jnp.dot(ckv_b, wv_ref[...], preferred_element_type=F32) + ones).astype(BF16)

    hq = proj(C_HQ, C_ZF)
    hg_ref[:, 0:HG_WIDTH] = hq * jax.nn.sigmoid(hq)
    hg_ref[:, HG_WIDTH:4 * HG_WIDTH] = proj(C_ZF, C_HG)
    og = proj(C_HG, C_GA)
    hg_ref[:, 4 * HG_WIDTH:5 * HG_WIDTH] = og * jax.nn.sigmoid(og)

    gate_ref[...] = jax.nn.sigmoid(proj(C_GA, C_END)).astype(BF16)


def _proj_call(x, mod, tabs, pw, *, seq_len, mod_row0, per_seq_mod, emit_cache, tm):
    n_tok = x.shape[0]
    tps = max(seq_len // tm, 1)
    rope = tabs is not None
    if per_seq_mod:
        mod_map = lambda t: (mod_row0 + t // tps, 0, 0)
    else:
        mod_map = lambda t: (mod_row0, 0, 0)
    row = lambda w: pl.BlockSpec((tm, w), lambda t: (t, 0))
    in_specs = [row(D_MODEL), pl.BlockSpec((1, 6, D_MODEL), mod_map)]
    args = [x, mod]
    if rope:
        in_specs.append(pl.BlockSpec((3, tm, LANES), lambda t: (0, t % tps, 0)))
        args.append(tabs)
    for name in ("g_mix", "w_in", "g_q", "w_uq", "g_kv", "w_k", "w_v"):
        in_specs.append(_const_spec(pw[name].shape))
        args.append(pw[name])
    hw = MLA_HEADS * LANES
    out_shape = [jax.ShapeDtypeStruct((n_tok, hw), BF16),
                 jax.ShapeDtypeStruct((n_tok, hw), BF16),
                 jax.ShapeDtypeStruct((n_tok, hw), BF16),
                 jax.ShapeDtypeStruct((n_tok, 5 * HG_WIDTH), F32),
                 jax.ShapeDtypeStruct((n_tok, 2 * D_MODEL), BF16)]
    out_specs = [row(hw), row(hw), row(hw), row(5 * HG_WIDTH), row(2 * D_MODEL)]
    if emit_cache:
        out_shape += [jax.ShapeDtypeStruct((n_tok, KV_LORA), F32),
                      jax.ShapeDtypeStruct((n_tok, QK_ROPE), F32)]
        out_specs += [row(KV_LORA), row(QK_ROPE)]
    return pl.pallas_call(
        functools.partial(_proj_kernel, rope=rope, emit_cache=emit_cache),
        grid=(n_tok // tm,),
        in_specs=in_specs,
        out_specs=out_specs,
        out_shape=out_shape,
        compiler_params=_params(("arbitrary",)),
        name="proj_lat" if rope else "proj_ctx",
    )(*args)


def _cachekv_kernel(ckv_ref, kr_ref, wk_ref, wv_ref, k_ref, v_ref):
    ckv_b = ckv_ref[...].astype(BF16)
    tm = ckv_b.shape[0]
    kr = jnp.concatenate([jnp.zeros((tm, KR_LANE0), BF16), kr_ref[...].astype(BF16)], axis=1)
    kin = jnp.concatenate([ckv_b, kr], axis=1)
    k_ref[...] = jnp.dot(kin, wk_ref[...], preferred_element_type=F32).astype(BF16)
    lane = lax.broadcasted_iota(jnp.int32, (1, MLA_HEADS * LANES), 1)
    ones = jnp.where((lane & (LANES - 1)) >= V_DIM, 1.0, 0.0)
    v_ref[...] = (jnp.dot(ckv_b, wv_ref[...], preferred_element_type=F32) + ones).astype(BF16)


def _cachekv_call(ckv, kr, pw, tm):
    n_tok = ckv.shape[0]
    hw = MLA_HEADS * LANES
    row = lambda w: pl.BlockSpec((tm, w), lambda t: (t, 0))
    return pl.pallas_call(
        _cachekv_kernel,
        grid=(n_tok // tm,),
        in_specs=[row(KV_LORA), row(QK_ROPE), _const_spec(pw["w_k"].shape), _const_spec(pw["w_v"].shape)],
        out_specs=[row(hw), row(hw)],
        out_shape=[jax.ShapeDtypeStruct((n_tok, hw), BF16)] * 2,
        compiler_params=_params(("arbitrary",)),
        name="cache_kv",
    )(ckv, kr, pw["w_k"], pw["w_v"])


def _attn_kernel(*refs, nseg, tk):
    q_ref = refs[0]
    kv = refs[1:1 + 2 * nseg]
    o_ref, s_ref = refs[1 + 2 * nseg:]
    tq = q_ref.shape[1]
    nt = (((1,), (1,)), ((), ()))
    lane = lax.broadcasted_iota(jnp.int32, (1, LANES), 1)
    chunks = [(kv[2 * seg], kv[2 * seg + 1], c * tk)
              for seg in range(nseg) for c in range(kv[2 * seg].shape[1] // tk)]
    outs = []
    for h in range(MLA_HEADS):
        hs = slice(h * LANES, (h + 1) * LANES)
        qh = q_ref[0, :, hs]
        mx = jnp.full((tq, LANES), -jnp.inf, F32)
        for i, (k_ref, _, off) in enumerate(chunks):
            s = lax.dot_general(qh, k_ref[0, off:off + tk, hs], nt, preferred_element_type=F32)
            s_ref[:, i * tk:(i + 1) * tk] = s
            for w in range(tk // LANES):
                mx = jnp.maximum(mx, s[:, w * LANES:(w + 1) * LANES])
        m = jnp.broadcast_to(jnp.max(mx, axis=1, keepdims=True), (tq, LANES))
        mb = jnp.concatenate([m] * (tk // LANES), axis=1)
        acc = jnp.zeros((tq, LANES), F32)
        for i, (_, v_ref, off) in enumerate(chunks):
            p = jnp.exp2(s_ref[:, i * tk:(i + 1) * tk] - mb).astype(BF16)
            acc = acc + jnp.dot(p, v_ref[0, off:off + tk, hs], preferred_element_type=F32)
        outs.append(acc / pltpu.roll(acc, V_DIM, 1))
    for pair in range(MLA_HEADS // 2):
        a = outs[2 * pair]
        b = pltpu.roll(outs[2 * pair + 1], V_DIM, 1)
        o_ref[0, :, pair * LANES:(pair + 1) * LANES] = jnp.where(lane < V_DIM, a, b).astype(BF16)


def _attn_call(q, kvs, *, tq, tk, name):
    b, t, hw = q.shape
    in_specs = [pl.BlockSpec((1, tq, hw), lambda i, j: (i, j, 0))]
    args = [q]
    n_keys = 0
    for k, v in kvs:
        n_keys += k.shape[1]
        for a in (k, v):
            mode = dict(pipeline_mode=pl.Buffered(1)) if t // tq > 1 else {}
            in_specs.append(pl.BlockSpec((1, a.shape[1], hw), lambda i, j: (i, 0, 0), **mode))
            args.append(a)
    return pl.pallas_call(
        functools.partial(_attn_kernel, nseg=len(kvs), tk=tk),
        grid=(b, t // tq),
        in_specs=in_specs,
        out_specs=pl.BlockSpec((1, tq, MLA_HEADS * V_DIM), lambda i, j: (i, j, 0)),
        out_shape=jax.ShapeDtypeStruct((b, t, MLA_HEADS * V_DIM), BF16),
        scratch_shapes=[pltpu.VMEM((tq, n_keys), F32)],
        compiler_params=_params(("arbitrary", "arbitrary")),
        name=name,
    )(*args)


def _hgrn_kernel(*refs, layer, has_s0):
    it = iter(refs)
    q_ref, zf_ref, zb_ref, v_ref, g_ref, lbp_ref, ghg_ref = (next(it) for _ in range(7))
    if has_s0:
        s0f_ref, s0b_ref = next(it), next(it)
    o_ref, sf_ref, sb_ref, sbst_ref, lgb_ref, kbs_ref = (next(it) for _ in range(6))

    seq = q_ref.shape[1]
    c_len = HG_CHUNK
    n_chunks = seq // c_len
    sub = SUBLANES

    lbp = lbp_ref[...]
    e = jnp.exp(lbp - jnp.max(lbp, axis=0, keepdims=True))
    lb = jnp.sum(e[:layer + 1], axis=0) / jnp.sum(e, axis=0)
    lb_f, lb_b = lb[0:1, :], lb[1:2, :]

    row8 = lax.broadcasted_iota(jnp.int32, (1, sub, LANES), 1)
    upper8 = [(row8 & (1 << j)) != 0 for j in range(3)]

    def sel8(j, a, b):
        shape3 = (c_len // sub, sub, LANES)
        return jnp.where(upper8[j], a.reshape(shape3), b.reshape(shape3)).reshape(c_len, LANES)

    ti = lax.broadcasted_iota(jnp.int32, (c_len, c_len), 0)
    si = lax.broadcasted_iota(jnp.int32, (c_len, c_len), 1)
    txs = ti ^ si
    lvl = jnp.full((c_len, c_len), -1, jnp.int32)
    for j in range(HG_LEVELS):
        lvl = lvl + jnp.where(txs >= (1 << j), 1, 0)

    def gates(z, lbv):
        f = lbv + (1.0 - lbv) * jax.nn.sigmoid(z)
        return jnp.log2(f), 1.0 - f

    def rep(slab, n_rows):
        return slab if n_rows == sub else jnp.concatenate([slab] * (n_rows // sub), axis=0)

    def halves(j):
        h = 1 << j
        return [(slice(b, b + h), slice(b + h, b + 2 * h)) for b in range(0, c_len, 2 * h)]

    def upsweep(x, inclusive):
        x3 = x.reshape(c_len // sub, sub, LANES)
        pre3 = [x3 if inclusive else jnp.zeros_like(x3)]
        tot3 = [x3]
        for j in range(3):
            h = 1 << j
            t = tot3[-1]
            if 2 * h == sub:
                sib = pltpu.roll(t, h, 1)
            else:
                sib = jnp.where(upper8[j], pltpu.roll(t, h, 1), pltpu.roll(t, sub - h, 1))
            tot3.append(t + sib)
            pre3.append(pre3[-1] + jnp.where(upper8[j], sib, 0.0))
        pre = [p.reshape(c_len, LANES) for p in pre3]
        tot = [t.reshape(c_len, LANES) for t in tot3]
        slabs = {3: [tot[3][r:r + sub] for r in range(0, c_len, sub)]}
        for j in range(3, HG_LEVELS):
            h = 1 << j
            p, s = pre[-1], slabs[j]
            parts = []
            for i, (lo, up) in enumerate(halves(j)):
                parts += [p[lo], p[up] + rep(s[2 * i], h)]
            pre.append(jnp.concatenate(parts, axis=0))
            slabs[j + 1] = [s[2 * i] + s[2 * i + 1] for i in range(len(s) // 2)]
        return pre, tot, slabs

    nt = (((1,), (1,)), ((), ()))
    tn = (((0,), (0,)), ((), ()))
    top = HG_LEVELS

    def rows_of(c):
        return pl.ds(pl.multiple_of(c * c_len, c_len), c_len)

    def bwd_body(i, st):
        c = n_chunks - 1 - i
        sbst_ref[c] = st
        lg, kb = gates(zb_ref[0, rows_of(c), :], lb_b)
        lgb_ref[rows_of(c), :] = lg
        kbs_ref[rows_of(c), :] = kb
        xb, _, sl = upsweep(lg, False)
        khat = (kb * jnp.exp2(xb[top])).astype(BF16)
        ut = lax.dot_general(v_ref[0, rows_of(c), :].astype(BF16), khat, tn, preferred_element_type=F32)
        return jnp.exp2(sl[top][0][0:1, :]) * st + ut

    st_b0 = s0b_ref[0, 0].T if has_s0 else jnp.zeros((HG_DV, HG_DK), F32)
    st_b = lax.fori_loop(0, n_chunks, bwd_body, st_b0, unroll=min(HG_UNROLL, n_chunks))
    sb_ref[0, 0] = st_b.T

    ghg = ghg_ref[...]

    def fwd_body(c, st):
        q = q_ref[0, rows_of(c), :]
        vb = v_ref[0, rows_of(c), :].astype(BF16)
        lgf, kf = gates(zf_ref[0, rows_of(c), :], lb_f)
        lgb, kb = lgb_ref[rows_of(c), :], kbs_ref[rows_of(c), :]
        pf, tf, sf = upsweep(lgf, True)
        xb, tb, sb = upsweep(lgb, False)
        a = lax.dot_general(q.astype(BF16), (kf + kb).astype(BF16), nt, preferred_element_type=F32)
        a = jnp.where(lvl == -1, a, 0.0)
        for j in range(HG_LEVELS):
            h = 1 << j
            if j < 3:
                eq = sel8(j, pf[j], tb[j] - xb[j])
                ek = sel8(j, xb[j], tf[j] - pf[j])
                ksel = sel8(j, kb, kf)
            else:
                eqs, eks, kss = [], [], []
                for i, (lo, up) in enumerate(halves(j)):
                    eqs += [rep(sb[j][2 * i], h) - xb[j][lo], pf[j][up]]
                    eks += [rep(sf[j][2 * i], h) - pf[j][lo], xb[j][up]]
                    kss += [kf[lo], kb[up]]
                eq, ek, ksel = (jnp.concatenate(p, axis=0) for p in (eqs, eks, kss))
            qs = (q * jnp.exp2(eq)).astype(BF16)
            ks = (ksel * jnp.exp2(ek)).astype(BF16)
            aj = lax.dot_general(qs, ks, nt, preferred_element_type=F32)
            a = jnp.where(lvl == j, aj, a)
        o = jnp.dot(a.astype(BF16), vb, preferred_element_type=F32)
        qhf = (q * jnp.exp2(pf[top])).astype(BF16)
        qhb = (q * jnp.exp2(rep(sb[top][0], c_len) - xb[top])).astype(BF16)
        o = o + lax.dot_general(qhf, st.astype(BF16), nt, preferred_element_type=F32)
        o = o + lax.dot_general(qhb, sbst_ref[c].astype(BF16), nt, preferred_element_type=F32)
        on = _rms(o, ghg) * g_ref[0, rows_of(c), :]
        o_ref[0, rows_of(c), :] = on.astype(BF16)
        khat = (kf * jnp.exp2(rep(sf[top][0], c_len) - pf[top])).astype(BF16)
        ut = lax.dot_general(vb, khat, tn, preferred_element_type=F32)
        return jnp.exp2(sf[top][0][0:1, :]) * st + ut

    st_f0 = s0f_ref[0, 0].T if has_s0 else jnp.zeros((HG_DV, HG_DK), F32)
    st_f = lax.fori_loop(0, n_chunks, fwd_body, st_f0, unroll=min(HG_UNROLL, n_chunks))
    sf_ref[0, 0] = st_f.T


def _hgrn_call(hgin, lb_param, g_hg, s0, *, layer, name):
    b, t, _ = hgin.shape
    has_s0 = s0 is not None
    col = lambda g: pl.BlockSpec((1, t, LANES), lambda i, h, g=g: (i, 0, g * HG_HEADS + h))
    in_specs = [col(0), col(1), col(2), col(3), col(4),
                pl.BlockSpec((lb_param.shape[0], 2, LANES), lambda i, h: (0, 0, h)),
                pl.BlockSpec((1, LANES), lambda i, h: (0, h))]
    args = [hgin] * 5 + [lb_param, g_hg]
    st_spec = pl.BlockSpec((1, 1, HG_DK, HG_DV), lambda i, h: (i, h, 0, 0))
    if has_s0:
        in_specs += [st_spec, st_spec]
        args += list(s0)
    st_shape = jax.ShapeDtypeStruct((b, HG_HEADS, HG_DK, HG_DV), F32)
    return pl.pallas_call(
        functools.partial(_hgrn_kernel, layer=layer, has_s0=has_s0),
        grid=(b, HG_HEADS),
        in_specs=in_specs,
        out_specs=[pl.BlockSpec((1, t, LANES), lambda i, h: (i, 0, h)), st_spec, st_spec],
        out_shape=[jax.ShapeDtypeStruct((b, t, HG_WIDTH), BF16), st_shape, st_shape],
        scratch_shapes=[pltpu.VMEM((t // HG_CHUNK, HG_DV, HG_DK), F32),
                        pltpu.VMEM((t, LANES), F32), pltpu.VMEM((t, LANES), F32)],
        compiler_params=_params(("arbitrary", "arbitrary")),
        name=name,
    )(*args)


def _mix_kernel(om_ref, oh_ref, gate_ref, x_ref, mod_ref, wmo_ref, who_ref, wout_ref, gffn_ref,
                x1_ref, h2_ref):
    a = jnp.dot(om_ref[...], wmo_ref[...], preferred_element_type=F32)
    b = jnp.dot(oh_ref[...], who_ref[...], preferred_element_type=F32)
    mix = gate_ref[:, 0:D_MODEL] * a + gate_ref[:, D_MODEL:2 * D_MODEL] * b
    y = jnp.dot(mix.astype(BF16), wout_ref[...], preferred_element_type=F32)
    gt1 = mod_ref[0, 2:3, :]
    sh2 = mod_ref[0, 3:4, :]
    sc2 = mod_ref[0, 4:5, :]
    x1 = x_ref[...] + gt1 * y
    x1_ref[...] = x1
    h2_ref[...] = (_rms(x1, gffn_ref[...]) * (1.0 + sc2) + sh2).astype(BF16)


def _mix_call(om, oh, gate, x, mod, pw, *, seq_len, mod_row0, per_seq_mod, tm, name):
    n_tok = x.shape[0]
    tps = max(seq_len // tm, 1)
    if per_seq_mod:
        mod_map = lambda t: (mod_row0 + t // tps, 0, 0)
    else:
        mod_map = lambda t: (mod_row0, 0, 0)
    row = lambda w: pl.BlockSpec((tm, w), lambda t: (t, 0))
    ws = [pw[k] for k in ("w_mla_o", "w_hg_o", "w_out", "g_ffn")]
    return pl.pallas_call(
        _mix_kernel,
        grid=(n_tok // tm,),
        in_specs=[row(MLA_HEADS * V_DIM), row(HG_WIDTH), row(2 * D_MODEL), row(D_MODEL),
                  pl.BlockSpec((1, 6, D_MODEL), mod_map)] + [_const_spec(w.shape) for w in ws],
        out_specs=[row(D_MODEL), row(D_MODEL)],
        out_shape=[jax.ShapeDtypeStruct((n_tok, D_MODEL), F32),
                   jax.ShapeDtypeStruct((n_tok, D_MODEL), BF16)],
        compiler_params=_params(("arbitrary",)),
        name=name,
    )(om, oh, gate, x, mod, *ws)


def _gelu_tanh(x):
    return 0.5 * x * (1.0 + jnp.tanh(0.7978845608028654 * (x + 0.044715 * (x * x * x))))


def _ffn_kernel(h2_ref, hp_ref, hn_ref, x1_ref, mod_ref, wg_ref, wv_ref, cw_ref, cb_ref, wo_ref,
                gfin_ref, y_ref, ext_ref, gsc_ref, u_ref, *, tm, seq_len):
    halo = BF16_SUBLANES
    t = pl.program_id(0)
    ext_ref[0:halo, :] = hp_ref[...]
    ext_ref[halo:halo + tm, :] = h2_ref[...]
    ext_ref[halo + tm:2 * halo + tm, :] = hn_ref[...]
    pos = (lax.broadcasted_iota(jnp.int32, (tm, 1), 0) + t * tm) & (seq_len - 1)
    has_prev = pos != 0
    has_next = pos != seq_len - 1
    for j in range(D_FF // FF_CHUNK):
        cs = slice(j * FF_CHUNK, (j + 1) * FF_CHUNK)
        gsc_ref[...] = jnp.dot(ext_ref[...], wg_ref[:, cs], preferred_element_type=F32)
        g_prev = jnp.where(has_prev, gsc_ref[halo - 1:halo - 1 + tm, :], 0.0)
        g_cur = gsc_ref[halo:halo + tm, :]
        g_next = jnp.where(has_next, gsc_ref[halo + 1:halo + 1 + tm, :], 0.0)
        gconv = cb_ref[:, cs] + g_prev * cw_ref[0:1, cs] + g_cur * cw_ref[1:2, cs] + g_next * cw_ref[2:3, cs]
        val = jnp.dot(h2_ref[...], wv_ref[:, cs], preferred_element_type=F32)
        u_ref[:, cs] = (_gelu_tanh(gconv) * val).astype(BF16)
    f = jnp.dot(u_ref[...], wo_ref[...], preferred_element_type=F32)
    gt2 = mod_ref[0, 5:6, :]
    y_ref[...] = _rms(x1_ref[...] + gt2 * f, gfin_ref[...])


def _ffn_call(h2, x1, mod, pw, *, seq_len, mod_row0, per_seq_mod, tm, name):
    n_tok = x1.shape[0]
    tps = max(seq_len // tm, 1)
    halo = BF16_SUBLANES
    n_hblk = n_tok // halo
    hpt = tm // halo
    if per_seq_mod:
        mod_map = lambda t: (mod_row0 + t // tps, 0, 0)
    else:
        mod_map = lambda t: (mod_row0, 0, 0)
    row = lambda w: pl.BlockSpec((tm, w), lambda t: (t, 0))
    ws = [pw[k] for k in ("w_ffn_g", "w_ffn_v", "conv_w", "conv_b", "w_ffn_out", "g_final")]
    return pl.pallas_call(
        functools.partial(_ffn_kernel, tm=tm, seq_len=seq_len),
        grid=(n_tok // tm,),
        in_specs=[row(D_MODEL),
                  pl.BlockSpec((halo, D_MODEL), lambda t: (jnp.maximum(t * hpt - 1, 0), 0)),
                  pl.BlockSpec((halo, D_MODEL), lambda t: (jnp.minimum((t + 1) * hpt, n_hblk - 1), 0)),
                  row(D_MODEL),
                  pl.BlockSpec((1, 6, D_MODEL), mod_map)] + [_const_spec(w.shape) for w in ws],
        out_specs=row(D_MODEL),
        out_shape=jax.ShapeDtypeStruct((n_tok, D_MODEL), F32),
        scratch_shapes=[pltpu.VMEM((tm + 2 * halo, D_MODEL), BF16),
                        pltpu.VMEM((tm + 2 * halo, FF_CHUNK), F32),
                        pltpu.VMEM((tm, D_FF), BF16)],
        compiler_params=_params(("arbitrary",)),
        name=name,
    )(h2, h2, h2, x1, mod, *ws)


def _rope_tables(seq_len):
    rows = seq_len // GRID_W
    rr, cc = jnp.meshgrid(jnp.arange(rows, dtype=F32), jnp.arange(GRID_W, dtype=F32), indexing="ij")
    n_freq = QK_ROPE // 4
    inv = jnp.power(ROPE_BASE, -jnp.arange(n_freq, dtype=F32) / n_freq)
    ang = jnp.concatenate([rr.reshape(-1)[:, None] * inv, cc.reshape(-1)[:, None] * inv], axis=-1)
    cos, sin = jnp.cos(ang), jnp.sin(ang)
    z16 = jnp.zeros_like(sin)
    ones = jnp.ones((seq_len, QK_NOPE), F32)
    zeros = jnp.zeros((seq_len, QK_NOPE), F32)
    s = ATTN_SCALE
    cs = jnp.concatenate([ones * s, cos * s, cos * s, cos, cos], axis=1)
    sa = jnp.concatenate([zeros, -sin * s, z16, -sin, z16], axis=1)
    sb = jnp.concatenate([zeros, z16, sin * s, z16, sin], axis=1)
    return jnp.stack([cs, sa, sb], axis=0)


def _prep_weights(l, g_mix, w_in, g_q, w_uq, g_kv, w_ukv, w_mla_o, g_hg, w_hg_o, w_out, g_ffn,
                  w_ffn_in, conv_w, conv_b, w_ffn_out, g_final):
    d = D_MODEL
    wi = w_in[l]
    c_kr = Q_LORA + KV_LORA
    w_in_p = jnp.concatenate([wi[:, :c_kr], jnp.zeros((d, KR_LANE0), F32),
                              wi[:, c_kr:c_kr + QK_ROPE], wi[:, c_kr + QK_ROPE:]], axis=1)
    hd = QK_NOPE + QK_ROPE
    w_uq_p = jnp.concatenate([w_uq[l].reshape(Q_LORA, MLA_HEADS, hd),
                              jnp.zeros((Q_LORA, MLA_HEADS, LANES - hd), F32)], axis=-1)
    wkv = w_ukv[l].reshape(KV_LORA, MLA_HEADS, QK_NOPE + V_DIM)
    zpad = jnp.zeros((KV_LORA, MLA_HEADS, LANES - QK_NOPE), F32)
    w_k_top = jnp.concatenate([wkv[..., :QK_NOPE], zpad], axis=-1).reshape(KV_LORA, MLA_HEADS * LANES)
    place = jnp.zeros((LANES, LANES), F32).at[
        KR_LANE0 + jnp.arange(QK_ROPE), QK_NOPE + jnp.arange(QK_ROPE)].set(1.0)
    w_k_bot = jnp.tile(place, (1, MLA_HEADS))
    w_k = jnp.concatenate([w_k_top, w_k_bot], axis=0)
    w_v = jnp.concatenate([wkv[..., QK_NOPE:], jnp.zeros((KV_LORA, MLA_HEADS, LANES - V_DIM), F32)],
                          axis=-1).reshape(KV_LORA, MLA_HEADS * LANES)
    return {
        "g_mix": g_mix[l].reshape(1, d), "w_in": w_in_p.astype(BF16),
        "g_q": g_q[l].reshape(1, Q_LORA), "w_uq": w_uq_p.reshape(Q_LORA, MLA_HEADS * LANES).astype(BF16),
        "g_kv": g_kv[l].reshape(1, KV_LORA), "w_k": w_k.astype(BF16), "w_v": w_v.astype(BF16),
        "w_mla_o": w_mla_o[l].astype(BF16), "w_hg_o": w_hg_o[l].astype(BF16),
        "w_out": w_out[l].astype(BF16), "g_ffn": g_ffn[l].reshape(1, d),
        "w_ffn_g": w_ffn_in[l][:, :D_FF].astype(BF16), "w_ffn_v": w_ffn_in[l][:, D_FF:].astype(BF16),
        "conv_w": conv_w[l], "conv_b": conv_b[l].reshape(1, D_FF),
        "w_ffn_out": w_ffn_out[l].astype(BF16), "g_final": g_final.reshape(1, d),
        "g_hg": g_hg[l].reshape(1, HG_WIDTH),
    }


def _trunk(x, mod, tabs, cache, pw, lb_param, *, layer, batch, seq_len, mod_row0, per_seq_mod,
           emit_cache, tm, tq, tag):
    n_tok = batch * seq_len
    hw = MLA_HEADS * LANES
    outs = _proj_call(x, mod, tabs, pw, seq_len=seq_len, mod_row0=mod_row0, per_seq_mod=per_seq_mod,
                      emit_cache=emit_cache, tm=tm)
    q, k, v, hgin, gate = outs[:5]
    kvs = [(k.reshape(batch, seq_len, hw), v.reshape(batch, seq_len, hw))]
    s0 = None
    if cache is not None:
        ckv_c, kr_c, s0f, s0b = cache
        past = ckv_c.shape[1]
        kc, vc = _cachekv_call(ckv_c.reshape(batch * past, KV_LORA), kr_c.reshape(batch * past, QK_ROPE),
                               pw, tm=min(512, batch * past))
        kvs.append((kc.reshape(batch, past, hw), vc.reshape(batch, past, hw)))
        s0 = (s0f, s0b)
    o_mla = _attn_call(q.reshape(batch, seq_len, hw), kvs, tq=tq, tk=min(512, seq_len), name="attn_" + tag)
    o_hg, s_f, s_b = _hgrn_call(hgin.reshape(batch, seq_len, 5 * HG_WIDTH), lb_param, pw["g_hg"], s0,
                                layer=layer, name="hgrn_" + tag)
    common = dict(seq_len=seq_len, mod_row0=mod_row0, per_seq_mod=per_seq_mod, tm=tm)
    x1, h2 = _mix_call(o_mla.reshape(n_tok, MLA_HEADS * V_DIM), o_hg.reshape(n_tok, HG_WIDTH), gate, x,
                       mod, pw, name="mix_" + tag, **common)
    y = _ffn_call(h2, x1, mod, pw, name="ffn_" + tag, **common)
    extra = outs[5:] if emit_cache else None
    return y, extra, s_f, s_b


def kernel(x_prompt, x_sample, cache_ckv, cache_krope, state_hgrn_fwd, state_hgrn_bwd, c, c_ctx, w_mod,
           b_mod, g_mix, w_in, g_q, w_uq, g_kv, w_ukv, w_mla_o, lb_param, g_hg, w_hg_o, w_out, g_ffn,
           w_ffn_in, conv_w, conv_b, w_ffn_out, g_final):
    batch, seq, d = x_prompt.shape
    dec_batch, dec_seq, _ = x_sample.shape
    depth = w_in.shape[0]
    assert depth == 1, "single trunk layer"
    l = 0
    cvec = jnp.concatenate([c_ctx[None, :], c, jnp.zeros((8 - 1 - dec_batch, d), F32)], axis=0)
    mod = _mod_call(cvec, w_mod[l], b_mod[l]).reshape(8, 6, d)
    pw = _prep_weights(l, g_mix, w_in, g_q, w_uq, g_kv, w_ukv, w_mla_o, g_hg, w_hg_o, w_out, g_ffn,
                       w_ffn_in, conv_w, conv_b, w_ffn_out, g_final)
    tabs = _rope_tables(dec_seq)

    y_ctx, cache_out, s_f, s_b = _trunk(
        x_prompt.reshape(batch * seq, d), mod, None, None, pw, lb_param, layer=l, batch=batch,
        seq_len=seq, mod_row0=0, per_seq_mod=False, emit_cache=True, tm=TOKEN_TILE, tq=min(Q_TILE, seq),
        tag="ctx")
    cache_l = (cache_ckv[:, l], cache_krope[:, l], state_hgrn_fwd[:, l], state_hgrn_bwd[:, l])
    y_lat, _, _, _ = _trunk(
        x_sample.reshape(dec_batch * dec_seq, d), mod, tabs, cache_l, pw, lb_param, layer=l,
        batch=dec_batch, seq_len=dec_seq, mod_row0=1, per_seq_mod=True, emit_cache=False, tm=TOKEN_TILE,
        tq=min(Q_TILE, dec_seq), tag="lat")

    ckv_new, kr_new = cache_out
    return (y_ctx.reshape(batch, seq, d), y_lat.reshape(dec_batch, dec_seq, d),
            ckv_new.reshape(batch, 1, seq, KV_LORA), kr_new.reshape(batch, 1, seq, QK_ROPE),
            s_f.reshape(batch, 1, HG_HEADS, HG_DK, HG_DV), s_b.reshape(batch, 1, HG_HEADS, HG_DK, HG_DV))
```

```python
import functools

import jax
import jax.numpy as jnp
from jax import lax
from jax.experimental import pallas as pl
from jax.experimental.pallas import tpu as pltpu

F32 = jnp.float32
BF16 = jnp.bfloat16

D_MODEL = 1024
GRID_W = 64
MLA_HEADS = 8
QK_NOPE = 64
QK_ROPE = 32
V_DIM = 64
Q_LORA = 384
KV_LORA = 256
ROPE_BASE = 10000.0
HG_HEADS = 4
HG_DK = 128
HG_DV = 128
HG_WIDTH = HG_HEADS * HG_DV
D_FF = 2816
EPS = 1e-6

LANES = 128
SUBLANES = 8
BF16_SUBLANES = 16
HG_CHUNK = 128
HG_LEVELS = 7
HG_UNROLL = 4
FF_CHUNK = 256
TOKEN_TILE = 512
Q_TILE = 512
ATTN_HEADS_PER_STEP = 4
VMEM_LIMIT = 56 * 1024 * 1024

C_PQ = 0
C_PKV = Q_LORA
C_PKR = C_PKV + KV_LORA
C_HQ = C_PKR + LANES
C_ZF = C_HQ + HG_WIDTH
C_HG = C_ZF + 3 * HG_WIDTH
C_GA = C_HG + HG_WIDTH
C_END = C_GA + 2 * D_MODEL
KR_LANE0 = LANES - QK_ROPE

ATTN_SCALE = (QK_NOPE + QK_ROPE) ** -0.5 * 1.4426950408889634

NT = (((1,), (1,)), ((), ()))
TN = (((0,), (0,)), ((), ()))


def _rms(x, g):
    return x * lax.rsqrt(jnp.mean(x * x, axis=-1, keepdims=True) + EPS) * g


def _const_spec(shape):
    n = len(shape)
    return pl.BlockSpec(shape, lambda *_: (0,) * n, pipeline_mode=pl.Buffered(1))


def _params(sem):
    return pltpu.CompilerParams(dimension_semantics=sem, vmem_limit_bytes=VMEM_LIMIT)


def _mod_kernel(c_ref, w_ref, b_ref, o_ref):
    c = c_ref[...]
    a = c * jax.nn.sigmoid(c)
    o_ref[...] = jnp.dot(a, w_ref[...], precision=lax.Precision.HIGHEST,
                         preferred_element_type=F32) + b_ref[...]


def _mod_call(cvec, w_mod, b_mod):
    n = w_mod.shape[1]
    tn = 1024
    return pl.pallas_call(
        _mod_kernel,
        grid=(n // tn,),
        in_specs=[pl.BlockSpec((8, D_MODEL), lambda j: (0, 0)),
                  pl.BlockSpec((D_MODEL, tn), lambda j: (0, j)),
                  pl.BlockSpec((1, tn), lambda j: (0, j))],
        out_specs=pl.BlockSpec((8, tn), lambda j: (0, j)),
        out_shape=jax.ShapeDtypeStruct((8, n), F32),
        compiler_params=_params(("arbitrary",)),
        name="mod",
    )(cvec, w_mod, b_mod.reshape(1, n))


def _proj_kernel(*refs, rope, emit_cache):
    it = iter(refs)
    x_ref = next(it)
    mod_ref = next(it)
    tab_ref = next(it) if rope else None
    gmix_ref, win_ref, gq_ref, wuq_ref, gkv_ref, wk_ref, wv_ref = (next(it) for _ in range(7))
    q_ref, k_ref, v_ref, hg_ref, gate_ref = (next(it) for _ in range(5))
    if emit_cache:
        ckv_ref, kr_ref = next(it), next(it)

    sh1 = mod_ref[0, 0:1, :]
    sc1 = mod_ref[0, 1:2, :]
    hb = (_rms(x_ref[...], gmix_ref[...]) * (1.0 + sc1) + sh1).astype(BF16)

    def proj(c0, c1):
        return jnp.dot(hb, win_ref[:, c0:c1], preferred_element_type=F32)

    if rope:
        cs1, sa1, sb1 = tab_ref[0], tab_ref[1], tab_ref[2]

        def rot(x, reps):
            w = x.shape[1]
            cs = jnp.concatenate([cs1] * reps, axis=1) if reps > 1 else cs1
            sa = jnp.concatenate([sa1] * reps, axis=1) if reps > 1 else sa1
            sb = jnp.concatenate([sb1] * reps, axis=1) if reps > 1 else sb1
            half = QK_ROPE // 2
            return x * cs + pltpu.roll(x, w - half, 1) * sa + pltpu.roll(x, half, 1) * sb

    qn = _rms(proj(C_PQ, C_PKV), gq_ref[...]).astype(BF16)
    q = jnp.dot(qn, wuq_ref[...], preferred_element_type=F32)
    q = rot(q, MLA_HEADS) if rope else q * ATTN_SCALE
    q_ref[...] = q.astype(BF16)

    ckv = _rms(proj(C_PKV, C_PKR), gkv_ref[...])
    pkr = proj(C_PKR, C_HQ)
    if emit_cache:
        ckv_ref[...] = ckv
        kr_ref[...] = pkr[:, KR_LANE0:]
    ckv_b = ckv.astype(BF16)
    kr = rot(pkr, 1) if rope else pkr
    kin = jnp.concatenate([ckv_b, kr.astype(BF16)], axis=1)
    k_ref[...] = jnp.dot(kin, wk_ref[...], preferred_element_type=F32).astype(BF16)
    lane = lax.broadcasted_iota(jnp.int32, (1, MLA_HEADS * LANES), 1)
    ones = jnp.where((lane & (LANES - 1)) >= V_DIM, 1.0, 0.0)
    v_ref[...] = (jnp.dot(ckv_b, wv_ref[...], preferred_element_type=F32) + ones).astype(BF16)

    hq = proj(C_HQ, C_ZF)
    hg_ref[:, 0:HG_WIDTH] = hq * jax.nn.sigmoid(hq)
    hg_ref[:, HG_WIDTH:4 * HG_WIDTH] = proj(C_ZF, C_HG)
    og = proj(C_HG, C_GA)
    hg_ref[:, 4 * HG_WIDTH:5 * HG_WIDTH] = og * jax.nn.sigmoid(og)

    gate_ref[...] = jax.nn.sigmoid(proj(C_GA, C_END)).astype(BF16)


def _proj_call(x, mod, tabs, pw, *, seq_len, mod_row0, per_seq_mod, emit_cache, tm):
    n_tok = x.shape[0]
    tps = max(seq_len // tm, 1)
    rope = tabs is not None
    if per_seq_mod:
        mod_map = lambda t: (mod_row0 + t // tps, 0, 0)
    else:
        mod_map = lambda t: (mod_row0, 0, 0)
    row = lambda w: pl.BlockSpec((tm, w), lambda t: (t, 0))
    in_specs = [row(D_MODEL), pl.BlockSpec((1, 6, D_MODEL), mod_map)]
    args = [x, mod]
    if rope:
        in_specs.append(pl.BlockSpec((3, tm, LANES), lambda t: (0, t % tps, 0)))
        args.append(tabs)
    for name in ("g_mix", "w_in", "g_q", "w_uq", "g_kv", "w_k", "w_v"):
        in_specs.append(_const_spec(pw[name].shape))
        args.append(pw[name])
    hw = MLA_HEADS * LANES
    out_shape = [jax.ShapeDtypeStruct((n_tok, hw), BF16),
                 jax.ShapeDtypeStruct((n_tok, hw), BF16),
                 jax.ShapeDtypeStruct((n_tok, hw), BF16),
                 jax.ShapeDtypeStruct((n_tok, 5 * HG_WIDTH), F32),
                 jax.ShapeDtypeStruct((n_tok, 2 * D_MODEL), BF16)]
    out_specs = [row(hw), row(hw), row(hw), row(5 * HG_WIDTH), row(2 * D_MODEL)]
    if emit_cache:
        out_shape += [jax.ShapeDtypeStruct((n_tok, KV_LORA), F32),
                      jax.ShapeDtypeStruct((n_tok, QK_ROPE), F32)]
        out_specs += [row(KV_LORA), row(QK_ROPE)]
    return pl.pallas_call(
        functools.partial(_proj_kernel, rope=rope, emit_cache=emit_cache),
        grid=(n_tok // tm,),
        in_specs=in_specs,
        out_specs=out_specs,
        out_shape=out_shape,
        compiler_params=_params(("arbitrary",)),
        name="proj_lat" if rope else "proj_ctx",
    )(*args)


def _cachekv_kernel(ckv_ref, kr_ref, wk_ref, wv_ref, k_ref, v_ref):
    ckv_b = ckv_ref[...].astype(BF16)
    tm = ckv_b.shape[0]
    kr = jnp.concatenate([jnp.zeros((tm, KR_LANE0), BF16), kr_ref[...].astype(BF16)], axis=1)
    kin = jnp.concatenate([ckv_b, kr], axis=1)
    k_ref[...] = jnp.dot(kin, wk_ref[...], preferred_element_type=F32).astype(BF16)
    lane = lax.broadcasted_iota(jnp.int32, (1, MLA_HEADS * LANES), 1)
    ones = jnp.where((lane & (LANES - 1)) >= V_DIM, 1.0, 0.0)
    v_ref[...] = (jnp.dot(ckv_b, wv_ref[...], preferred_element_type=F32) + ones).astype(BF16)


def _cachekv_call(ckv, kr, pw, tm):
    n_tok = ckv.shape[0]
    hw = MLA_HEADS * LANES
    row = lambda w: pl.BlockSpec((tm, w), lambda t: (t, 0))
    return pl.pallas_call(
        _cachekv_kernel,
        grid=(n_tok // tm,),
        in_specs=[row(KV_LORA), row(QK_ROPE), _const_spec(pw["w_k"].shape), _const_spec(pw["w_v"].shape)],
        out_specs=[row(hw), row(hw)],
        out_shape=[jax.ShapeDtypeStruct((n_tok, hw), BF16)] * 2,
        compiler_params=_params(("arbitrary",)),
        name="cache_kv",
    )(ckv, kr, pw["w_k"], pw["w_v"])


def _hg_consts():
    c_len, sub = HG_CHUNK, SUBLANES
    row8 = lax.broadcasted_iota(jnp.int32, (1, sub, LANES), 1)
    upper8 = [(row8 & (1 << j)) != 0 for j in range(3)]
    ti = lax.broadcasted_iota(jnp.int32, (c_len, c_len), 0)
    si = lax.broadcasted_iota(jnp.int32, (c_len, c_len), 1)
    txs = ti ^ si
    lvl = jnp.full((c_len, c_len), -1, jnp.int32)
    for j in range(HG_LEVELS):
        lvl = lvl + jnp.where(txs >= (1 << j), 1, 0)
    return upper8, lvl


def _hg_lower_bounds(lbp, layer):
    e = jnp.exp(lbp - jnp.max(lbp, axis=0, keepdims=True))
    return jnp.sum(e[:layer + 1], axis=0) / jnp.sum(e, axis=0)


def _hg_gates(z, lbv):
    f = lbv + (1.0 - lbv) * jax.nn.sigmoid(z)
    return jnp.log2(f), 1.0 - f


def _hg_rep(slab, n_rows):
    return slab if n_rows == SUBLANES else jnp.concatenate([slab] * (n_rows // SUBLANES), axis=0)


def _hg_halves(j):
    h = 1 << j
    return [(slice(b, b + h), slice(b + h, b + 2 * h)) for b in range(0, HG_CHUNK, 2 * h)]


def _hg_sel8(upper8, j, a, b):
    shape3 = (HG_CHUNK // SUBLANES, SUBLANES, LANES)
    return jnp.where(upper8[j], a.reshape(shape3), b.reshape(shape3)).reshape(HG_CHUNK, LANES)


def _hg_upsweep(x, inclusive, upper8):
    c_len, sub = HG_CHUNK, SUBLANES
    x3 = x.reshape(c_len // sub, sub, LANES)
    pre3 = [x3 if inclusive else jnp.zeros_like(x3)]
    tot3 = [x3]
    for j in range(3):
        h = 1 << j
        t = tot3[-1]
        if 2 * h == sub:
            sib = pltpu.roll(t, h, 1)
        else:
            sib = jnp.where(upper8[j], pltpu.roll(t, h, 1), pltpu.roll(t, sub - h, 1))
        tot3.append(t + sib)
        pre3.append(pre3[-1] + jnp.where(upper8[j], sib, 0.0))
    pre = [p.reshape(c_len, LANES) for p in pre3]
    tot = [t.reshape(c_len, LANES) for t in tot3]
    slabs = {3: [tot[3][r:r + sub] for r in range(0, c_len, sub)]}
    for j in range(3, HG_LEVELS):
        h = 1 << j
        p, s = pre[-1], slabs[j]
        parts = []
        for i, (lo, up) in enumerate(_hg_halves(j)):
            parts += [p[lo], p[up] + _hg_rep(s[2 * i], h)]
        pre.append(jnp.concatenate(parts, axis=0))
        slabs[j + 1] = [s[2 * i] + s[2 * i + 1] for i in range(len(s) // 2)]
    return pre, tot, slabs


def _hg_chunk_terms(q, zf, zb, v, lb_f, lb_b, upper8, lvl):
    c_len, top = HG_CHUNK, HG_LEVELS
    vb = v.astype(BF16)
    lgf, kf = _hg_gates(zf, lb_f)
    lgb, kb = _hg_gates(zb, lb_b)
    pf, tf, sf = _hg_upsweep(lgf, True, upper8)
    xb, tb, sb = _hg_upsweep(lgb, False, upper8)
    a = lax.dot_general(q.astype(BF16), (kf + kb).astype(BF16), NT, preferred_element_type=F32)
    a = jnp.where(lvl == -1, a, 0.0)
    for j in range(HG_LEVELS):
        h = 1 << j
        if j < 3:
            eq = _hg_sel8(upper8, j, pf[j], tb[j] - xb[j])
            ek = _hg_sel8(upper8, j, xb[j], tf[j] - pf[j])
            ksel = _hg_sel8(upper8, j, kb, kf)
        else:
            eqs, eks, kss = [], [], []
            for i, (lo, up) in enumerate(_hg_halves(j)):
                eqs += [_hg_rep(sb[j][2 * i], h) - xb[j][lo], pf[j][up]]
                eks += [_hg_rep(sf[j][2 * i], h) - pf[j][lo], xb[j][up]]
                kss += [kf[lo], kb[up]]
            eq, ek, ksel = (jnp.concatenate(p, axis=0) for p in (eqs, eks, kss))
        qs = (q * jnp.exp2(eq)).astype(BF16)
        ks = (ksel * jnp.exp2(ek)).astype(BF16)
        aj = lax.dot_general(qs, ks, NT, preferred_element_type=F32)
        a = jnp.where(lvl == j, aj, a)
    o_intra = jnp.dot(a.astype(BF16), vb, preferred_element_type=F32)
    tot_f, tot_b = sf[top][0], sb[top][0]
    qhf = (q * jnp.exp2(pf[top])).astype(BF16)
    qhb = (q * jnp.exp2(_hg_rep(tot_b, c_len) - xb[top])).astype(BF16)
    khf = (kf * jnp.exp2(_hg_rep(tot_f, c_len) - pf[top])).astype(BF16)
    khb = (kb * jnp.exp2(xb[top])).astype(BF16)
    utf = lax.dot_general(vb, khf, TN, preferred_element_type=F32)
    utb = lax.dot_general(vb, khb, TN, preferred_element_type=F32)
    return o_intra, qhf, qhb, utf, utb, jnp.exp2(tot_f[0:1, :]), jnp.exp2(tot_b[0:1, :])


def _attn_hg_kernel(*refs, nseg, tk, layer):
    it = iter(refs)
    q_ref = next(it)
    kv = [next(it) for _ in range(2 * nseg)]
    hq_ref, zf_ref, zb_ref, hv_ref, lbp_ref = (next(it) for _ in range(5))
    o_ref, oi_ref, qhf_ref, qhb_ref, utf_ref, utb_ref, df_ref, db_ref = (next(it) for _ in range(8))
    s_ref = next(it)
    tq = q_ref.shape[1]
    heads = q_ref.shape[2] // LANES
    hg_heads = hq_ref.shape[2] // LANES
    lane = lax.broadcasted_iota(jnp.int32, (1, LANES), 1)
    chunks = [(kv[2 * seg], kv[2 * seg + 1], c * tk)
              for seg in range(nseg) for c in range(kv[2 * seg].shape[1] // tk)]

    upper8, lvl = _hg_consts()
    lb = _hg_lower_bounds(lbp_ref[...], layer)
    local = [(ci, h) for ci in range(tq // HG_CHUNK) for h in range(hg_heads)]
    per_head = len(local) // heads

    def hg_local(ci, h):
        rs = slice(ci * HG_CHUNK, (ci + 1) * HG_CHUNK)
        cs = slice(h * LANES, (h + 1) * LANES)
        oi, qhf, qhb, utf, utb, dec_f, dec_b = _hg_chunk_terms(
            hq_ref[0, rs, cs], zf_ref[0, rs, cs], zb_ref[0, rs, cs], hv_ref[0, rs, cs],
            lb[0:1, cs], lb[1:2, cs], upper8, lvl)
        oi_ref[0, rs, cs] = oi
        qhf_ref[0, rs, cs] = qhf
        qhb_ref[0, rs, cs] = qhb
        utf_ref[0, ci, h] = utf
        utb_ref[0, ci, h] = utb
        df_ref[0, ci, :, cs] = dec_f
        db_ref[0, ci, :, cs] = dec_b

    outs = []
    for h in range(heads):
        hs = slice(h * LANES, (h + 1) * LANES)
        qh = q_ref[0, :, hs]
        mx = jnp.full((tq, LANES), -jnp.inf, F32)
        for i, (k_ref, _, off) in enumerate(chunks):
            s = lax.dot_general(qh, k_ref[0, off:off + tk, hs], NT, preferred_element_type=F32)
            s_ref[:, i * tk:(i + 1) * tk] = s
            for w in range(tk // LANES):
                mx = jnp.maximum(mx, s[:, w * LANES:(w + 1) * LANES])
        m = jnp.broadcast_to(jnp.max(mx, axis=1, keepdims=True), (tq, LANES))
        mb = jnp.concatenate([m] * (tk // LANES), axis=1)
        acc = jnp.zeros((tq, LANES), F32)
        for i, (_, v_ref, off) in enumerate(chunks):
            p = jnp.exp2(s_ref[:, i * tk:(i + 1) * tk] - mb).astype(BF16)
            acc = acc + jnp.dot(p, v_ref[0, off:off + tk, hs], preferred_element_type=F32)
        outs.append(acc / pltpu.roll(acc, V_DIM, 1))
        for ci, hh in local[h * per_head:(h + 1) * per_head]:
            hg_local(ci, hh)
    for pair in range(heads // 2):
        a = outs[2 * pair]
        b = pltpu.roll(outs[2 * pair + 1], V_DIM, 1)
        o_ref[0, :, pair * LANES:(pair + 1) * LANES] = jnp.where(lane < V_DIM, a, b).astype(BF16)


def _attn_hg_call(q, kvs, hgin, lb_param, *, tq, tk, heads, layer, name):
    b, t, _ = q.shape
    n_chunks = t // HG_CHUNK
    cpt = tq // HG_CHUNK
    groups = MLA_HEADS // heads
    hgh = HG_HEADS // groups
    assert heads % 2 == 0 and hgh >= 1 and (cpt * hgh) % heads == 0
    aw, gw = heads * LANES, hgh * LANES
    in_specs = [pl.BlockSpec((1, tq, aw), lambda i, p, j: (i, j, p))]
    args = [q]
    n_keys = 0
    for k, v in kvs:
        n_keys += k.shape[1]
        for a in (k, v):
            in_specs.append(pl.BlockSpec((1, a.shape[1], aw), lambda i, p, j: (i, 0, p)))
            args.append(a)
    for g in range(4):
        in_specs.append(pl.BlockSpec((1, tq, gw), lambda i, p, j, g=g: (i, j, g * groups + p)))
        args.append(hgin)
    in_specs.append(pl.BlockSpec((lb_param.shape[0], 2, gw), lambda i, p, j: (0, 0, p)))
    args.append(lb_param)
    rows = lambda w: pl.BlockSpec((1, tq, w), lambda i, p, j: (i, j, p))
    ut_spec = pl.BlockSpec((1, cpt, hgh, HG_DV, HG_DK), lambda i, p, j: (i, j, p, 0, 0))
    dec_spec = pl.BlockSpec((1, cpt, 1, gw), lambda i, p, j: (i, j, 0, p))
    ut_shape = jax.ShapeDtypeStruct((b, n_chunks, HG_HEADS, HG_DV, HG_DK), F32)
    dec_shape = jax.ShapeDtypeStruct((b, n_chunks, 1, HG_WIDTH), F32)
    return pl.pallas_call(
        functools.partial(_attn_hg_kernel, nseg=len(kvs), tk=tk, layer=layer),
        grid=(b, groups, t // tq),
        in_specs=in_specs,
        out_specs=[rows(heads * V_DIM), rows(gw), rows(gw), rows(gw),
                   ut_spec, ut_spec, dec_spec, dec_spec],
        out_shape=[jax.ShapeDtypeStruct((b, t, MLA_HEADS * V_DIM), BF16),
                   jax.ShapeDtypeStruct((b, t, HG_WIDTH), F32),
                   jax.ShapeDtypeStruct((b, t, HG_WIDTH), BF16),
                   jax.ShapeDtypeStruct((b, t, HG_WIDTH), BF16),
                   ut_shape, ut_shape, dec_shape, dec_shape],
        scratch_shapes=[pltpu.VMEM((tq, n_keys), F32)],
        compiler_params=_params(("arbitrary", "arbitrary", "arbitrary")),
        name=name,
    )(*args)


def _hg_scan_kernel(*refs, has_s0):
    it = iter(refs)
    oi_ref, qhf_ref, qhb_ref, utf_ref, utb_ref, df_ref, db_ref, g_ref, ghg_ref = (next(it) for _ in range(9))
    if has_s0:
        s0f_ref, s0b_ref = next(it), next(it)
    o_ref, sf_ref, sb_ref, sbst_ref = (next(it) for _ in range(4))
    c_len = HG_CHUNK
    n_chunks = utf_ref.shape[1]
    unroll = min(HG_UNROLL, n_chunks)

    def rows_of(c):
        return pl.ds(pl.multiple_of(c * c_len, c_len), c_len)

    def bwd_body(i, st):
        c = n_chunks - 1 - i
        sbst_ref[c] = st.astype(BF16)
        return db_ref[0, c] * st + utb_ref[0, c, 0]

    st_b0 = s0b_ref[0, 0].T if has_s0 else jnp.zeros((HG_DV, HG_DK), F32)
    st_b = lax.fori_loop(0, n_chunks, bwd_body, st_b0, unroll=unroll)
    sb_ref[0, 0] = st_b.T

    ghg = ghg_ref[...]

    def fwd_body(c, st):
        qcat = jnp.concatenate([qhf_ref[0, rows_of(c), :], qhb_ref[0, rows_of(c), :]], axis=1)
        scat = jnp.concatenate([st.astype(BF16), sbst_ref[c]], axis=1)
        o = oi_ref[0, rows_of(c), :] + lax.dot_general(qcat, scat, NT, preferred_element_type=F32)
        on = _rms(o, ghg) * g_ref[0, rows_of(c), :]
        o_ref[0, rows_of(c), :] = on.astype(BF16)
        return df_ref[0, c] * st + utf_ref[0, c, 0]

    st_f0 = s0f_ref[0, 0].T if has_s0 else jnp.zeros((HG_DV, HG_DK), F32)
    st_f = lax.fori_loop(0, n_chunks, fwd_body, st_f0, unroll=unroll)
    sf_ref[0, 0] = st_f.T


def _hg_scan_call(oi, qhf, qhb, utf, utb, dec_f, dec_b, hgin, g_hg, s0, *, name):
    b, t, _ = oi.shape
    n_chunks = t // HG_CHUNK
    has_s0 = s0 is not None
    col = pl.BlockSpec((1, t, LANES), lambda i, h: (i, 0, h))
    ut_spec = pl.BlockSpec((1, n_chunks, 1, HG_DV, HG_DK), lambda i, h: (i, 0, h, 0, 0))
    dec_spec = pl.BlockSpec((1, n_chunks, 1, LANES), lambda i, h: (i, 0, 0, h))
    gate_col = pl.BlockSpec((1, t, LANES), lambda i, h: (i, 0, 4 * HG_HEADS + h))
    in_specs = [col, col, col, ut_spec, ut_spec, dec_spec, dec_spec, gate_col,
                pl.BlockSpec((1, LANES), lambda i, h: (0, h))]
    args = [oi, qhf, qhb, utf, utb, dec_f, dec_b, hgin, g_hg]
    st_spec = pl.BlockSpec((1, 1, HG_DK, HG_DV), lambda i, h: (i, h, 0, 0))
    if has_s0:
        in_specs += [st_spec, st_spec]
        args += list(s0)
    st_shape = jax.ShapeDtypeStruct((b, HG_HEADS, HG_DK, HG_DV), F32)
    return pl.pallas_call(
        functools.partial(_hg_scan_kernel, has_s0=has_s0),
        grid=(b, HG_HEADS),
        in_specs=in_specs,
        out_specs=[col, st_spec, st_spec],
        out_shape=[jax.ShapeDtypeStruct((b, t, HG_WIDTH), BF16), st_shape, st_shape],
        scratch_shapes=[pltpu.VMEM((n_chunks, HG_DV, HG_DK), BF16)],
        compiler_params=_params(("arbitrary", "arbitrary")),
        name=name,
    )(*args)


def _mix_kernel(om_ref, oh_ref, gate_ref, x_ref, mod_ref, wmo_ref, who_ref, wout_ref, gffn_ref,
                x1_ref, h2_ref):
    a = jnp.dot(om_ref[...], wmo_ref[...], preferred_element_type=F32)
    b = jnp.dot(oh_ref[...], who_ref[...], preferred_element_type=F32)
    mix = gate_ref[:, 0:D_MODEL] * a + gate_ref[:, D_MODEL:2 * D_MODEL] * b
    y = jnp.dot(mix.astype(BF16), wout_ref[...], preferred_element_type=F32)
    gt1 = mod_ref[0, 2:3, :]
    sh2 = mod_ref[0, 3:4, :]
    sc2 = mod_ref[0, 4:5, :]
    x1 = x_ref[...] + gt1 * y
    x1_ref[...] = x1
    h2_ref[...] = (_rms(x1, gffn_ref[...]) * (1.0 + sc2) + sh2).astype(BF16)


def _mix_call(om, oh, gate, x, mod, pw, *, seq_len, mod_row0, per_seq_mod, tm, name):
    n_tok = x.shape[0]
    tps = max(seq_len // tm, 1)
    if per_seq_mod:
        mod_map = lambda t: (mod_row0 + t // tps, 0, 0)
    else:
        mod_map = lambda t: (mod_row0, 0, 0)
    row = lambda w: pl.BlockSpec((tm, w), lambda t: (t, 0))
    ws = [pw[k] for k in ("w_mla_o", "w_hg_o", "w_out", "g_ffn")]
    return pl.pallas_call(
        _mix_kernel,
        grid=(n_tok // tm,),
        in_specs=[row(MLA_HEADS * V_DIM), row(HG_WIDTH), row(2 * D_MODEL), row(D_MODEL),
                  pl.BlockSpec((1, 6, D_MODEL), mod_map)] + [_const_spec(w.shape) for w in ws],
        out_specs=[row(D_MODEL), row(D_MODEL)],
        out_shape=[jax.ShapeDtypeStruct((n_tok, D_MODEL), F32),
                   jax.ShapeDtypeStruct((n_tok, D_MODEL), BF16)],
        compiler_params=_params(("arbitrary",)),
        name=name,
    )(om, oh, gate, x, mod, *ws)


def _gelu_tanh(x):
    return 0.5 * x * (1.0 + jnp.tanh(0.7978845608028654 * (x + 0.044715 * (x * x * x))))


def _ffn_kernel(h2_ref, hp_ref, hn_ref, x1_ref, mod_ref, wg_ref, wv_ref, cw_ref, cb_ref, wo_ref,
                gfin_ref, y_ref, ext_ref, gsc_ref, u_ref, *, tm, seq_len):
    halo = BF16_SUBLANES
    t = pl.program_id(0)
    ext_ref[0:halo, :] = hp_ref[...]
    ext_ref[halo:halo + tm, :] = h2_ref[...]
    ext_ref[halo + tm:2 * halo + tm, :] = hn_ref[...]
    pos = (lax.broadcasted_iota(jnp.int32, (tm, 1), 0) + t * tm) & (seq_len - 1)
    has_prev = pos != 0
    has_next = pos != seq_len - 1
    for j in range(D_FF // FF_CHUNK):
        cs = slice(j * FF_CHUNK, (j + 1) * FF_CHUNK)
        gsc_ref[...] = jnp.dot(ext_ref[...], wg_ref[:, cs], preferred_element_type=F32)
        g_prev = jnp.where(has_prev, gsc_ref[halo - 1:halo - 1 + tm, :], 0.0)
        g_cur = gsc_ref[halo:halo + tm, :]
        g_next = jnp.where(has_next, gsc_ref[halo + 1:halo + 1 + tm, :], 0.0)
        gconv = cb_ref[:, cs] + g_prev * cw_ref[0:1, cs] + g_cur * cw_ref[1:2, cs] + g_next * cw_ref[2:3, cs]
        val = jnp.dot(h2_ref[...], wv_ref[:, cs], preferred_element_type=F32)
        u_ref[:, cs] = (_gelu_tanh(gconv) * val).astype(BF16)
    f = jnp.dot(u_ref[...], wo_ref[...], preferred_element_type=F32)
    gt2 = mod_ref[0, 5:6, :]
    y_ref[...] = _rms(x1_ref[...] + gt2 * f, gfin_ref[...])


def _ffn_call(h2, x1, mod, pw, *, seq_len, mod_row0, per_seq_mod, tm, name):
    n_tok = x1.shape[0]
    tps = max(seq_len // tm, 1)
    halo = BF16_SUBLANES
    n_hblk = n_tok // halo
    hpt = tm // halo
    if per_seq_mod:
        mod_map = lambda t: (mod_row0 + t // tps, 0, 0)
    else:
        mod_map = lambda t: (mod_row0, 0, 0)
    row = lambda w: pl.BlockSpec((tm, w), lambda t: (t, 0))
    ws = [pw[k] for k in ("w_ffn_g", "w_ffn_v", "conv_w", "conv_b", "w_ffn_out", "g_final")]
    return pl.pallas_call(
        functools.partial(_ffn_kernel, tm=tm, seq_len=seq_len),
        grid=(n_tok // tm,),
        in_specs=[row(D_MODEL),
                  pl.BlockSpec((halo, D_MODEL), lambda t: (jnp.maximum(t * hpt - 1, 0), 0)),
                  pl.BlockSpec((halo, D_MODEL), lambda t: (jnp.minimum((t + 1) * hpt, n_hblk - 1), 0)),
                  row(D_MODEL),
                  pl.BlockSpec((1, 6, D_MODEL), mod_map)] + [_const_spec(w.shape) for w in ws],
        out_specs=row(D_MODEL),
        out_shape=jax.ShapeDtypeStruct((n_tok, D_MODEL), F32),
        scratch_shapes=[pltpu.VMEM((tm + 2 * halo, D_MODEL), BF16),
                        pltpu.VMEM((tm + 2 * halo, FF_CHUNK), F32),
                        pltpu.VMEM((tm, D_FF), BF16)],
        compiler_params=_params(("arbitrary",)),
        name=name,
    )(h2, h2, h2, x1, mod, *ws)


def _rope_tables(seq_len):
    rows = seq_len // GRID_W
    rr, cc = jnp.meshgrid(jnp.arange(rows, dtype=F32), jnp.arange(GRID_W, dtype=F32), indexing="ij")
    n_freq = QK_ROPE // 4
    inv = jnp.power(ROPE_BASE, -jnp.arange(n_freq, dtype=F32) / n_freq)
    ang = jnp.concatenate([rr.reshape(-1)[:, None] * inv, cc.reshape(-1)[:, None] * inv], axis=-1)
    cos, sin = jnp.cos(ang), jnp.sin(ang)
    z16 = jnp.zeros_like(sin)
    ones = jnp.ones((seq_len, QK_NOPE), F32)
    zeros = jnp.zeros((seq_len, QK_NOPE), F32)
    s = ATTN_SCALE
    cs = jnp.concatenate([ones * s, cos * s, cos * s, cos, cos], axis=1)
    sa = jnp.concatenate([zeros, -sin * s, z16, -sin, z16], axis=1)
    sb = jnp.concatenate([zeros, z16, sin * s, z16, sin], axis=1)
    return jnp.stack([cs, sa, sb], axis=0)


def _prep_weights(l, g_mix, w_in, g_q, w_uq, g_kv, w_ukv, w_mla_o, g_hg, w_hg_o, w_out, g_ffn,
                  w_ffn_in, conv_w, conv_b, w_ffn_out, g_final):
    d = D_MODEL
    wi = w_in[l]
    c_kr = Q_LORA + KV_LORA
    w_in_p = jnp.concatenate([wi[:, :c_kr], jnp.zeros((d, KR_LANE0), F32),
                              wi[:, c_kr:c_kr + QK_ROPE], wi[:, c_kr + QK_ROPE:]], axis=1)
    hd = QK_NOPE + QK_ROPE
    w_uq_p = jnp.concatenate([w_uq[l].reshape(Q_LORA, MLA_HEADS, hd),
                              jnp.zeros((Q_LORA, MLA_HEADS, LANES - hd), F32)], axis=-1)
    wkv = w_ukv[l].reshape(KV_LORA, MLA_HEADS, QK_NOPE + V_DIM)
    zpad = jnp.zeros((KV_LORA, MLA_HEADS, LANES - QK_NOPE), F32)
    w_k_top = jnp.concatenate([wkv[..., :QK_NOPE], zpad], axis=-1).reshape(KV_LORA, MLA_HEADS * LANES)
    place = jnp.zeros((LANES, LANES), F32).at[
        KR_LANE0 + jnp.arange(QK_ROPE), QK_NOPE + jnp.arange(QK_ROPE)].set(1.0)
    w_k_bot = jnp.tile(place, (1, MLA_HEADS))
    w_k = jnp.concatenate([w_k_top, w_k_bot], axis=0)
    w_v = jnp.concatenate([wkv[..., QK_NOPE:], jnp.zeros((KV_LORA, MLA_HEADS, LANES - V_DIM), F32)],
                          axis=-1).reshape(KV_LORA, MLA_HEADS * LANES)
    return {
        "g_mix": g_mix[l].reshape(1, d), "w_in": w_in_p.astype(BF16),
        "g_q": g_q[l].reshape(1, Q_LORA), "w_uq": w_uq_p.reshape(Q_LORA, MLA_HEADS * LANES).astype(BF16),
        "g_kv": g_kv[l].reshape(1, KV_LORA), "w_k": w_k.astype(BF16), "w_v": w_v.astype(BF16),
        "w_mla_o": w_mla_o[l].astype(BF16), "w_hg_o": w_hg_o[l].astype(BF16),
        "w_out": w_out[l].astype(BF16), "g_ffn": g_ffn[l].reshape(1, d),
        "w_ffn_g": w_ffn_in[l][:, :D_FF].astype(BF16), "w_ffn_v": w_ffn_in[l][:, D_FF:].astype(BF16),
        "conv_w": conv_w[l], "conv_b": conv_b[l].reshape(1, D_FF),
        "w_ffn_out": w_ffn_out[l].astype(BF16), "g_final": g_final.reshape(1, d),
        "g_hg": g_hg[l].reshape(1, HG_WIDTH),
    }


def _trunk(x, mod, tabs, cache, pw, lb_param, *, layer, batch, seq_len, mod_row0, per_seq_mod,
           emit_cache, tm, tq, tag):
    n_tok = batch * seq_len
    hw = MLA_HEADS * LANES
    outs = _proj_call(x, mod, tabs, pw, seq_len=seq_len, mod_row0=mod_row0, per_seq_mod=per_seq_mod,
                      emit_cache=emit_cache, tm=tm)
    q, k, v, hgin, gate = outs[:5]
    hgin = hgin.reshape(batch, seq_len, 5 * HG_WIDTH)
    kvs = [(k.reshape(batch, seq_len, hw), v.reshape(batch, seq_len, hw))]
    s0 = None
    if cache is not None:
        ckv_c, kr_c, s0f, s0b = cache
        past = ckv_c.shape[1]
        kc, vc = _cachekv_call(ckv_c.reshape(batch * past, KV_LORA), kr_c.reshape(batch * past, QK_ROPE),
                               pw, tm=min(512, batch * past))
        kvs.append((kc.reshape(batch, past, hw), vc.reshape(batch, past, hw)))
        s0 = (s0f, s0b)
    o_mla, *hg_terms = _attn_hg_call(q.reshape(batch, seq_len, hw), kvs, hgin, lb_param, tq=tq,
                                     tk=min(512, seq_len), heads=MLA_HEADS if seq_len <= tq else ATTN_HEADS_PER_STEP,
                                     layer=layer, name="attn_hg_" + tag)
    o_hg, s_f, s_b = _hg_scan_call(*hg_terms, hgin, pw["g_hg"], s0, name="hg_scan_" + tag)
    common = dict(seq_len=seq_len, mod_row0=mod_row0, per_seq_mod=per_seq_mod, tm=tm)
    x1, h2 = _mix_call(o_mla.reshape(n_tok, MLA_HEADS * V_DIM), o_hg.reshape(n_tok, HG_WIDTH), gate, x,
                       mod, pw, name="mix_" + tag, **common)
    y = _ffn_call(h2, x1, mod, pw, name="ffn_" + tag, **common)
    extra = outs[5:] if emit_cache else None
    return y, extra, s_f, s_b


def kernel(x_prompt, x_sample, cache_ckv, cache_krope, state_hgrn_fwd, state_hgrn_bwd, c, c_ctx, w_mod,
           b_mod, g_mix, w_in, g_q, w_uq, g_kv, w_ukv, w_mla_o, lb_param, g_hg, w_hg_o, w_out, g_ffn,
           w_ffn_in, conv_w, conv_b, w_ffn_out, g_final):
    batch, seq, d = x_prompt.shape
    dec_batch, dec_seq, _ = x_sample.shape
    depth = w_in.shape[0]
    assert depth == 1, "single trunk layer"
    l = 0
    cvec = jnp.concatenate([c_ctx[None, :], c, jnp.zeros((8 - 1 - dec_batch, d), F32)], axis=0)
    mod = _mod_call(cvec, w_mod[l], b_mod[l]).reshape(8, 6, d)
    pw = _prep_weights(l, g_mix, w_in, g_q, w_uq, g_kv, w_ukv, w_mla_o, g_hg, w_hg_o, w_out, g_ffn,
                       w_ffn_in, conv_w, conv_b, w_ffn_out, g_final)
    tabs = _rope_tables(dec_seq)

    y_ctx, cache_out, s_f, s_b = _trunk(
        x_prompt.reshape(batch * seq, d), mod, None, None, pw, lb_param, layer=l, batch=batch,
        seq_len=seq, mod_row0=0, per_seq_mod=False, emit_cache=True, tm=TOKEN_TILE, tq=min(Q_TILE, seq),
        tag="ctx")
    cache_l = (cache_ckv[:, l], cache_krope[:, l], state_hgrn_fwd[:, l], state_hgrn_bwd[:, l])
    y_lat, _, _, _ = _trunk(
        x_sample.reshape(dec_batch * dec_seq, d), mod, tabs, cache_l, pw, lb_param, layer=l,
        batch=dec_batch, seq_len=dec_seq, mod_row0=1, per_seq_mod=True, emit_cache=False, tm=TOKEN_TILE,
        tq=min(Q_TILE, dec_seq), tag="lat")

    ckv_new, kr_new = cache_out
    return (y_ctx.reshape(batch, seq, d), y_lat.reshape(dec_batch, dec_seq, d),
            ckv_new.reshape(batch, 1, seq, KV_LORA), kr_new.reshape(batch, 1, seq, QK_ROPE),
            s_f.reshape(batch, 1, HG_HEADS, HG_DK, HG_DV), s_b.reshape(batch, 1, HG_HEADS, HG_DK, HG_DV))
```

```python
import functools

import jax
import jax.numpy as jnp
from jax import lax
from jax.experimental import pallas as pl
from jax.experimental.pallas import tpu as pltpu

F32 = jnp.float32
BF16 = jnp.bfloat16

D_MODEL = 1024
GRID_W = 64
MLA_HEADS = 8
QK_NOPE = 64
QK_ROPE = 32
V_DIM = 64
Q_LORA = 384
KV_LORA = 256
ROPE_BASE = 10000.0
HG_HEADS = 4
HG_DK = 128
HG_DV = 128
HG_WIDTH = HG_HEADS * HG_DV
D_FF = 2816
EPS = 1e-6

LANES = 128
SUBLANES = 8
BF16_SUBLANES = 16
HG_CHUNK = 128
HG_LEVELS = 7
HG_UNROLL = 4
FF_CHUNK = 256
TOKEN_TILE = 512
Q_TILE = 512
ATTN_HEADS_PER_STEP = 4
VMEM_LIMIT = 56 * 1024 * 1024

C_PQ = 0
C_PKV = Q_LORA
C_PKR = C_PKV + KV_LORA
C_HQ = C_PKR + LANES
C_ZF = C_HQ + HG_WIDTH
C_HG = C_ZF + 3 * HG_WIDTH
C_GA = C_HG + HG_WIDTH
C_END = C_GA + 2 * D_MODEL
KR_LANE0 = LANES - QK_ROPE

ATTN_SCALE = (QK_NOPE + QK_ROPE) ** -0.5 * 1.4426950408889634

NT = (((1,), (1,)), ((), ()))
TN = (((0,), (0,)), ((), ()))


def _rms(x, g):
    return x * lax.rsqrt(jnp.mean(x * x, axis=-1, keepdims=True) + EPS) * g


def _const_spec(shape):
    n = len(shape)
    return pl.BlockSpec(shape, lambda *_: (0,) * n, pipeline_mode=pl.Buffered(1))


def _params(sem):
    return pltpu.CompilerParams(dimension_semantics=sem, vmem_limit_bytes=VMEM_LIMIT)


def _mod_kernel(c_ref, w_ref, b_ref, o_ref):
    c = c_ref[...]
    a = c * jax.nn.sigmoid(c)
    o_ref[...] = jnp.dot(a, w_ref[...], precision=lax.Precision.HIGHEST,
                         preferred_element_type=F32) + b_ref[...]


def _mod_call(cvec, w_mod, b_mod):
    n = w_mod.shape[1]
    tn = 1024
    return pl.pallas_call(
        _mod_kernel,
        grid=(n // tn,),
        in_specs=[pl.BlockSpec((8, D_MODEL), lambda j: (0, 0)),
                  pl.BlockSpec((D_MODEL, tn), lambda j: (0, j)),
                  pl.BlockSpec((1, tn), lambda j: (0, j))],
        out_specs=pl.BlockSpec((8, tn), lambda j: (0, j)),
        out_shape=jax.ShapeDtypeStruct((8, n), F32),
        compiler_params=_params(("arbitrary",)),
        name="mod",
    )(cvec, w_mod, b_mod.reshape(1, n))


def _proj_kernel(*refs, rope, emit_cache):
    it = iter(refs)
    x_ref = next(it)
    mod_ref = next(it)
    tab_ref = next(it) if rope else None
    gmix_ref, win_ref, gq_ref, wuq_ref, gkv_ref, wk_ref, wv_ref = (next(it) for _ in range(7))
    q_ref, k_ref, v_ref, hg_ref, gate_ref = (next(it) for _ in range(5))
    if emit_cache:
        ckv_ref, kr_ref = next(it), next(it)

    sh1 = mod_ref[0, 0:1, :]
    sc1 = mod_ref[0, 1:2, :]
    hb = (_rms(x_ref[...], gmix_ref[...]) * (1.0 + sc1) + sh1).astype(BF16)

    def proj(c0, c1):
        return jnp.dot(hb, win_ref[:, c0:c1], preferred_element_type=F32)

    if rope:
        cs1, sa1, sb1 = tab_ref[0], tab_ref[1], tab_ref[2]

        def rot(x, reps):
            w = x.shape[1]
            cs = jnp.concatenate([cs1] * reps, axis=1) if reps > 1 else cs1
            sa = jnp.concatenate([sa1] * reps, axis=1) if reps > 1 else sa1
            sb = jnp.concatenate([sb1] * reps, axis=1) if reps > 1 else sb1
            half = QK_ROPE // 2
            return x * cs + pltpu.roll(x, w - half, 1) * sa + pltpu.roll(x, half, 1) * sb

    qn = _rms(proj(C_PQ, C_PKV), gq_ref[...]).astype(BF16)
    q = jnp.dot(qn, wuq_ref[...], preferred_element_type=F32)
    q = rot(q, MLA_HEADS) if rope else q * ATTN_SCALE
    q_ref[...] = q.astype(BF16)

    ckv = _rms(proj(C_PKV, C_PKR), gkv_ref[...])
    pkr = proj(C_PKR, C_HQ)
    if emit_cache:
        ckv_ref[...] = ckv
        kr_ref[...] = pkr[:, KR_LANE0:]
    ckv_b = ckv.astype(BF16)
    kr = rot(pkr, 1) if rope else pkr
    kin = jnp.concatenate([ckv_b, kr.astype(BF16)], axis=1)
    k_ref[...] = jnp.dot(kin, wk_ref[...], preferred_element_type=F32).astype(BF16)
    lane = lax.broadcasted_iota(jnp.int32, (1, MLA_HEADS * LANES), 1)
    ones = jnp.where((lane & (LANES - 1)) >= V_DIM, 1.0, 0.0)
    v_ref[...] = (jnp.dot(ckv_b, wv_ref[...], preferred_element_type=F32) + ones).astype(BF16)

    hq = proj(C_HQ, C_ZF)
    hg_ref[:, 0:HG_WIDTH] = hq * jax.nn.sigmoid(hq)
    hg_ref[:, HG_WIDTH:4 * HG_WIDTH] = proj(C_ZF, C_HG)
    og = proj(C_HG, C_GA)
    hg_ref[:, 4 * HG_WIDTH:5 * HG_WIDTH] = og * jax.nn.sigmoid(og)

    gate_ref[...] = jax.nn.sigmoid(proj(C_GA, C_END)).astype(BF16)


def _proj_call(x, mod, tabs, pw, *, seq_len, mod_row0, per_seq_mod, emit_cache, tm):
    n_tok = x.shape[0]
    tps = max(seq_len // tm, 1)
    rope = tabs is not None
    if per_seq_mod:
        mod_map = lambda t: (mod_row0 + t // tps, 0, 0)
    else:
        mod_map = lambda t: (mod_row0, 0, 0)
    row = lambda w: pl.BlockSpec((tm, w), lambda t: (t, 0))
    in_specs = [row(D_MODEL), pl.BlockSpec((1, 6, D_MODEL), mod_map)]
    args = [x, mod]
    if rope:
        in_specs.append(pl.BlockSpec((3, tm, LANES), lambda t: (0, t % tps, 0)))
        args.append(tabs)
    for name in ("g_mix", "w_in", "g_q", "w_uq", "g_kv", "w_k", "w_v"):
        in_specs.append(_const_spec(pw[name].shape))
        args.append(pw[name])
    hw = MLA_HEADS * LANES
    out_shape = [jax.ShapeDtypeStruct((n_tok, hw), BF16),
                 jax.ShapeDtypeStruct((n_tok, hw), BF16),
                 jax.ShapeDtypeStruct((n_tok, hw), BF16),
                 jax.ShapeDtypeStruct((n_tok, 5 * HG_WIDTH), F32),
                 jax.ShapeDtypeStruct((n_tok, 2 * D_MODEL), BF16)]
    out_specs = [row(hw), row(hw), row(hw), row(5 * HG_WIDTH), row(2 * D_MODEL)]
    if emit_cache:
        out_shape += [jax.ShapeDtypeStruct((n_tok, KV_LORA), F32),
                      jax.ShapeDtypeStruct((n_tok, QK_ROPE), F32)]
        out_specs += [row(KV_LORA), row(QK_ROPE)]
    return pl.pallas_call(
        functools.partial(_proj_kernel, rope=rope, emit_cache=emit_cache),
        grid=(n_tok // tm,),
        in_specs=in_specs,
        out_specs=out_specs,
        out_shape=out_shape,
        compiler_params=_params(("arbitrary",)),
        name="proj_lat" if rope else "proj_ctx",
    )(*args)


def _cachekv_kernel(ckv_ref, kr_ref, wk_ref, wv_ref, k_ref, v_ref):
    ckv_b = ckv_ref[...].astype(BF16)
    tm = ckv_b.shape[0]
    kr = jnp.concatenate([jnp.zeros((tm, KR_LANE0), BF16), kr_ref[...].astype(BF16)], axis=1)
    kin = jnp.concatenate([ckv_b, kr], axis=1)
    k_ref[...] = jnp.dot(kin, wk_ref[...], preferred_element_type=F32).astype(BF16)
    lane = lax.broadcasted_iota(jnp.int32, (1, MLA_HEADS * LANES), 1)
    ones = jnp.where((lane & (LANES - 1)) >= V_DIM, 1.0, 0.0)
    v_ref[...] = (jnp.dot(ckv_b, wv_ref[...], preferred_element_type=F32) + ones).astype(BF16)


def _cachekv_call(ckv, kr, pw, tm):
    n_tok = ckv.shape[0]
    hw = MLA_HEADS * LANES
    row = lambda w: pl.BlockSpec((tm, w), lambda t: (t, 0))
    return pl.pallas_call(
        _cachekv_kernel,
        grid=(n_tok // tm,),
        in_specs=[row(KV_LORA), row(QK_ROPE), _const_spec(pw["w_k"].shape), _const_spec(pw["w_v"].shape)],
        out_specs=[row(hw), row(hw)],
        out_shape=[jax.ShapeDtypeStruct((n_tok, hw), BF16)] * 2,
        compiler_params=_params(("arbitrary",)),
        name="cache_kv",
    )(ckv, kr, pw["w_k"], pw["w_v"])


def _hg_consts():
    c_len, sub = HG_CHUNK, SUBLANES
    row8 = lax.broadcasted_iota(jnp.int32, (1, sub, LANES), 1)
    upper8 = [(row8 & (1 << j)) != 0 for j in range(3)]
    ti = lax.broadcasted_iota(jnp.int32, (c_len, c_len), 0)
    si = lax.broadcasted_iota(jnp.int32, (c_len, c_len), 1)
    txs = ti ^ si
    lvl = jnp.full((c_len, c_len), -1, jnp.int32)
    for j in range(HG_LEVELS):
        lvl = lvl + jnp.where(txs >= (1 << j), 1, 0)
    return upper8, lvl


def _hg_lower_bounds(lbp, layer):
    e = jnp.exp(lbp - jnp.max(lbp, axis=0, keepdims=True))
    return jnp.sum(e[:layer + 1], axis=0) / jnp.sum(e, axis=0)


def _hg_gates(z, lbv):
    f = lbv + (1.0 - lbv) * jax.nn.sigmoid(z)
    return jnp.log2(f), 1.0 - f


def _hg_rep(slab, n_rows):
    return slab if n_rows == SUBLANES else jnp.concatenate([slab] * (n_rows // SUBLANES), axis=0)


def _hg_halves(j):
    h = 1 << j
    return [(slice(b, b + h), slice(b + h, b + 2 * h)) for b in range(0, HG_CHUNK, 2 * h)]


def _hg_sel8(upper8, j, a, b):
    shape3 = (HG_CHUNK // SUBLANES, SUBLANES, LANES)
    return jnp.where(upper8[j], a.reshape(shape3), b.reshape(shape3)).reshape(HG_CHUNK, LANES)


def _hg_upsweep(x, inclusive, upper8):
    c_len, sub = HG_CHUNK, SUBLANES
    x3 = x.reshape(c_len // sub, sub, LANES)
    pre3 = [x3 if inclusive else jnp.zeros_like(x3)]
    tot3 = [x3]
    for j in range(3):
        h = 1 << j
        t = tot3[-1]
        if 2 * h == sub:
            sib = pltpu.roll(t, h, 1)
        else:
            sib = jnp.where(upper8[j], pltpu.roll(t, h, 1), pltpu.roll(t, sub - h, 1))
        tot3.append(t + sib)
        pre3.append(pre3[-1] + jnp.where(upper8[j], sib, 0.0))
    pre = [p.reshape(c_len, LANES) for p in pre3]
    tot = [t.reshape(c_len, LANES) for t in tot3]
    slabs = {3: [tot[3][r:r + sub] for r in range(0, c_len, sub)]}
    for j in range(3, HG_LEVELS):
        h = 1 << j
        p, s = pre[-1], slabs[j]
        parts = []
        for i, (lo, up) in enumerate(_hg_halves(j)):
            parts += [p[lo], p[up] + _hg_rep(s[2 * i], h)]
        pre.append(jnp.concatenate(parts, axis=0))
        slabs[j + 1] = [s[2 * i] + s[2 * i + 1] for i in range(len(s) // 2)]
    return pre, tot, slabs


def _hg_chunk_terms(q, zf, zb, v, lb_f, lb_b, upper8, lvl):
    c_len, top = HG_CHUNK, HG_LEVELS
    vb = v.astype(BF16)
    lgf, kf = _hg_gates(zf, lb_f)
    lgb, kb = _hg_gates(zb, lb_b)
    pf, tf, sf = _hg_upsweep(lgf, True, upper8)
    xb, tb, sb = _hg_upsweep(lgb, False, upper8)
    a = lax.dot_general(q.astype(BF16), (kf + kb).astype(BF16), NT, preferred_element_type=F32)
    a = jnp.where(lvl == -1, a, 0.0)
    for j in range(HG_LEVELS):
        h = 1 << j
        if j < 3:
            eq = _hg_sel8(upper8, j, pf[j], tb[j] - xb[j])
            ek = _hg_sel8(upper8, j, xb[j], tf[j] - pf[j])
            ksel = _hg_sel8(upper8, j, kb, kf)
        else:
            eqs, eks, kss = [], [], []
            for i, (lo, up) in enumerate(_hg_halves(j)):
                eqs += [_hg_rep(sb[j][2 * i], h) - xb[j][lo], pf[j][up]]
                eks += [_hg_rep(sf[j][2 * i], h) - pf[j][lo], xb[j][up]]
                kss += [kf[lo], kb[up]]
            eq, ek, ksel = (jnp.concatenate(p, axis=0) for p in (eqs, eks, kss))
        qs = (q * jnp.exp2(eq)).astype(BF16)
        ks = (ksel * jnp.exp2(ek)).astype(BF16)
        aj = lax.dot_general(qs, ks, NT, preferred_element_type=F32)
        a = jnp.where(lvl == j, aj, a)
    o_intra = jnp.dot(a.astype(BF16), vb, preferred_element_type=F32)
    tot_f, tot_b = sf[top][0], sb[top][0]
    qhf = (q * jnp.exp2(pf[top])).astype(BF16)
    qhb = (q * jnp.exp2(_hg_rep(tot_b, c_len) - xb[top])).astype(BF16)
    khf = (kf * jnp.exp2(_hg_rep(tot_f, c_len) - pf[top])).astype(BF16)
    khb = (kb * jnp.exp2(xb[top])).astype(BF16)
    utf = lax.dot_general(vb, khf, TN, preferred_element_type=F32)
    utb = lax.dot_general(vb, khb, TN, preferred_element_type=F32)
    return o_intra, qhf, qhb, utf, utb, jnp.exp2(tot_f[0:1, :]), jnp.exp2(tot_b[0:1, :])


def _attn_hg_kernel(*refs, nseg, tk, layer, full_seq):
    it = iter(refs)
    q_ref = next(it)
    kv = [next(it) for _ in range(2 * nseg)]
    hq_ref, zf_ref, zb_ref, hv_ref, lbp_ref = (next(it) for _ in range(5))
    if full_seq:
        g_ref, ghg_ref = next(it), next(it)
        o_ref, ohg_ref, sf_ref, sb_ref = (next(it) for _ in range(4))
    else:
        o_ref, oi_ref, qhf_ref, qhb_ref, utf_ref, utb_ref, df_ref, db_ref = (next(it) for _ in range(8))
    s_ref = next(it)
    tq = q_ref.shape[1]
    heads = q_ref.shape[2] // LANES
    hg_heads = hq_ref.shape[2] // LANES
    lane = lax.broadcasted_iota(jnp.int32, (1, LANES), 1)
    chunks = [(kv[2 * seg], kv[2 * seg + 1], c * tk)
              for seg in range(nseg) for c in range(kv[2 * seg].shape[1] // tk)]

    upper8, lvl = _hg_consts()
    lb = _hg_lower_bounds(lbp_ref[...], layer)
    local = [(ci, h) for ci in range(tq // HG_CHUNK) for h in range(hg_heads)]
    per_head = len(local) // heads
    kept = {}

    def hg_local(ci, h):
        rs = slice(ci * HG_CHUNK, (ci + 1) * HG_CHUNK)
        cs = slice(h * LANES, (h + 1) * LANES)
        terms = _hg_chunk_terms(
            hq_ref[0, rs, cs], zf_ref[0, rs, cs], zb_ref[0, rs, cs], hv_ref[0, rs, cs],
            lb[0:1, cs], lb[1:2, cs], upper8, lvl)
        if full_seq:
            kept[(ci, h)] = terms
            return
        oi, qhf, qhb, utf, utb, dec_f, dec_b = terms
        oi_ref[0, rs, cs] = oi
        qhf_ref[0, rs, cs] = qhf
        qhb_ref[0, rs, cs] = qhb
        utf_ref[0, ci, h] = utf
        utb_ref[0, ci, h] = utb
        df_ref[0, ci, :, cs] = dec_f
        db_ref[0, ci, :, cs] = dec_b

    outs = []
    for h in range(heads):
        hs = slice(h * LANES, (h + 1) * LANES)
        qh = q_ref[0, :, hs]
        mx = jnp.full((tq, LANES), -jnp.inf, F32)
        for i, (k_ref, _, off) in enumerate(chunks):
            s = lax.dot_general(qh, k_ref[0, off:off + tk, hs], NT, preferred_element_type=F32)
            s_ref[:, i * tk:(i + 1) * tk] = s
            for w in range(tk // LANES):
                mx = jnp.maximum(mx, s[:, w * LANES:(w + 1) * LANES])
        m = jnp.broadcast_to(jnp.max(mx, axis=1, keepdims=True), (tq, LANES))
        mb = jnp.concatenate([m] * (tk // LANES), axis=1)
        acc = jnp.zeros((tq, LANES), F32)
        for i, (_, v_ref, off) in enumerate(chunks):
            p = jnp.exp2(s_ref[:, i * tk:(i + 1) * tk] - mb).astype(BF16)
            acc = acc + jnp.dot(p, v_ref[0, off:off + tk, hs], preferred_element_type=F32)
        outs.append(acc / pltpu.roll(acc, V_DIM, 1))
        for ci, hh in local[h * per_head:(h + 1) * per_head]:
            hg_local(ci, hh)
    for pair in range(heads // 2):
        a = outs[2 * pair]
        b = pltpu.roll(outs[2 * pair + 1], V_DIM, 1)
        o_ref[0, :, pair * LANES:(pair + 1) * LANES] = jnp.where(lane < V_DIM, a, b).astype(BF16)

    if full_seq:
        n = tq // HG_CHUNK
        for h in range(hg_heads):
            cs = slice(h * LANES, (h + 1) * LANES)
            enter_b, st = [None] * n, None
            for c in reversed(range(n)):
                enter_b[c] = st
                _, _, _, _, utb, _, dec_b = kept[(c, h)]
                st = utb if st is None else dec_b * st + utb
            sb_ref[0, h] = st.T
            st = None
            for c in range(n):
                rs = slice(c * HG_CHUNK, (c + 1) * HG_CHUNK)
                oi, qhf, qhb, utf, _, dec_f, _ = kept[(c, h)]
                if st is not None:
                    oi = oi + lax.dot_general(qhf, st.astype(BF16), NT, preferred_element_type=F32)
                if enter_b[c] is not None:
                    oi = oi + lax.dot_general(qhb, enter_b[c].astype(BF16), NT, preferred_element_type=F32)
                ohg_ref[0, rs, cs] = (_rms(oi, ghg_ref[:, cs]) * g_ref[0, rs, cs]).astype(BF16)
                st = utf if st is None else dec_f * st + utf
            sf_ref[0, h] = st.T


def _attn_hg_call(q, kvs, hgin, lb_param, g_hg, *, tq, tk, heads, layer, full_seq, name):
    b, t, _ = q.shape
    n_chunks = t // HG_CHUNK
    cpt = tq // HG_CHUNK
    groups = MLA_HEADS // heads
    hgh = HG_HEADS // groups
    assert heads % 2 == 0 and hgh >= 1 and (cpt * hgh) % heads == 0
    aw, gw = heads * LANES, hgh * LANES
    in_specs = [pl.BlockSpec((1, tq, aw), lambda i, p, j: (i, j, p))]
    args = [q]
    n_keys = 0
    for k, v in kvs:
        n_keys += k.shape[1]
        for a in (k, v):
            in_specs.append(pl.BlockSpec((1, a.shape[1], aw), lambda i, p, j: (i, 0, p)))
            args.append(a)
    for g in range(4):
        in_specs.append(pl.BlockSpec((1, tq, gw), lambda i, p, j, g=g: (i, j, g * groups + p)))
        args.append(hgin)
    in_specs.append(pl.BlockSpec((lb_param.shape[0], 2, gw), lambda i, p, j: (0, 0, p)))
    args.append(lb_param)
    rows = lambda w: pl.BlockSpec((1, tq, w), lambda i, p, j: (i, j, p))
    o_shape = jax.ShapeDtypeStruct((b, t, MLA_HEADS * V_DIM), BF16)
    if full_seq:
        assert t == tq
        in_specs += [pl.BlockSpec((1, tq, gw), lambda i, p, j: (i, j, 4 * groups + p)),
                     pl.BlockSpec((1, gw), lambda i, p, j: (0, p))]
        args += [hgin, g_hg]
        st_spec = pl.BlockSpec((1, hgh, HG_DK, HG_DV), lambda i, p, j: (i, p, 0, 0))
        st_shape = jax.ShapeDtypeStruct((b, HG_HEADS, HG_DK, HG_DV), F32)
        out_specs = [rows(heads * V_DIM), rows(gw), st_spec, st_spec]
        out_shape = [o_shape, jax.ShapeDtypeStruct((b, t, HG_WIDTH), BF16), st_shape, st_shape]
    else:
        ut_spec = pl.BlockSpec((1, cpt, hgh, HG_DV, HG_DK), lambda i, p, j: (i, j, p, 0, 0))
        dec_spec = pl.BlockSpec((1, cpt, 1, gw), lambda i, p, j: (i, j, 0, p))
        ut_shape = jax.ShapeDtypeStruct((b, n_chunks, HG_HEADS, HG_DV, HG_DK), F32)
        dec_shape = jax.ShapeDtypeStruct((b, n_chunks, 1, HG_WIDTH), F32)
        out_specs = [rows(heads * V_DIM), rows(gw), rows(gw), rows(gw),
                     ut_spec, ut_spec, dec_spec, dec_spec]
        out_shape = [o_shape,
                     jax.ShapeDtypeStruct((b, t, HG_WIDTH), F32),
                     jax.ShapeDtypeStruct((b, t, HG_WIDTH), BF16),
                     jax.ShapeDtypeStruct((b, t, HG_WIDTH), BF16),
                     ut_shape, ut_shape, dec_shape, dec_shape]
    return pl.pallas_call(
        functools.partial(_attn_hg_kernel, nseg=len(kvs), tk=tk, layer=layer, full_seq=full_seq),
        grid=(b, groups, t // tq),
        in_specs=in_specs,
        out_specs=out_specs,
        out_shape=out_shape,
        scratch_shapes=[pltpu.VMEM((tq, n_keys), F32)],
        compiler_params=_params(("arbitrary", "arbitrary", "arbitrary")),
        name=name,
    )(*args)


def _hg_scan_kernel(*refs, has_s0):
    it = iter(refs)
    oi_ref, qhf_ref, qhb_ref, utf_ref, utb_ref, df_ref, db_ref, g_ref, ghg_ref = (next(it) for _ in range(9))
    if has_s0:
        s0f_ref, s0b_ref = next(it), next(it)
    o_ref, sf_ref, sb_ref, sbst_ref = (next(it) for _ in range(4))
    c_len = HG_CHUNK
    n_chunks = utf_ref.shape[1]
    unroll = min(HG_UNROLL, n_chunks)

    def rows_of(c):
        return pl.ds(pl.multiple_of(c * c_len, c_len), c_len)

    def bwd_body(i, st):
        c = n_chunks - 1 - i
        sbst_ref[c] = st.astype(BF16)
        return db_ref[0, c] * st + utb_ref[0, c, 0]

    st_b0 = s0b_ref[0, 0].T if has_s0 else jnp.zeros((HG_DV, HG_DK), F32)
    st_b = lax.fori_loop(0, n_chunks, bwd_body, st_b0, unroll=unroll)
    sb_ref[0, 0] = st_b.T

    ghg = ghg_ref[...]

    def fwd_body(c, st):
        qcat = jnp.concatenate([qhf_ref[0, rows_of(c), :], qhb_ref[0, rows_of(c), :]], axis=1)
        scat = jnp.concatenate([st.astype(BF16), sbst_ref[c]], axis=1)
        o = oi_ref[0, rows_of(c), :] + lax.dot_general(qcat, scat, NT, preferred_element_type=F32)
        on = _rms(o, ghg) * g_ref[0, rows_of(c), :]
        o_ref[0, rows_of(c), :] = on.astype(BF16)
        return df_ref[0, c] * st + utf_ref[0, c, 0]

    st_f0 = s0f_ref[0, 0].T if has_s0 else jnp.zeros((HG_DV, HG_DK), F32)
    st_f = lax.fori_loop(0, n_chunks, fwd_body, st_f0, unroll=unroll)
    sf_ref[0, 0] = st_f.T


def _hg_scan_call(oi, qhf, qhb, utf, utb, dec_f, dec_b, hgin, g_hg, s0, *, name):
    b, t, _ = oi.shape
    n_chunks = t // HG_CHUNK
    has_s0 = s0 is not None
    col = pl.BlockSpec((1, t, LANES), lambda i, h: (i, 0, h))
    ut_spec = pl.BlockSpec((1, n_chunks, 1, HG_DV, HG_DK), lambda i, h: (i, 0, h, 0, 0))
    dec_spec = pl.BlockSpec((1, n_chunks, 1, LANES), lambda i, h: (i, 0, 0, h))
    gate_col = pl.BlockSpec((1, t, LANES), lambda i, h: (i, 0, 4 * HG_HEADS + h))
    in_specs = [col, col, col, ut_spec, ut_spec, dec_spec, dec_spec, gate_col,
                pl.BlockSpec((1, LANES), lambda i, h: (0, h))]
    args = [oi, qhf, qhb, utf, utb, dec_f, dec_b, hgin, g_hg]
    st_spec = pl.BlockSpec((1, 1, HG_DK, HG_DV), lambda i, h: (i, h, 0, 0))
    if has_s0:
        in_specs += [st_spec, st_spec]
        args += list(s0)
    st_shape = jax.ShapeDtypeStruct((b, HG_HEADS, HG_DK, HG_DV), F32)
    return pl.pallas_call(
        functools.partial(_hg_scan_kernel, has_s0=has_s0),
        grid=(b, HG_HEADS),
        in_specs=in_specs,
        out_specs=[col, st_spec, st_spec],
        out_shape=[jax.ShapeDtypeStruct((b, t, HG_WIDTH), BF16), st_shape, st_shape],
        scratch_shapes=[pltpu.VMEM((n_chunks, HG_DV, HG_DK), BF16)],
        compiler_params=_params(("arbitrary", "arbitrary")),
        name=name,
    )(*args)


def _mix_kernel(om_ref, oh_ref, gate_ref, x_ref, mod_ref, wmo_ref, who_ref, wout_ref, gffn_ref,
                x1_ref, h2_ref):
    a = jnp.dot(om_ref[...], wmo_ref[...], preferred_element_type=F32)
    b = jnp.dot(oh_ref[...], who_ref[...], preferred_element_type=F32)
    mix = gate_ref[:, 0:D_MODEL] * a + gate_ref[:, D_MODEL:2 * D_MODEL] * b
    y = jnp.dot(mix.astype(BF16), wout_ref[...], preferred_element_type=F32)
    gt1 = mod_ref[0, 2:3, :]
    sh2 = mod_ref[0, 3:4, :]
    sc2 = mod_ref[0, 4:5, :]
    x1 = x_ref[...] + gt1 * y
    x1_ref[...] = x1
    h2_ref[...] = (_rms(x1, gffn_ref[...]) * (1.0 + sc2) + sh2).astype(BF16)


def _mix_call(om, oh, gate, x, mod, pw, *, seq_len, mod_row0, per_seq_mod, tm, name):
    n_tok = x.shape[0]
    tps = max(seq_len // tm, 1)
    if per_seq_mod:
        mod_map = lambda t: (mod_row0 + t // tps, 0, 0)
    else:
        mod_map = lambda t: (mod_row0, 0, 0)
    row = lambda w: pl.BlockSpec((tm, w), lambda t: (t, 0))
    ws = [pw[k] for k in ("w_mla_o", "w_hg_o", "w_out", "g_ffn")]
    return pl.pallas_call(
        _mix_kernel,
        grid=(n_tok // tm,),
        in_specs=[row(MLA_HEADS * V_DIM), row(HG_WIDTH), row(2 * D_MODEL), row(D_MODEL),
                  pl.BlockSpec((1, 6, D_MODEL), mod_map)] + [_const_spec(w.shape) for w in ws],
        out_specs=[row(D_MODEL), row(D_MODEL)],
        out_shape=[jax.ShapeDtypeStruct((n_tok, D_MODEL), F32),
                   jax.ShapeDtypeStruct((n_tok, D_MODEL), BF16)],
        compiler_params=_params(("arbitrary",)),
        name=name,
    )(om, oh, gate, x, mod, *ws)


def _gelu_tanh(x):
    return 0.5 * x * (1.0 + jnp.tanh(0.7978845608028654 * (x + 0.044715 * (x * x * x))))


def _ffn_kernel(h2_ref, hp_ref, hn_ref, x1_ref, mod_ref, wi_ref, cw_ref, cb_ref, wo_ref,
                gfin_ref, y_ref, ext_ref, gsc_ref, u_ref, *, tm, seq_len):
    halo = BF16_SUBLANES
    t = pl.program_id(0)
    ext_ref[0:halo, :] = hp_ref[...]
    ext_ref[halo:halo + tm, :] = h2_ref[...]
    ext_ref[halo + tm:2 * halo + tm, :] = hn_ref[...]
    pos = (lax.broadcasted_iota(jnp.int32, (tm, 1), 0) + t * tm) & (seq_len - 1)
    has_prev = pos != 0
    has_next = pos != seq_len - 1
    for j in range(D_FF // FF_CHUNK):
        cs = slice(j * FF_CHUNK, (j + 1) * FF_CHUNK)
        vs = slice(D_FF + j * FF_CHUNK, D_FF + (j + 1) * FF_CHUNK)
        gsc_ref[...] = jnp.dot(ext_ref[...], wi_ref[:, cs], preferred_element_type=F32)
        g_prev = jnp.where(has_prev, gsc_ref[halo - 1:halo - 1 + tm, :], 0.0)
        g_cur = gsc_ref[halo:halo + tm, :]
        g_next = jnp.where(has_next, gsc_ref[halo + 1:halo + 1 + tm, :], 0.0)
        gconv = cb_ref[:, cs] + g_prev * cw_ref[0:1, cs] + g_cur * cw_ref[1:2, cs] + g_next * cw_ref[2:3, cs]
        val = jnp.dot(h2_ref[...], wi_ref[:, vs], preferred_element_type=F32)
        u_ref[:, cs] = (_gelu_tanh(gconv) * val).astype(BF16)
    f = jnp.dot(u_ref[...], wo_ref[...], preferred_element_type=F32)
    gt2 = mod_ref[0, 5:6, :]
    y_ref[...] = _rms(x1_ref[...] + gt2 * f, gfin_ref[...])


def _ffn_call(h2, x1, mod, pw, *, seq_len, mod_row0, per_seq_mod, tm, name):
    n_tok = x1.shape[0]
    tps = max(seq_len // tm, 1)
    halo = BF16_SUBLANES
    n_hblk = n_tok // halo
    hpt = tm // halo
    if per_seq_mod:
        mod_map = lambda t: (mod_row0 + t // tps, 0, 0)
    else:
        mod_map = lambda t: (mod_row0, 0, 0)
    row = lambda w: pl.BlockSpec((tm, w), lambda t: (t, 0))
    ws = [pw[k] for k in ("w_ffn_in", "conv_w", "conv_b", "w_ffn_out", "g_final")]
    return pl.pallas_call(
        functools.partial(_ffn_kernel, tm=tm, seq_len=seq_len),
        grid=(n_tok // tm,),
        in_specs=[row(D_MODEL),
                  pl.BlockSpec((halo, D_MODEL), lambda t: (jnp.maximum(t * hpt - 1, 0), 0)),
                  pl.BlockSpec((halo, D_MODEL), lambda t: (jnp.minimum((t + 1) * hpt, n_hblk - 1), 0)),
                  row(D_MODEL),
                  pl.BlockSpec((1, 6, D_MODEL), mod_map)] + [_const_spec(w.shape) for w in ws],
        out_specs=row(D_MODEL),
        out_shape=jax.ShapeDtypeStruct((n_tok, D_MODEL), F32),
        scratch_shapes=[pltpu.VMEM((tm + 2 * halo, D_MODEL), BF16),
                        pltpu.VMEM((tm + 2 * halo, FF_CHUNK), F32),
                        pltpu.VMEM((tm, D_FF), BF16)],
        compiler_params=_params(("arbitrary",)),
        name=name,
    )(h2, h2, h2, x1, mod, *ws)


def _rope_tables(seq_len):
    rows = seq_len // GRID_W
    rr, cc = jnp.meshgrid(jnp.arange(rows, dtype=F32), jnp.arange(GRID_W, dtype=F32), indexing="ij")
    n_freq = QK_ROPE // 4
    inv = jnp.power(ROPE_BASE, -jnp.arange(n_freq, dtype=F32) / n_freq)
    ang = jnp.concatenate([rr.reshape(-1)[:, None] * inv, cc.reshape(-1)[:, None] * inv], axis=-1)
    cos, sin = jnp.cos(ang), jnp.sin(ang)
    z16 = jnp.zeros_like(sin)
    ones = jnp.ones((seq_len, QK_NOPE), F32)
    zeros = jnp.zeros((seq_len, QK_NOPE), F32)
    s = ATTN_SCALE
    cs = jnp.concatenate([ones * s, cos * s, cos * s, cos, cos], axis=1)
    sa = jnp.concatenate([zeros, -sin * s, z16, -sin, z16], axis=1)
    sb = jnp.concatenate([zeros, z16, sin * s, z16, sin], axis=1)
    return jnp.stack([cs, sa, sb], axis=0)


def _prep_weights(l, g_mix, w_in, g_q, w_uq, g_kv, w_ukv, w_mla_o, g_hg, w_hg_o, w_out, g_ffn,
                  w_ffn_in, conv_w, conv_b, w_ffn_out, g_final):
    d = D_MODEL
    wi = w_in[l]
    c_kr = Q_LORA + KV_LORA
    wi = wi.astype(BF16)
    w_in_p = jnp.concatenate([wi[:, :c_kr], jnp.zeros((d, KR_LANE0), BF16),
                              wi[:, c_kr:c_kr + QK_ROPE], wi[:, c_kr + QK_ROPE:]], axis=1)
    hd = QK_NOPE + QK_ROPE
    w_uq_p = jnp.concatenate([w_uq[l].reshape(Q_LORA, MLA_HEADS, hd),
                              jnp.zeros((Q_LORA, MLA_HEADS, LANES - hd), F32)], axis=-1)
    wkv = w_ukv[l].reshape(KV_LORA, MLA_HEADS, QK_NOPE + V_DIM)
    zpad = jnp.zeros((KV_LORA, MLA_HEADS, LANES - QK_NOPE), F32)
    w_k_top = jnp.concatenate([wkv[..., :QK_NOPE], zpad], axis=-1).reshape(KV_LORA, MLA_HEADS * LANES)
    place = jnp.zeros((LANES, LANES), F32).at[
        KR_LANE0 + jnp.arange(QK_ROPE), QK_NOPE + jnp.arange(QK_ROPE)].set(1.0)
    w_k_bot = jnp.tile(place, (1, MLA_HEADS))
    w_k = jnp.concatenate([w_k_top, w_k_bot], axis=0)
    w_v = jnp.concatenate([wkv[..., QK_NOPE:], jnp.zeros((KV_LORA, MLA_HEADS, LANES - V_DIM), F32)],
                          axis=-1).reshape(KV_LORA, MLA_HEADS * LANES)
    return {
        "g_mix": g_mix[l].reshape(1, d), "w_in": w_in_p,
        "g_q": g_q[l].reshape(1, Q_LORA), "w_uq": w_uq_p.reshape(Q_LORA, MLA_HEADS * LANES).astype(BF16),
        "g_kv": g_kv[l].reshape(1, KV_LORA), "w_k": w_k.astype(BF16), "w_v": w_v.astype(BF16),
        "w_mla_o": w_mla_o[l].astype(BF16), "w_hg_o": w_hg_o[l].astype(BF16),
        "w_out": w_out[l].astype(BF16), "g_ffn": g_ffn[l].reshape(1, d),
        "w_ffn_in": w_ffn_in[l].astype(BF16),
        "conv_w": conv_w[l], "conv_b": conv_b[l].reshape(1, D_FF),
        "w_ffn_out": w_ffn_out[l].astype(BF16), "g_final": g_final.reshape(1, d),
        "g_hg": g_hg[l].reshape(1, HG_WIDTH),
    }


def _trunk(x, mod, tabs, cache, pw, lb_param, *, layer, batch, seq_len, mod_row0, per_seq_mod,
           emit_cache, tm, tq, tag):
    n_tok = batch * seq_len
    hw = MLA_HEADS * LANES
    outs = _proj_call(x, mod, tabs, pw, seq_len=seq_len, mod_row0=mod_row0, per_seq_mod=per_seq_mod,
                      emit_cache=emit_cache, tm=tm)
    q, k, v, hgin, gate = outs[:5]
    hgin = hgin.reshape(batch, seq_len, 5 * HG_WIDTH)
    kvs = [(k.reshape(batch, seq_len, hw), v.reshape(batch, seq_len, hw))]
    s0 = None
    if cache is not None:
        ckv_c, kr_c, s0f, s0b = cache
        past = ckv_c.shape[1]
        kc, vc = _cachekv_call(ckv_c.reshape(batch * past, KV_LORA), kr_c.reshape(batch * past, QK_ROPE),
                               pw, tm=min(512, batch * past))
        kvs.append((kc.reshape(batch, past, hw), vc.reshape(batch, past, hw)))
        s0 = (s0f, s0b)
    full_seq = seq_len == tq and s0 is None
    o_mla, *hg_out = _attn_hg_call(q.reshape(batch, seq_len, hw), kvs, hgin, lb_param, pw["g_hg"], tq=tq,
                                   tk=min(512, seq_len),
                                   heads=MLA_HEADS if seq_len == tq else ATTN_HEADS_PER_STEP,
                                   layer=layer, full_seq=full_seq, name="attn_hg_" + tag)
    if full_seq:
        o_hg, s_f, s_b = hg_out
    else:
        o_hg, s_f, s_b = _hg_scan_call(*hg_out, hgin, pw["g_hg"], s0, name="hg_scan_" + tag)
    common = dict(seq_len=seq_len, mod_row0=mod_row0, per_seq_mod=per_seq_mod, tm=tm)
    x1, h2 = _mix_call(o_mla.reshape(n_tok, MLA_HEADS * V_DIM), o_hg.reshape(n_tok, HG_WIDTH), gate, x,
                       mod, pw, name="mix_" + tag, **common)
    y = _ffn_call(h2, x1, mod, pw, name="ffn_" + tag, **common)
    extra = outs[5:] if emit_cache else None
    return y, extra, s_f, s_b


def kernel(x_prompt, x_sample, cache_ckv, cache_krope, state_hgrn_fwd, state_hgrn_bwd, c, c_ctx, w_mod,
           b_mod, g_mix, w_in, g_q, w_uq, g_kv, w_ukv, w_mla_o, lb_param, g_hg, w_hg_o, w_out, g_ffn,
           w_ffn_in, conv_w, conv_b, w_ffn_out, g_final):
    batch, seq, d = x_prompt.shape
    dec_batch, dec_seq, _ = x_sample.shape
    depth = w_in.shape[0]
    assert depth == 1, "single trunk layer"
    l = 0
    cvec = jnp.concatenate([c_ctx[None, :], c, jnp.zeros((8 - 1 - dec_batch, d), F32)], axis=0)
    mod = _mod_call(cvec, w_mod[l], b_mod[l]).reshape(8, 6, d)
    pw = _prep_weights(l, g_mix, w_in, g_q, w_uq, g_kv, w_ukv, w_mla_o, g_hg, w_hg_o, w_out, g_ffn,
                       w_ffn_in, conv_w, conv_b, w_ffn_out, g_final)
    tabs = _rope_tables(dec_seq)

    y_ctx, cache_out, s_f, s_b = _trunk(
        x_prompt.reshape(batch * seq, d), mod, None, None, pw, lb_param, layer=l, batch=batch,
        seq_len=seq, mod_row0=0, per_seq_mod=False, emit_cache=True, tm=TOKEN_TILE, tq=min(Q_TILE, seq),
        tag="ctx")
    cache_l = (cache_ckv[:, l], cache_krope[:, l], state_hgrn_fwd[:, l], state_hgrn_bwd[:, l])
    y_lat, _, _, _ = _trunk(
        x_sample.reshape(dec_batch * dec_seq, d), mod, tabs, cache_l, pw, lb_param, layer=l,
        batch=dec_batch, seq_len=dec_seq, mod_row0=1, per_seq_mod=True, emit_cache=False, tm=TOKEN_TILE,
        tq=min(Q_TILE, dec_seq), tag="lat")

    ckv_new, kr_new = cache_out
    return (y_ctx.reshape(batch, seq, d), y_lat.reshape(dec_batch, dec_seq, d),
            ckv_new.reshape(batch, 1, seq, KV_LORA), kr_new.reshape(batch, 1, seq, QK_ROPE),
            s_f.reshape(batch, 1, HG_HEADS, HG_DK, HG_DV), s_b.reshape(batch, 1, HG_HEADS, HG_DK, HG_DV))
```

```python
import functools

import jax
import jax.numpy as jnp
from jax import lax
from jax.experimental import pallas as pl
from jax.experimental.pallas import tpu as pltpu

F32 = jnp.float32
BF16 = jnp.bfloat16

D_MODEL = 1024
GRID_W = 64
MLA_HEADS = 8
QK_NOPE = 64
QK_ROPE = 32
V_DIM = 64
Q_LORA = 384
KV_LORA = 256
ROPE_BASE = 10000.0
HG_HEADS = 4
HG_DK = 128
HG_DV = 128
HG_WIDTH = HG_HEADS * HG_DV
D_FF = 2816
EPS = 1e-6

LANES = 128
SUBLANES = 8
BF16_SUBLANES = 16
HG_CHUNK = 128
HG_LEVELS = 7
HG_UNROLL = 4
FF_CHUNK = 256
TOKEN_TILE = 512
Q_TILE = 512
ATTN_HEADS_PER_STEP = 4
VMEM_LIMIT = 56 * 1024 * 1024

C_PQ = 0
C_PKV = Q_LORA
C_PKR = C_PKV + KV_LORA
C_HQ = C_PKR + QK_ROPE
C_ZF = C_HQ + HG_WIDTH
C_HG = C_ZF + 3 * HG_WIDTH
C_GA = C_HG + HG_WIDTH
C_END = C_GA + 2 * D_MODEL
KR_LANE0 = LANES - QK_ROPE

ATTN_SCALE = (QK_NOPE + QK_ROPE) ** -0.5 * 1.4426950408889634

NT = (((1,), (1,)), ((), ()))
TN = (((0,), (0,)), ((), ()))


def _rms(x, g):
    return x * lax.rsqrt(jnp.mean(x * x, axis=-1, keepdims=True) + EPS) * g


def _const_spec(shape):
    n = len(shape)
    return pl.BlockSpec(shape, lambda *_: (0,) * n, pipeline_mode=pl.Buffered(1))


def _params(sem):
    return pltpu.CompilerParams(dimension_semantics=sem, vmem_limit_bytes=VMEM_LIMIT)


def _mod_kernel(c_ref, w_ref, b_ref, o_ref):
    c = c_ref[...]
    a = c * jax.nn.sigmoid(c)
    o_ref[...] = jnp.dot(a, w_ref[...], precision=lax.Precision.HIGHEST,
                         preferred_element_type=F32) + b_ref[...]


def _mod_call(cvec, w_mod, b_mod):
    n = w_mod.shape[1]
    tn = 1024
    return pl.pallas_call(
        _mod_kernel,
        grid=(n // tn,),
        in_specs=[pl.BlockSpec((8, D_MODEL), lambda j: (0, 0)),
                  pl.BlockSpec((D_MODEL, tn), lambda j: (0, j)),
                  pl.BlockSpec((1, tn), lambda j: (0, j))],
        out_specs=pl.BlockSpec((8, tn), lambda j: (0, j)),
        out_shape=jax.ShapeDtypeStruct((8, n), F32),
        compiler_params=_params(("arbitrary",)),
        name="mod",
    )(cvec, w_mod, b_mod.reshape(1, n))


def _proj_kernel(*refs, rope, emit_cache):
    it = iter(refs)
    x_ref = next(it)
    mod_ref = next(it)
    tab_ref = next(it) if rope else None
    gmix_ref, win_ref, gq_ref, wuq_ref, gkv_ref, wk_ref, wv_ref = (next(it) for _ in range(7))
    q_ref, k_ref, v_ref, hg_ref, gate_ref = (next(it) for _ in range(5))
    if emit_cache:
        ckv_ref, kr_ref = next(it), next(it)

    sh1 = mod_ref[0, 0:1, :]
    sc1 = mod_ref[0, 1:2, :]
    hb = (_rms(x_ref[...], gmix_ref[...]) * (1.0 + sc1) + sh1).astype(BF16)

    def proj(c0, c1):
        return lax.dot_general(hb, win_ref[c0:c1, :], NT, preferred_element_type=F32)

    if rope:
        cs1, sa1, sb1 = tab_ref[0], tab_ref[1], tab_ref[2]

        def rot(x, reps):
            w = x.shape[1]
            cs = jnp.concatenate([cs1] * reps, axis=1) if reps > 1 else cs1
            sa = jnp.concatenate([sa1] * reps, axis=1) if reps > 1 else sa1
            sb = jnp.concatenate([sb1] * reps, axis=1) if reps > 1 else sb1
            half = QK_ROPE // 2
            return x * cs + pltpu.roll(x, w - half, 1) * sa + pltpu.roll(x, half, 1) * sb

    qn = _rms(proj(C_PQ, C_PKV), gq_ref[...]).astype(BF16)
    q = jnp.dot(qn, wuq_ref[...], preferred_element_type=F32)
    q = rot(q, MLA_HEADS) if rope else q * ATTN_SCALE
    q_ref[...] = q.astype(BF16)

    ckv = _rms(proj(C_PKV, C_PKR), gkv_ref[...])
    pkr = proj(C_HQ - LANES, C_HQ)
    if emit_cache:
        ckv_ref[...] = ckv
        kr_ref[...] = pkr[:, KR_LANE0:]
    ckv_b = ckv.astype(BF16)
    kr = rot(pkr, 1) if rope else pkr
    kin = jnp.concatenate([ckv_b, kr.astype(BF16)], axis=1)
    k_ref[...] = jnp.dot(kin, wk_ref[...], preferred_element_type=F32).astype(BF16)
    lane = lax.broadcasted_iota(jnp.int32, (1, MLA_HEADS * LANES), 1)
    ones = jnp.where((lane & (LANES - 1)) >= V_DIM, 1.0, 0.0)
    v_ref[...] = (jnp.dot(ckv_b, wv_ref[...], preferred_element_type=F32) + ones).astype(BF16)

    hq = proj(C_HQ, C_ZF)
    hg_ref[:, 0:HG_WIDTH] = hq * jax.nn.sigmoid(hq)
    hg_ref[:, HG_WIDTH:4 * HG_WIDTH] = proj(C_ZF, C_HG)
    og = proj(C_HG, C_GA)
    hg_ref[:, 4 * HG_WIDTH:5 * HG_WIDTH] = og * jax.nn.sigmoid(og)

    gate_ref[...] = jax.nn.sigmoid(proj(C_GA, C_END)).astype(BF16)


def _proj_call(x, mod, tabs, pw, *, seq_len, mod_row0, per_seq_mod, emit_cache, tm):
    n_tok = x.shape[0]
    tps = max(seq_len // tm, 1)
    rope = tabs is not None
    if per_seq_mod:
        mod_map = lambda t: (mod_row0 + t // tps, 0, 0)
    else:
        mod_map = lambda t: (mod_row0, 0, 0)
    row = lambda w: pl.BlockSpec((tm, w), lambda t: (t, 0))
    in_specs = [row(D_MODEL), pl.BlockSpec((1, 6, D_MODEL), mod_map)]
    args = [x, mod]
    if rope:
        in_specs.append(pl.BlockSpec((3, tm, LANES), lambda t: (0, t % tps, 0)))
        args.append(tabs)
    for name in ("g_mix", "w_in", "g_q", "w_uq", "g_kv", "w_k", "w_v"):
        in_specs.append(_const_spec(pw[name].shape))
        args.append(pw[name])
    hw = MLA_HEADS * LANES
    out_shape = [jax.ShapeDtypeStruct((n_tok, hw), BF16),
                 jax.ShapeDtypeStruct((n_tok, hw), BF16),
                 jax.ShapeDtypeStruct((n_tok, hw), BF16),
                 jax.ShapeDtypeStruct((n_tok, 5 * HG_WIDTH), F32),
                 jax.ShapeDtypeStruct((n_tok, 2 * D_MODEL), BF16)]
    out_specs = [row(hw), row(hw), row(hw), row(5 * HG_WIDTH), row(2 * D_MODEL)]
    if emit_cache:
        out_shape += [jax.ShapeDtypeStruct((n_tok, KV_LORA), F32),
                      jax.ShapeDtypeStruct((n_tok, QK_ROPE), F32)]
        out_specs += [row(KV_LORA), row(QK_ROPE)]
    return pl.pallas_call(
        functools.partial(_proj_kernel, rope=rope, emit_cache=emit_cache),
        grid=(n_tok // tm,),
        in_specs=in_specs,
        out_specs=out_specs,
        out_shape=out_shape,
        compiler_params=_params(("arbitrary",)),
        name="proj_lat" if rope else "proj_ctx",
    )(*args)


def _cachekv_kernel(ckv_ref, kr_ref, wk_ref, wv_ref, k_ref, v_ref):
    ckv_b = ckv_ref[...].astype(BF16)
    tm = ckv_b.shape[0]
    kr = jnp.concatenate([jnp.zeros((tm, KR_LANE0), BF16), kr_ref[...].astype(BF16)], axis=1)
    kin = jnp.concatenate([ckv_b, kr], axis=1)
    k_ref[...] = jnp.dot(kin, wk_ref[...], preferred_element_type=F32).astype(BF16)
    lane = lax.broadcasted_iota(jnp.int32, (1, MLA_HEADS * LANES), 1)
    ones = jnp.where((lane & (LANES - 1)) >= V_DIM, 1.0, 0.0)
    v_ref[...] = (jnp.dot(ckv_b, wv_ref[...], preferred_element_type=F32) + ones).astype(BF16)


def _cachekv_call(ckv, kr, pw, tm):
    n_tok = ckv.shape[0]
    hw = MLA_HEADS * LANES
    row = lambda w: pl.BlockSpec((tm, w), lambda t: (t, 0))
    return pl.pallas_call(
        _cachekv_kernel,
        grid=(n_tok // tm,),
        in_specs=[row(KV_LORA), row(QK_ROPE), _const_spec(pw["w_k"].shape), _const_spec(pw["w_v"].shape)],
        out_specs=[row(hw), row(hw)],
        out_shape=[jax.ShapeDtypeStruct((n_tok, hw), BF16)] * 2,
        compiler_params=_params(("arbitrary",)),
        name="cache_kv",
    )(ckv, kr, pw["w_k"], pw["w_v"])


def _hg_consts():
    c_len, sub = HG_CHUNK, SUBLANES
    row8 = lax.broadcasted_iota(jnp.int32, (1, sub, LANES), 1)
    upper8 = [(row8 & (1 << j)) != 0 for j in range(3)]
    ti = lax.broadcasted_iota(jnp.int32, (c_len, c_len), 0)
    si = lax.broadcasted_iota(jnp.int32, (c_len, c_len), 1)
    txs = ti ^ si
    lvl = jnp.full((c_len, c_len), -1, jnp.int32)
    for j in range(HG_LEVELS):
        lvl = lvl + jnp.where(txs >= (1 << j), 1, 0)
    return upper8, lvl


def _hg_lower_bounds(lbp, layer):
    e = jnp.exp(lbp - jnp.max(lbp, axis=0, keepdims=True))
    return jnp.sum(e[:layer + 1], axis=0) / jnp.sum(e, axis=0)


def _hg_gates(z, lbv):
    f = lbv + (1.0 - lbv) * jax.nn.sigmoid(z)
    return jnp.log2(f), 1.0 - f


def _hg_rep(slab, n_rows):
    return slab if n_rows == SUBLANES else jnp.concatenate([slab] * (n_rows // SUBLANES), axis=0)


def _hg_halves(j):
    h = 1 << j
    return [(slice(b, b + h), slice(b + h, b + 2 * h)) for b in range(0, HG_CHUNK, 2 * h)]


def _hg_sel8(upper8, j, a, b):
    shape3 = (HG_CHUNK // SUBLANES, SUBLANES, LANES)
    return jnp.where(upper8[j], a.reshape(shape3), b.reshape(shape3)).reshape(HG_CHUNK, LANES)


def _hg_upsweep(x, inclusive, upper8):
    c_len, sub = HG_CHUNK, SUBLANES
    x3 = x.reshape(c_len // sub, sub, LANES)
    pre3 = [x3 if inclusive else jnp.zeros_like(x3)]
    tot3 = [x3]
    for j in range(3):
        h = 1 << j
        t = tot3[-1]
        if 2 * h == sub:
            sib = pltpu.roll(t, h, 1)
        else:
            sib = jnp.where(upper8[j], pltpu.roll(t, h, 1), pltpu.roll(t, sub - h, 1))
        tot3.append(t + sib)
        pre3.append(pre3[-1] + jnp.where(upper8[j], sib, 0.0))
    pre = [p.reshape(c_len, LANES) for p in pre3]
    tot = [t.reshape(c_len, LANES) for t in tot3]
    slabs = {3: [tot[3][r:r + sub] for r in range(0, c_len, sub)]}
    for j in range(3, HG_LEVELS):
        h = 1 << j
        p, s = pre[-1], slabs[j]
        parts = []
        for i, (lo, up) in enumerate(_hg_halves(j)):
            parts += [p[lo], p[up] + _hg_rep(s[2 * i], h)]
        pre.append(jnp.concatenate(parts, axis=0))
        slabs[j + 1] = [s[2 * i] + s[2 * i + 1] for i in range(len(s) // 2)]
    return pre, tot, slabs


def _hg_chunk_terms(q, zf, zb, v, lb_f, lb_b, upper8, lvl):
    c_len, top = HG_CHUNK, HG_LEVELS
    vb = v.astype(BF16)
    lgf, kf = _hg_gates(zf, lb_f)
    lgb, kb = _hg_gates(zb, lb_b)
    pf, tf, sf = _hg_upsweep(lgf, True, upper8)
    xb, tb, sb = _hg_upsweep(lgb, False, upper8)
    a = lax.dot_general(q.astype(BF16), (kf + kb).astype(BF16), NT, preferred_element_type=F32)
    a = jnp.where(lvl == -1, a, 0.0)
    for j in range(HG_LEVELS):
        h = 1 << j
        if j < 3:
            eq = _hg_sel8(upper8, j, pf[j], tb[j] - xb[j])
            ek = _hg_sel8(upper8, j, xb[j], tf[j] - pf[j])
            ksel = _hg_sel8(upper8, j, kb, kf)
        else:
            eqs, eks, kss = [], [], []
            for i, (lo, up) in enumerate(_hg_halves(j)):
                eqs += [_hg_rep(sb[j][2 * i], h) - xb[j][lo], pf[j][up]]
                eks += [_hg_rep(sf[j][2 * i], h) - pf[j][lo], xb[j][up]]
                kss += [kf[lo], kb[up]]
            eq, ek, ksel = (jnp.concatenate(p, axis=0) for p in (eqs, eks, kss))
        qs = (q * jnp.exp2(eq)).astype(BF16)
        ks = (ksel * jnp.exp2(ek)).astype(BF16)
        aj = lax.dot_general(qs, ks, NT, preferred_element_type=F32)
        a = jnp.where(lvl == j, aj, a)
    o_intra = jnp.dot(a.astype(BF16), vb, preferred_element_type=F32)
    tot_f, tot_b = sf[top][0], sb[top][0]
    qhf = (q * jnp.exp2(pf[top])).astype(BF16)
    qhb = (q * jnp.exp2(_hg_rep(tot_b, c_len) - xb[top])).astype(BF16)
    khf = (kf * jnp.exp2(_hg_rep(tot_f, c_len) - pf[top])).astype(BF16)
    khb = (kb * jnp.exp2(xb[top])).astype(BF16)
    utf = lax.dot_general(vb, khf, TN, preferred_element_type=F32)
    utb = lax.dot_general(vb, khb, TN, preferred_element_type=F32)
    return o_intra, qhf, qhb, utf, utb, jnp.exp2(tot_f[0:1, :]), jnp.exp2(tot_b[0:1, :])


def _attn_hg_kernel(*refs, nseg, tk, layer, full_seq):
    it = iter(refs)
    q_ref = next(it)
    kv = [next(it) for _ in range(2 * nseg)]
    hq_ref, zf_ref, zb_ref, hv_ref, lbp_ref = (next(it) for _ in range(5))
    if full_seq:
        g_ref, ghg_ref = next(it), next(it)
        o_ref, ohg_ref, sf_ref, sb_ref = (next(it) for _ in range(4))
    else:
        o_ref, oi_ref, qhf_ref, qhb_ref, utf_ref, utb_ref, df_ref, db_ref = (next(it) for _ in range(8))
    s_ref = next(it)
    tq = q_ref.shape[1]
    heads = q_ref.shape[2] // LANES
    hg_heads = hq_ref.shape[2] // LANES
    lane = lax.broadcasted_iota(jnp.int32, (1, LANES), 1)
    chunks = [(kv[2 * seg], kv[2 * seg + 1], c * tk)
              for seg in range(nseg) for c in range(kv[2 * seg].shape[1] // tk)]

    upper8, lvl = _hg_consts()
    lb = _hg_lower_bounds(lbp_ref[...], layer)
    local = [(ci, h) for ci in range(tq // HG_CHUNK) for h in range(hg_heads)]
    per_head = len(local) // heads
    kept = {}

    def hg_local(ci, h):
        rs = slice(ci * HG_CHUNK, (ci + 1) * HG_CHUNK)
        cs = slice(h * LANES, (h + 1) * LANES)
        terms = _hg_chunk_terms(
            hq_ref[0, rs, cs], zf_ref[0, rs, cs], zb_ref[0, rs, cs], hv_ref[0, rs, cs],
            lb[0:1, cs], lb[1:2, cs], upper8, lvl)
        if full_seq:
            kept[(ci, h)] = terms
            return
        oi, qhf, qhb, utf, utb, dec_f, dec_b = terms
        oi_ref[0, rs, cs] = oi
        qhf_ref[0, rs, cs] = qhf
        qhb_ref[0, rs, cs] = qhb
        utf_ref[0, ci, h] = utf
        utb_ref[0, ci, h] = utb
        df_ref[0, ci, :, cs] = dec_f
        db_ref[0, ci, :, cs] = dec_b

    outs = []
    for h in range(heads):
        hs = slice(h * LANES, (h + 1) * LANES)
        qh = q_ref[0, :, hs]
        mx = jnp.full((tq, LANES), -jnp.inf, F32)
        for i, (k_ref, _, off) in enumerate(chunks):
            s = lax.dot_general(qh, k_ref[0, off:off + tk, hs], NT, preferred_element_type=F32)
            s_ref[:, i * tk:(i + 1) * tk] = s
            for w in range(tk // LANES):
                mx = jnp.maximum(mx, s[:, w * LANES:(w + 1) * LANES])
        m = jnp.broadcast_to(jnp.max(mx, axis=1, keepdims=True), (tq, LANES))
        mb = jnp.concatenate([m] * (tk // LANES), axis=1)
        acc = jnp.zeros((tq, LANES), F32)
        for i, (_, v_ref, off) in enumerate(chunks):
            p = jnp.exp2(s_ref[:, i * tk:(i + 1) * tk] - mb).astype(BF16)
            acc = acc + jnp.dot(p, v_ref[0, off:off + tk, hs], preferred_element_type=F32)
        outs.append(acc / pltpu.roll(acc, V_DIM, 1))
        for ci, hh in local[h * per_head:(h + 1) * per_head]:
            hg_local(ci, hh)
    for pair in range(heads // 2):
        a = outs[2 * pair]
        b = pltpu.roll(outs[2 * pair + 1], V_DIM, 1)
        o_ref[0, :, pair * LANES:(pair + 1) * LANES] = jnp.where(lane < V_DIM, a, b).astype(BF16)

    if full_seq:
        n = tq // HG_CHUNK
        for h in range(hg_heads):
            cs = slice(h * LANES, (h + 1) * LANES)
            enter_b, st = [None] * n, None
            for c in reversed(range(n)):
                enter_b[c] = st
                _, _, _, _, utb, _, dec_b = kept[(c, h)]
                st = utb if st is None else dec_b * st + utb
            sb_ref[0, h] = st.T
            st = None
            for c in range(n):
                rs = slice(c * HG_CHUNK, (c + 1) * HG_CHUNK)
                oi, qhf, qhb, utf, _, dec_f, _ = kept[(c, h)]
                if st is not None:
                    oi = oi + lax.dot_general(qhf, st.astype(BF16), NT, preferred_element_type=F32)
                if enter_b[c] is not None:
                    oi = oi + lax.dot_general(qhb, enter_b[c].astype(BF16), NT, preferred_element_type=F32)
                ohg_ref[0, rs, cs] = (_rms(oi, ghg_ref[:, cs]) * g_ref[0, rs, cs]).astype(BF16)
                st = utf if st is None else dec_f * st + utf
            sf_ref[0, h] = st.T


def _attn_hg_call(q, kvs, hgin, lb_param, g_hg, *, tq, tk, heads, layer, full_seq, name):
    b, t, _ = q.shape
    n_chunks = t // HG_CHUNK
    cpt = tq // HG_CHUNK
    groups = MLA_HEADS // heads
    hgh = HG_HEADS // groups
    assert heads % 2 == 0 and hgh >= 1 and (cpt * hgh) % heads == 0
    aw, gw = heads * LANES, hgh * LANES
    in_specs = [pl.BlockSpec((1, tq, aw), lambda i, p, j: (i, j, p))]
    args = [q]
    n_keys = 0
    for k, v in kvs:
        n_keys += k.shape[1]
        for a in (k, v):
            in_specs.append(pl.BlockSpec((1, a.shape[1], aw), lambda i, p, j: (i, 0, p)))
            args.append(a)
    for g in range(4):
        in_specs.append(pl.BlockSpec((1, tq, gw), lambda i, p, j, g=g: (i, j, g * groups + p)))
        args.append(hgin)
    in_specs.append(pl.BlockSpec((lb_param.shape[0], 2, gw), lambda i, p, j: (0, 0, p)))
    args.append(lb_param)
    rows = lambda w: pl.BlockSpec((1, tq, w), lambda i, p, j: (i, j, p))
    o_shape = jax.ShapeDtypeStruct((b, t, MLA_HEADS * V_DIM), BF16)
    if full_seq:
        assert t == tq
        in_specs += [pl.BlockSpec((1, tq, gw), lambda i, p, j: (i, j, 4 * groups + p)),
                     pl.BlockSpec((1, gw), lambda i, p, j: (0, p))]
        args += [hgin, g_hg]
        st_spec = pl.BlockSpec((1, hgh, HG_DK, HG_DV), lambda i, p, j: (i, p, 0, 0))
        st_shape = jax.ShapeDtypeStruct((b, HG_HEADS, HG_DK, HG_DV), F32)
        out_specs = [rows(heads * V_DIM), rows(gw), st_spec, st_spec]
        out_shape = [o_shape, jax.ShapeDtypeStruct((b, t, HG_WIDTH), BF16), st_shape, st_shape]
    else:
        ut_spec = pl.BlockSpec((1, cpt, hgh, HG_DV, HG_DK), lambda i, p, j: (i, j, p, 0, 0))
        dec_spec = pl.BlockSpec((1, cpt, 1, gw), lambda i, p, j: (i, j, 0, p))
        ut_shape = jax.ShapeDtypeStruct((b, n_chunks, HG_HEADS, HG_DV, HG_DK), F32)
        dec_shape = jax.ShapeDtypeStruct((b, n_chunks, 1, HG_WIDTH), F32)
        out_specs = [rows(heads * V_DIM), rows(gw), rows(gw), rows(gw),
                     ut_spec, ut_spec, dec_spec, dec_spec]
        out_shape = [o_shape,
                     jax.ShapeDtypeStruct((b, t, HG_WIDTH), F32),
                     jax.ShapeDtypeStruct((b, t, HG_WIDTH), BF16),
                     jax.ShapeDtypeStruct((b, t, HG_WIDTH), BF16),
                     ut_shape, ut_shape, dec_shape, dec_shape]
    return pl.pallas_call(
        functools.partial(_attn_hg_kernel, nseg=len(kvs), tk=tk, layer=layer, full_seq=full_seq),
        grid=(b, groups, t // tq),
        in_specs=in_specs,
        out_specs=out_specs,
        out_shape=out_shape,
        scratch_shapes=[pltpu.VMEM((tq, n_keys), F32)],
        compiler_params=_params(("arbitrary", "arbitrary", "arbitrary")),
        name=name,
    )(*args)


def _hg_scan_kernel(*refs, has_s0):
    it = iter(refs)
    oi_ref, qhf_ref, qhb_ref, utf_ref, utb_ref, df_ref, db_ref, g_ref, ghg_ref = (next(it) for _ in range(9))
    if has_s0:
        s0f_ref, s0b_ref = next(it), next(it)
    o_ref, sf_ref, sb_ref, sbst_ref = (next(it) for _ in range(4))
    c_len = HG_CHUNK
    n_chunks = utf_ref.shape[1]
    unroll = min(HG_UNROLL, n_chunks)

    def rows_of(c):
        return pl.ds(pl.multiple_of(c * c_len, c_len), c_len)

    def bwd_body(i, st):
        c = n_chunks - 1 - i
        sbst_ref[c] = st.astype(BF16)
        return db_ref[0, c] * st + utb_ref[0, c, 0]

    st_b0 = s0b_ref[0, 0].T if has_s0 else jnp.zeros((HG_DV, HG_DK), F32)
    st_b = lax.fori_loop(0, n_chunks, bwd_body, st_b0, unroll=unroll)
    sb_ref[0, 0] = st_b.T

    ghg = ghg_ref[...]

    def fwd_body(c, st):
        qcat = jnp.concatenate([qhf_ref[0, rows_of(c), :], qhb_ref[0, rows_of(c), :]], axis=1)
        scat = jnp.concatenate([st.astype(BF16), sbst_ref[c]], axis=1)
        o = oi_ref[0, rows_of(c), :] + lax.dot_general(qcat, scat, NT, preferred_element_type=F32)
        on = _rms(o, ghg) * g_ref[0, rows_of(c), :]
        o_ref[0, rows_of(c), :] = on.astype(BF16)
        return df_ref[0, c] * st + utf_ref[0, c, 0]

    st_f0 = s0f_ref[0, 0].T if has_s0 else jnp.zeros((HG_DV, HG_DK), F32)
    st_f = lax.fori_loop(0, n_chunks, fwd_body, st_f0, unroll=unroll)
    sf_ref[0, 0] = st_f.T


def _hg_scan_call(oi, qhf, qhb, utf, utb, dec_f, dec_b, hgin, g_hg, s0, *, name):
    b, t, _ = oi.shape
    n_chunks = t // HG_CHUNK
    has_s0 = s0 is not None
    col = pl.BlockSpec((1, t, LANES), lambda i, h: (i, 0, h))
    ut_spec = pl.BlockSpec((1, n_chunks, 1, HG_DV, HG_DK), lambda i, h: (i, 0, h, 0, 0))
    dec_spec = pl.BlockSpec((1, n_chunks, 1, LANES), lambda i, h: (i, 0, 0, h))
    gate_col = pl.BlockSpec((1, t, LANES), lambda i, h: (i, 0, 4 * HG_HEADS + h))
    in_specs = [col, col, col, ut_spec, ut_spec, dec_spec, dec_spec, gate_col,
                pl.BlockSpec((1, LANES), lambda i, h: (0, h))]
    args = [oi, qhf, qhb, utf, utb, dec_f, dec_b, hgin, g_hg]
    st_spec = pl.BlockSpec((1, 1, HG_DK, HG_DV), lambda i, h: (i, h, 0, 0))
    if has_s0:
        in_specs += [st_spec, st_spec]
        args += list(s0)
    st_shape = jax.ShapeDtypeStruct((b, HG_HEADS, HG_DK, HG_DV), F32)
    return pl.pallas_call(
        functools.partial(_hg_scan_kernel, has_s0=has_s0),
        grid=(b, HG_HEADS),
        in_specs=in_specs,
        out_specs=[col, st_spec, st_spec],
        out_shape=[jax.ShapeDtypeStruct((b, t, HG_WIDTH), BF16), st_shape, st_shape],
        scratch_shapes=[pltpu.VMEM((n_chunks, HG_DV, HG_DK), BF16)],
        compiler_params=_params(("arbitrary", "arbitrary")),
        name=name,
    )(*args)


def _mix_kernel(om_ref, oh_ref, gate_ref, x_ref, mod_ref, wmo_ref, who_ref, wout_ref, gffn_ref,
                x1_ref, h2_ref):
    a = jnp.dot(om_ref[...], wmo_ref[...], preferred_element_type=F32)
    b = jnp.dot(oh_ref[...], who_ref[...], preferred_element_type=F32)
    mix = gate_ref[:, 0:D_MODEL] * a + gate_ref[:, D_MODEL:2 * D_MODEL] * b
    y = jnp.dot(mix.astype(BF16), wout_ref[...], preferred_element_type=F32)
    gt1 = mod_ref[0, 2:3, :]
    sh2 = mod_ref[0, 3:4, :]
    sc2 = mod_ref[0, 4:5, :]
    x1 = x_ref[...] + gt1 * y
    x1_ref[...] = x1
    h2_ref[...] = (_rms(x1, gffn_ref[...]) * (1.0 + sc2) + sh2).astype(BF16)


def _mix_call(om, oh, gate, x, mod, pw, *, seq_len, mod_row0, per_seq_mod, tm, name):
    n_tok = x.shape[0]
    tps = max(seq_len // tm, 1)
    if per_seq_mod:
        mod_map = lambda t: (mod_row0 + t // tps, 0, 0)
    else:
        mod_map = lambda t: (mod_row0, 0, 0)
    row = lambda w: pl.BlockSpec((tm, w), lambda t: (t, 0))
    ws = [pw[k] for k in ("w_mla_o", "w_hg_o", "w_out", "g_ffn")]
    return pl.pallas_call(
        _mix_kernel,
        grid=(n_tok // tm,),
        in_specs=[row(MLA_HEADS * V_DIM), row(HG_WIDTH), row(2 * D_MODEL), row(D_MODEL),
                  pl.BlockSpec((1, 6, D_MODEL), mod_map)] + [_const_spec(w.shape) for w in ws],
        out_specs=[row(D_MODEL), row(D_MODEL)],
        out_shape=[jax.ShapeDtypeStruct((n_tok, D_MODEL), F32),
                   jax.ShapeDtypeStruct((n_tok, D_MODEL), BF16)],
        compiler_params=_params(("arbitrary",)),
        name=name,
    )(om, oh, gate, x, mod, *ws)


def _gelu_tanh(x):
    return 0.5 * x * (1.0 + jnp.tanh(0.7978845608028654 * (x + 0.044715 * (x * x * x))))


def _ffn_kernel(h2_ref, hp_ref, hn_ref, x1_ref, mod_ref, wi_ref, cw_ref, cb_ref, wo_ref,
                gfin_ref, y_ref, ext_ref, gsc_ref, u_ref, *, tm, seq_len):
    halo = BF16_SUBLANES
    t = pl.program_id(0)
    ext_ref[0:halo, :] = hp_ref[...]
    ext_ref[halo:halo + tm, :] = h2_ref[...]
    ext_ref[halo + tm:2 * halo + tm, :] = hn_ref[...]
    pos = (lax.broadcasted_iota(jnp.int32, (tm, 1), 0) + t * tm) & (seq_len - 1)
    has_prev = pos != 0
    has_next = pos != seq_len - 1
    for j in range(D_FF // FF_CHUNK):
        cs = slice(j * FF_CHUNK, (j + 1) * FF_CHUNK)
        vs = slice(D_FF + j * FF_CHUNK, D_FF + (j + 1) * FF_CHUNK)
        gsc_ref[...] = jnp.dot(ext_ref[...], wi_ref[:, cs], preferred_element_type=F32)
        g_prev = jnp.where(has_prev, gsc_ref[halo - 1:halo - 1 + tm, :], 0.0)
        g_cur = gsc_ref[halo:halo + tm, :]
        g_next = jnp.where(has_next, gsc_ref[halo + 1:halo + 1 + tm, :], 0.0)
        gconv = cb_ref[:, cs] + g_prev * cw_ref[0:1, cs] + g_cur * cw_ref[1:2, cs] + g_next * cw_ref[2:3, cs]
        val = jnp.dot(h2_ref[...], wi_ref[:, vs], preferred_element_type=F32)
        u_ref[:, cs] = (_gelu_tanh(gconv) * val).astype(BF16)
    f = jnp.dot(u_ref[...], wo_ref[...], preferred_element_type=F32)
    gt2 = mod_ref[0, 5:6, :]
    y_ref[...] = _rms(x1_ref[...] + gt2 * f, gfin_ref[...])


def _ffn_call(h2, x1, mod, pw, *, seq_len, mod_row0, per_seq_mod, tm, name):
    n_tok = x1.shape[0]
    tps = max(seq_len // tm, 1)
    halo = BF16_SUBLANES
    n_hblk = n_tok // halo
    hpt = tm // halo
    if per_seq_mod:
        mod_map = lambda t: (mod_row0 + t // tps, 0, 0)
    else:
        mod_map = lambda t: (mod_row0, 0, 0)
    row = lambda w: pl.BlockSpec((tm, w), lambda t: (t, 0))
    ws = [pw[k] for k in ("w_ffn_in", "conv_w", "conv_b", "w_ffn_out", "g_final")]
    return pl.pallas_call(
        functools.partial(_ffn_kernel, tm=tm, seq_len=seq_len),
        grid=(n_tok // tm,),
        in_specs=[row(D_MODEL),
                  pl.BlockSpec((halo, D_MODEL), lambda t: (jnp.maximum(t * hpt - 1, 0), 0)),
                  pl.BlockSpec((halo, D_MODEL), lambda t: (jnp.minimum((t + 1) * hpt, n_hblk - 1), 0)),
                  row(D_MODEL),
                  pl.BlockSpec((1, 6, D_MODEL), mod_map)] + [_const_spec(w.shape) for w in ws],
        out_specs=row(D_MODEL),
        out_shape=jax.ShapeDtypeStruct((n_tok, D_MODEL), F32),
        scratch_shapes=[pltpu.VMEM((tm + 2 * halo, D_MODEL), BF16),
                        pltpu.VMEM((tm + 2 * halo, FF_CHUNK), F32),
                        pltpu.VMEM((tm, D_FF), BF16)],
        compiler_params=_params(("arbitrary",)),
        name=name,
    )(h2, h2, h2, x1, mod, *ws)


def _rope_tables(seq_len):
    n_freq = QK_ROPE // 4
    inv = jnp.power(ROPE_BASE, -jnp.arange(n_freq, dtype=F32) / n_freq)
    z8 = jnp.zeros((n_freq,), F32)
    z64 = jnp.zeros((QK_NOPE,), F32)
    inv_r = jnp.concatenate([z64] + [inv, z8] * 4)
    inv_c = jnp.concatenate([z64] + [z8, inv] * 4)
    pos = jnp.arange(seq_len, dtype=jnp.int32)
    r_pos = (pos // GRID_W).astype(F32)[:, None]
    c_pos = (pos % GRID_W).astype(F32)[:, None]
    ang = r_pos * inv_r + c_pos * inv_c
    cos, sin = jnp.cos(ang), jnp.sin(ang)
    s = ATTN_SCALE
    half = QK_ROPE // 2
    o16, z16 = jnp.ones((half,), F32), jnp.zeros((half,), F32)
    c_cs = jnp.concatenate([jnp.full((QK_NOPE + QK_ROPE,), s, F32), o16, o16])
    c_sa = jnp.concatenate([z64, -s * o16, z16, -o16, z16])
    c_sb = jnp.concatenate([z64, z16, s * o16, z16, o16])
    return jnp.stack([cos * c_cs, sin * c_sa, sin * c_sb], axis=0)


def _prep_weights(l, g_mix, w_in, g_q, w_uq, g_kv, w_ukv, w_mla_o, g_hg, w_hg_o, w_out, g_ffn,
                  w_ffn_in, conv_w, conv_b, w_ffn_out, g_final):
    d = D_MODEL
    w_in_t = jnp.swapaxes(w_in[l], 0, 1).astype(BF16)
    hd = QK_NOPE + QK_ROPE
    w_uq_p = jnp.concatenate([w_uq[l].reshape(Q_LORA, MLA_HEADS, hd),
                              jnp.zeros((Q_LORA, MLA_HEADS, LANES - hd), F32)], axis=-1)
    wkv = w_ukv[l].reshape(KV_LORA, MLA_HEADS, QK_NOPE + V_DIM)
    zpad = jnp.zeros((KV_LORA, MLA_HEADS, LANES - QK_NOPE), F32)
    w_k_top = jnp.concatenate([wkv[..., :QK_NOPE], zpad], axis=-1).reshape(KV_LORA, MLA_HEADS * LANES)
    place = jnp.zeros((LANES, LANES), F32).at[
        KR_LANE0 + jnp.arange(QK_ROPE), QK_NOPE + jnp.arange(QK_ROPE)].set(1.0)
    w_k_bot = jnp.tile(place, (1, MLA_HEADS))
    w_k = jnp.concatenate([w_k_top, w_k_bot], axis=0)
    w_v = jnp.concatenate([wkv[..., QK_NOPE:], jnp.zeros((KV_LORA, MLA_HEADS, LANES - V_DIM), F32)],
                          axis=-1).reshape(KV_LORA, MLA_HEADS * LANES)
    return {
        "g_mix": g_mix[l].reshape(1, d), "w_in": w_in_t,
        "g_q": g_q[l].reshape(1, Q_LORA), "w_uq": w_uq_p.reshape(Q_LORA, MLA_HEADS * LANES).astype(BF16),
        "g_kv": g_kv[l].reshape(1, KV_LORA), "w_k": w_k.astype(BF16), "w_v": w_v.astype(BF16),
        "w_mla_o": w_mla_o[l].astype(BF16), "w_hg_o": w_hg_o[l].astype(BF16),
        "w_out": w_out[l].astype(BF16), "g_ffn": g_ffn[l].reshape(1, d),
        "w_ffn_in": w_ffn_in[l].astype(BF16),
        "conv_w": conv_w[l], "conv_b": conv_b[l].reshape(1, D_FF),
        "w_ffn_out": w_ffn_out[l].astype(BF16), "g_final": g_final.reshape(1, d),
        "g_hg": g_hg[l].reshape(1, HG_WIDTH),
    }


def _trunk(x, mod, tabs, cache, pw, lb_param, *, layer, batch, seq_len, mod_row0, per_seq_mod,
           emit_cache, tm, tq, tag):
    n_tok = batch * seq_len
    hw = MLA_HEADS * LANES
    outs = _proj_call(x, mod, tabs, pw, seq_len=seq_len, mod_row0=mod_row0, per_seq_mod=per_seq_mod,
                      emit_cache=emit_cache, tm=tm)
    q, k, v, hgin, gate = outs[:5]
    hgin = hgin.reshape(batch, seq_len, 5 * HG_WIDTH)
    kvs = [(k.reshape(batch, seq_len, hw), v.reshape(batch, seq_len, hw))]
    s0 = None
    if cache is not None:
        ckv_c, kr_c, s0f, s0b = cache
        past = ckv_c.shape[1]
        kc, vc = _cachekv_call(ckv_c.reshape(batch * past, KV_LORA), kr_c.reshape(batch * past, QK_ROPE),
                               pw, tm=min(512, batch * past))
        kvs.append((kc.reshape(batch, past, hw), vc.reshape(batch, past, hw)))
        s0 = (s0f, s0b)
    full_seq = seq_len == tq and s0 is None
    o_mla, *hg_out = _attn_hg_call(q.reshape(batch, seq_len, hw), kvs, hgin, lb_param, pw["g_hg"], tq=tq,
                                   tk=min(512, seq_len),
                                   heads=MLA_HEADS if seq_len == tq else ATTN_HEADS_PER_STEP,
                                   layer=layer, full_seq=full_seq, name="attn_hg_" + tag)
    if full_seq:
        o_hg, s_f, s_b = hg_out
    else:
        o_hg, s_f, s_b = _hg_scan_call(*hg_out, hgin, pw["g_hg"], s0, name="hg_scan_" + tag)
    common = dict(seq_len=seq_len, mod_row0=mod_row0, per_seq_mod=per_seq_mod, tm=tm)
    x1, h2 = _mix_call(o_mla.reshape(n_tok, MLA_HEADS * V_DIM), o_hg.reshape(n_tok, HG_WIDTH), gate, x,
                       mod, pw, name="mix_" + tag, **common)
    y = _ffn_call(h2, x1, mod, pw, name="ffn_" + tag, **common)
    extra = outs[5:] if emit_cache else None
    return y, extra, s_f, s_b


def kernel(x_prompt, x_sample, cache_ckv, cache_krope, state_hgrn_fwd, state_hgrn_bwd, c, c_ctx, w_mod,
           b_mod, g_mix, w_in, g_q, w_uq, g_kv, w_ukv, w_mla_o, lb_param, g_hg, w_hg_o, w_out, g_ffn,
           w_ffn_in, conv_w, conv_b, w_ffn_out, g_final):
    batch, seq, d = x_prompt.shape
    dec_batch, dec_seq, _ = x_sample.shape
    depth = w_in.shape[0]
    assert depth == 1, "single trunk layer"
    l = 0
    cvec = jnp.concatenate([c_ctx[None, :], c, jnp.zeros((8 - 1 - dec_batch, d), F32)], axis=0)
    mod = _mod_call(cvec, w_mod[l], b_mod[l]).reshape(8, 6, d)
    pw = _prep_weights(l, g_mix, w_in, g_q, w_uq, g_kv, w_ukv, w_mla_o, g_hg, w_hg_o, w_out, g_ffn,
                       w_ffn_in, conv_w, conv_b, w_ffn_out, g_final)
    tabs = _rope_tables(dec_seq)

    y_ctx, cache_out, s_f, s_b = _trunk(
        x_prompt.reshape(batch * seq, d), mod, None, None, pw, lb_param, layer=l, batch=batch,
        seq_len=seq, mod_row0=0, per_seq_mod=False, emit_cache=True, tm=TOKEN_TILE, tq=min(Q_TILE, seq),
        tag="ctx")
    cache_l = (cache_ckv[:, l], cache_krope[:, l], state_hgrn_fwd[:, l], state_hgrn_bwd[:, l])
    y_lat, _, _, _ = _trunk(
        x_sample.reshape(dec_batch * dec_seq, d), mod, tabs, cache_l, pw, lb_param, layer=l,
        batch=dec_batch, seq_len=dec_seq, mod_row0=1, per_seq_mod=True, emit_cache=False, tm=TOKEN_TILE,
        tq=min(Q_TILE, dec_seq), tag="lat")

    ckv_new, kr_new = cache_out
    return (y_ctx.reshape(batch, seq, d), y_lat.reshape(dec_batch, dec_seq, d),
            ckv_new.reshape(batch, 1, seq, KV_LORA), kr_new.reshape(batch, 1, seq, QK_ROPE),
            s_f.reshape(batch, 1, HG_HEADS, HG_DK, HG_DV), s_b.reshape(batch, 1, HG_HEADS, HG_DK, HG_DV))
```

```python
import functools

import jax
import jax.numpy as jnp
from jax import lax
from jax.experimental import pallas as pl
from jax.experimental.pallas import tpu as pltpu

F32 = jnp.float32
BF16 = jnp.bfloat16

D_MODEL = 1024
GRID_W = 64
MLA_HEADS = 8
QK_NOPE = 64
QK_ROPE = 32
V_DIM = 64
Q_LORA = 384
KV_LORA = 256
ROPE_BASE = 10000.0
HG_HEADS = 4
HG_DK = 128
HG_DV = 128
HG_WIDTH = HG_HEADS * HG_DV
D_FF = 2816
EPS = 1e-6

LANES = 128
SUBLANES = 8
BF16_SUBLANES = 16
HG_CHUNK = 128
HG_LEVELS = 7
HG_UNROLL = 4
FF_CHUNK = 256
TOKEN_TILE = 512
Q_TILE = 512
ATTN_HEADS_PER_STEP = 4
VMEM_LIMIT = 56 * 1024 * 1024

C_PQ = 0
C_PKV = Q_LORA
C_PKR = C_PKV + KV_LORA
C_HQ = C_PKR + QK_ROPE
C_ZF = C_HQ + HG_WIDTH
C_HG = C_ZF + 3 * HG_WIDTH
C_GA = C_HG + HG_WIDTH
C_END = C_GA + 2 * D_MODEL
KR_LANE0 = LANES - QK_ROPE

ATTN_SCALE = (QK_NOPE + QK_ROPE) ** -0.5 * 1.4426950408889634

NT = (((1,), (1,)), ((), ()))
TN = (((0,), (0,)), ((), ()))


def _rms(x, g):
    return x * lax.rsqrt(jnp.mean(x * x, axis=-1, keepdims=True) + EPS) * g


def _const_spec(shape):
    n = len(shape)
    return pl.BlockSpec(shape, lambda *_: (0,) * n, pipeline_mode=pl.Buffered(1))


def _params(sem):
    return pltpu.CompilerParams(dimension_semantics=sem, vmem_limit_bytes=VMEM_LIMIT)


def _mod_kernel(c_ref, w_ref, b_ref, o_ref):
    c = c_ref[...]
    a = c * jax.nn.sigmoid(c)
    o_ref[...] = jnp.dot(a, w_ref[...], precision=lax.Precision.HIGHEST,
                         preferred_element_type=F32) + b_ref[...]


def _mod_call(cvec, w_mod, b_mod):
    n = w_mod.shape[1]
    tn = 1024
    return pl.pallas_call(
        _mod_kernel,
        grid=(n // tn,),
        in_specs=[pl.BlockSpec((8, D_MODEL), lambda j: (0, 0)),
                  pl.BlockSpec((D_MODEL, tn), lambda j: (0, j)),
                  pl.BlockSpec((1, tn), lambda j: (0, j))],
        out_specs=pl.BlockSpec((8, tn), lambda j: (0, j)),
        out_shape=jax.ShapeDtypeStruct((8, n), F32),
        compiler_params=_params(("arbitrary",)),
        name="mod",
    )(cvec, w_mod, b_mod.reshape(1, n))


def _proj_kernel(*refs, rope, emit_cache):
    it = iter(refs)
    x_ref = next(it)
    mod_ref = next(it)
    tab_ref = next(it) if rope else None
    gmix_ref, win_ref, gq_ref, wuq_ref, gkv_ref, wk_ref, wv_ref = (next(it) for _ in range(7))
    q_ref, k_ref, v_ref, hg_ref, gate_ref = (next(it) for _ in range(5))
    if emit_cache:
        ckv_ref, kr_ref = next(it), next(it)

    sh1 = mod_ref[0, 0:1, :]
    sc1 = mod_ref[0, 1:2, :]
    hb = (_rms(x_ref[...], gmix_ref[...]) * (1.0 + sc1) + sh1).astype(BF16)

    def proj(c0, c1):
        return lax.dot_general(hb, win_ref[c0:c1, :], NT, preferred_element_type=F32)

    if rope:
        cs1, sa1, sb1 = tab_ref[0], tab_ref[1], tab_ref[2]

        def rot(x, reps):
            w = x.shape[1]
            cs = jnp.concatenate([cs1] * reps, axis=1) if reps > 1 else cs1
            sa = jnp.concatenate([sa1] * reps, axis=1) if reps > 1 else sa1
            sb = jnp.concatenate([sb1] * reps, axis=1) if reps > 1 else sb1
            half = QK_ROPE // 2
            return x * cs + pltpu.roll(x, w - half, 1) * sa + pltpu.roll(x, half, 1) * sb

    qn = _rms(proj(C_PQ, C_PKV), gq_ref[...]).astype(BF16)
    q = jnp.dot(qn, wuq_ref[...], preferred_element_type=F32)
    q = rot(q, MLA_HEADS) if rope else q * ATTN_SCALE
    q_ref[...] = q.astype(BF16)

    ckv = _rms(proj(C_PKV, C_PKR), gkv_ref[...])
    pkr = proj(C_HQ - LANES, C_HQ)
    if emit_cache:
        ckv_ref[...] = ckv
        kr_ref[...] = pkr[:, KR_LANE0:]
    ckv_b = ckv.astype(BF16)
    kr = rot(pkr, 1) if rope else pkr
    kin = jnp.concatenate([ckv_b, kr.astype(BF16)], axis=1)
    k_ref[...] = jnp.dot(kin, wk_ref[...], preferred_element_type=F32).astype(BF16)
    lane = lax.broadcasted_iota(jnp.int32, (1, MLA_HEADS * LANES), 1)
    ones = jnp.where((lane & (LANES - 1)) >= V_DIM, 1.0, 0.0)
    v_ref[...] = (jnp.dot(ckv_b, wv_ref[...], preferred_element_type=F32) + ones).astype(BF16)

    hq = proj(C_HQ, C_ZF)
    hg_ref[:, 0:HG_WIDTH] = hq * jax.nn.sigmoid(hq)
    hg_ref[:, HG_WIDTH:4 * HG_WIDTH] = proj(C_ZF, C_HG)
    og = proj(C_HG, C_GA)
    hg_ref[:, 4 * HG_WIDTH:5 * HG_WIDTH] = og * jax.nn.sigmoid(og)

    gate_ref[...] = jax.nn.sigmoid(proj(C_GA, C_END)).astype(BF16)


def _proj_call(x, mod, tabs, pw, *, seq_len, mod_row0, per_seq_mod, emit_cache, tm):
    n_tok = x.shape[0]
    tps = max(seq_len // tm, 1)
    rope = tabs is not None
    if per_seq_mod:
        mod_map = lambda t: (mod_row0 + t // tps, 0, 0)
    else:
        mod_map = lambda t: (mod_row0, 0, 0)
    row = lambda w: pl.BlockSpec((tm, w), lambda t: (t, 0))
    in_specs = [row(D_MODEL), pl.BlockSpec((1, 6, D_MODEL), mod_map)]
    args = [x, mod]
    if rope:
        in_specs.append(pl.BlockSpec((3, tm, LANES), lambda t: (0, t % tps, 0)))
        args.append(tabs)
    for name in ("g_mix", "w_in", "g_q", "w_uq", "g_kv", "w_k", "w_v"):
        in_specs.append(_const_spec(pw[name].shape))
        args.append(pw[name])
    hw = MLA_HEADS * LANES
    out_shape = [jax.ShapeDtypeStruct((n_tok, hw), BF16),
                 jax.ShapeDtypeStruct((n_tok, hw), BF16),
                 jax.ShapeDtypeStruct((n_tok, hw), BF16),
                 jax.ShapeDtypeStruct((n_tok, 5 * HG_WIDTH), F32),
                 jax.ShapeDtypeStruct((n_tok, 2 * D_MODEL), BF16)]
    out_specs = [row(hw), row(hw), row(hw), row(5 * HG_WIDTH), row(2 * D_MODEL)]
    if emit_cache:
        out_shape += [jax.ShapeDtypeStruct((n_tok, KV_LORA), F32),
                      jax.ShapeDtypeStruct((n_tok, QK_ROPE), F32)]
        out_specs += [row(KV_LORA), row(QK_ROPE)]
    return pl.pallas_call(
        functools.partial(_proj_kernel, rope=rope, emit_cache=emit_cache),
        grid=(n_tok // tm,),
        in_specs=in_specs,
        out_specs=out_specs,
        out_shape=out_shape,
        compiler_params=_params(("arbitrary",)),
        name="proj_lat" if rope else "proj_ctx",
    )(*args)


def _cachekv_kernel(ckv_ref, kr_ref, wk_ref, wv_ref, k_ref, v_ref):
    ckv_b = ckv_ref[...].astype(BF16)
    tm = ckv_b.shape[0]
    kr = jnp.concatenate([jnp.zeros((tm, KR_LANE0), BF16), kr_ref[...].astype(BF16)], axis=1)
    kin = jnp.concatenate([ckv_b, kr], axis=1)
    k_ref[...] = jnp.dot(kin, wk_ref[...], preferred_element_type=F32).astype(BF16)
    lane = lax.broadcasted_iota(jnp.int32, (1, MLA_HEADS * LANES), 1)
    ones = jnp.where((lane & (LANES - 1)) >= V_DIM, 1.0, 0.0)
    v_ref[...] = (jnp.dot(ckv_b, wv_ref[...], preferred_element_type=F32) + ones).astype(BF16)


def _cachekv_call(ckv, kr, pw, tm):
    n_tok = ckv.shape[0]
    hw = MLA_HEADS * LANES
    row = lambda w: pl.BlockSpec((tm, w), lambda t: (t, 0))
    return pl.pallas_call(
        _cachekv_kernel,
        grid=(n_tok // tm,),
        in_specs=[row(KV_LORA), row(QK_ROPE), _const_spec(pw["w_k"].shape), _const_spec(pw["w_v"].shape)],
        out_specs=[row(hw), row(hw)],
        out_shape=[jax.ShapeDtypeStruct((n_tok, hw), BF16)] * 2,
        compiler_params=_params(("arbitrary",)),
        name="cache_kv",
    )(ckv, kr, pw["w_k"], pw["w_v"])


def _hg_consts():
    c_len, sub = HG_CHUNK, SUBLANES
    row8 = lax.broadcasted_iota(jnp.int32, (1, sub, LANES), 1)
    upper8 = [(row8 & (1 << j)) != 0 for j in range(3)]
    ti = lax.broadcasted_iota(jnp.int32, (c_len, c_len), 0)
    si = lax.broadcasted_iota(jnp.int32, (c_len, c_len), 1)
    txs = ti ^ si
    lvl = jnp.full((c_len, c_len), -1, jnp.int32)
    for j in range(HG_LEVELS):
        lvl = lvl + jnp.where(txs >= (1 << j), 1, 0)
    return upper8, lvl


def _hg_lower_bounds(lbp, layer):
    e = jnp.exp(lbp - jnp.max(lbp, axis=0, keepdims=True))
    return jnp.sum(e[:layer + 1], axis=0) / jnp.sum(e, axis=0)


def _hg_gates(z, lbv):
    f = lbv + (1.0 - lbv) * jax.nn.sigmoid(z)
    return jnp.log2(f), 1.0 - f


def _hg_rep(slab, n_rows):
    return slab if n_rows == SUBLANES else jnp.concatenate([slab] * (n_rows // SUBLANES), axis=0)


def _hg_halves(j):
    h = 1 << j
    return [(slice(b, b + h), slice(b + h, b + 2 * h)) for b in range(0, HG_CHUNK, 2 * h)]


def _hg_sel8(upper8, j, a, b):
    shape3 = (HG_CHUNK // SUBLANES, SUBLANES, LANES)
    return jnp.where(upper8[j], a.reshape(shape3), b.reshape(shape3)).reshape(HG_CHUNK, LANES)


def _hg_upsweep(x, inclusive, upper8):
    c_len, sub = HG_CHUNK, SUBLANES
    x3 = x.reshape(c_len // sub, sub, LANES)
    pre3 = [x3 if inclusive else jnp.zeros_like(x3)]
    tot3 = [x3]
    for j in range(3):
        h = 1 << j
        t = tot3[-1]
        if 2 * h == sub:
            sib = pltpu.roll(t, h, 1)
        else:
            sib = jnp.where(upper8[j], pltpu.roll(t, h, 1), pltpu.roll(t, sub - h, 1))
        tot3.append(t + sib)
        pre3.append(pre3[-1] + jnp.where(upper8[j], sib, 0.0))
    pre = [p.reshape(c_len, LANES) for p in pre3]
    tot = [t.reshape(c_len, LANES) for t in tot3]
    slabs = {3: [tot[3][r:r + sub] for r in range(0, c_len, sub)]}
    for j in range(3, HG_LEVELS):
        h = 1 << j
        p, s = pre[-1], slabs[j]
        parts = []
        for i, (lo, up) in enumerate(_hg_halves(j)):
            parts += [p[lo], p[up] + _hg_rep(s[2 * i], h)]
        pre.append(jnp.concatenate(parts, axis=0))
        slabs[j + 1] = [s[2 * i] + s[2 * i + 1] for i in range(len(s) // 2)]
    return pre, tot, slabs


def _hg_chunk_terms(q, zf, zb, v, lb_f, lb_b, upper8, lvl):
    c_len, top = HG_CHUNK, HG_LEVELS
    vb = v.astype(BF16)
    lgf, kf = _hg_gates(zf, lb_f)
    lgb, kb = _hg_gates(zb, lb_b)
    pf, tf, sf = _hg_upsweep(lgf, True, upper8)
    xb, tb, sb = _hg_upsweep(lgb, False, upper8)
    a = lax.dot_general(q.astype(BF16), (kf + kb).astype(BF16), NT, preferred_element_type=F32)
    a = jnp.where(lvl == -1, a, 0.0)
    for j in range(HG_LEVELS):
        h = 1 << j
        if j < 3:
            eq = _hg_sel8(upper8, j, pf[j], tb[j] - xb[j])
            ek = _hg_sel8(upper8, j, xb[j], tf[j] - pf[j])
            ksel = _hg_sel8(upper8, j, kb, kf)
        else:
            eqs, eks, kss = [], [], []
            for i, (lo, up) in enumerate(_hg_halves(j)):
                eqs += [_hg_rep(sb[j][2 * i], h) - xb[j][lo], pf[j][up]]
                eks += [_hg_rep(sf[j][2 * i], h) - pf[j][lo], xb[j][up]]
                kss += [kf[lo], kb[up]]
            eq, ek, ksel = (jnp.concatenate(p, axis=0) for p in (eqs, eks, kss))
        qs = (q * jnp.exp2(eq)).astype(BF16)
        ks = (ksel * jnp.exp2(ek)).astype(BF16)
        aj = lax.dot_general(qs, ks, NT, preferred_element_type=F32)
        a = jnp.where(lvl == j, aj, a)
    o_intra = jnp.dot(a.astype(BF16), vb, preferred_element_type=F32)
    tot_f, tot_b = sf[top][0], sb[top][0]
    qhf = (q * jnp.exp2(pf[top])).astype(BF16)
    qhb = (q * jnp.exp2(_hg_rep(tot_b, c_len) - xb[top])).astype(BF16)
    khf = (kf * jnp.exp2(_hg_rep(tot_f, c_len) - pf[top])).astype(BF16)
    khb = (kb * jnp.exp2(xb[top])).astype(BF16)
    utf = lax.dot_general(vb, khf, TN, preferred_element_type=F32)
    utb = lax.dot_general(vb, khb, TN, preferred_element_type=F32)
    return o_intra, qhf, qhb, utf, utb, jnp.exp2(tot_f[0:1, :]), jnp.exp2(tot_b[0:1, :])


def _attn_hg_kernel(*refs, nseg, tk, layer, full_seq):
    it = iter(refs)
    q_ref = next(it)
    kv = [next(it) for _ in range(2 * nseg)]
    hq_ref, zf_ref, zb_ref, hv_ref, lbp_ref = (next(it) for _ in range(5))
    if full_seq:
        g_ref, ghg_ref = next(it), next(it)
        o_ref, ohg_ref, sf_ref, sb_ref = (next(it) for _ in range(4))
    else:
        o_ref, oi_ref, qhf_ref, qhb_ref, utf_ref, utb_ref, df_ref, db_ref = (next(it) for _ in range(8))
    s_ref = next(it)
    tq = q_ref.shape[1]
    heads = q_ref.shape[2] // LANES
    hg_heads = hq_ref.shape[2] // LANES
    lane = lax.broadcasted_iota(jnp.int32, (1, LANES), 1)
    chunks = [(kv[2 * seg], kv[2 * seg + 1], c * tk)
              for seg in range(nseg) for c in range(kv[2 * seg].shape[1] // tk)]

    upper8, lvl = _hg_consts()
    lb = _hg_lower_bounds(lbp_ref[...], layer)
    local = [(ci, h) for ci in range(tq // HG_CHUNK) for h in range(hg_heads)]
    per_head = len(local) // heads
    kept = {}

    def hg_local(ci, h):
        rs = slice(ci * HG_CHUNK, (ci + 1) * HG_CHUNK)
        cs = slice(h * LANES, (h + 1) * LANES)
        terms = _hg_chunk_terms(
            hq_ref[0, rs, cs], zf_ref[0, rs, cs], zb_ref[0, rs, cs], hv_ref[0, rs, cs],
            lb[0:1, cs], lb[1:2, cs], upper8, lvl)
        if full_seq:
            kept[(ci, h)] = terms
            return
        oi, qhf, qhb, utf, utb, dec_f, dec_b = terms
        oi_ref[0, rs, cs] = oi
        qhf_ref[0, rs, cs] = qhf
        qhb_ref[0, rs, cs] = qhb
        utf_ref[0, ci, h] = utf
        utb_ref[0, ci, h] = utb
        df_ref[0, ci, :, cs] = dec_f
        db_ref[0, ci, :, cs] = dec_b

    outs = []
    for h in range(heads):
        hs = slice(h * LANES, (h + 1) * LANES)
        qh = q_ref[0, :, hs]
        mx = jnp.full((tq, LANES), -jnp.inf, F32)
        for i, (k_ref, _, off) in enumerate(chunks):
            s = lax.dot_general(qh, k_ref[0, off:off + tk, hs], NT, preferred_element_type=F32)
            s_ref[:, i * tk:(i + 1) * tk] = s
            for w in range(tk // LANES):
                mx = jnp.maximum(mx, s[:, w * LANES:(w + 1) * LANES])
        m = jnp.broadcast_to(jnp.max(mx, axis=1, keepdims=True), (tq, LANES))
        mb = jnp.concatenate([m] * (tk // LANES), axis=1)
        acc = jnp.zeros((tq, LANES), F32)
        for i, (_, v_ref, off) in enumerate(chunks):
            p = jnp.exp2(s_ref[:, i * tk:(i + 1) * tk] - mb).astype(BF16)
            acc = acc + jnp.dot(p, v_ref[0, off:off + tk, hs], preferred_element_type=F32)
        outs.append(acc / pltpu.roll(acc, V_DIM, 1))
        for ci, hh in local[h * per_head:(h + 1) * per_head]:
            hg_local(ci, hh)
    for pair in range(heads // 2):
        a = outs[2 * pair]
        b = pltpu.roll(outs[2 * pair + 1], V_DIM, 1)
        o_ref[0, :, pair * LANES:(pair + 1) * LANES] = jnp.where(lane < V_DIM, a, b).astype(BF16)

    if full_seq:
        n = tq // HG_CHUNK
        for h in range(hg_heads):
            cs = slice(h * LANES, (h + 1) * LANES)
            enter_b, st = [None] * n, None
            for c in reversed(range(n)):
                enter_b[c] = st
                _, _, _, _, utb, _, dec_b = kept[(c, h)]
                st = utb if st is None else dec_b * st + utb
            sb_ref[0, h] = st.T
            st = None
            for c in range(n):
                rs = slice(c * HG_CHUNK, (c + 1) * HG_CHUNK)
                oi, qhf, qhb, utf, _, dec_f, _ = kept[(c, h)]
                if st is not None:
                    oi = oi + lax.dot_general(qhf, st.astype(BF16), NT, preferred_element_type=F32)
                if enter_b[c] is not None:
                    oi = oi + lax.dot_general(qhb, enter_b[c].astype(BF16), NT, preferred_element_type=F32)
                ohg_ref[0, rs, cs] = (_rms(oi, ghg_ref[:, cs]) * g_ref[0, rs, cs]).astype(BF16)
                st = utf if st is None else dec_f * st + utf
            sf_ref[0, h] = st.T


def _attn_hg_call(q, kvs, hgin, lb_param, g_hg, *, tq, tk, heads, layer, full_seq, name):
    b, t, _ = q.shape
    n_chunks = t // HG_CHUNK
    cpt = tq // HG_CHUNK
    groups = MLA_HEADS // heads
    hgh = HG_HEADS // groups
    assert heads % 2 == 0 and hgh >= 1 and (cpt * hgh) % heads == 0
    aw, gw = heads * LANES, hgh * LANES
    in_specs = [pl.BlockSpec((1, tq, aw), lambda i, p, j: (i, j, p))]
    args = [q]
    n_keys = 0
    for k, v in kvs:
        n_keys += k.shape[1]
        for a in (k, v):
            mode = dict(pipeline_mode=pl.Buffered(1)) if groups == 1 and t // tq > 1 else {}
            in_specs.append(pl.BlockSpec((1, a.shape[1], aw), lambda i, p, j: (i, 0, p), **mode))
            args.append(a)
    for g in range(4):
        in_specs.append(pl.BlockSpec((1, tq, gw), lambda i, p, j, g=g: (i, j, g * groups + p)))
        args.append(hgin)
    in_specs.append(pl.BlockSpec((lb_param.shape[0], 2, gw), lambda i, p, j: (0, 0, p)))
    args.append(lb_param)
    rows = lambda w: pl.BlockSpec((1, tq, w), lambda i, p, j: (i, j, p))
    o_shape = jax.ShapeDtypeStruct((b, t, MLA_HEADS * V_DIM), BF16)
    if full_seq:
        assert t == tq
        in_specs += [pl.BlockSpec((1, tq, gw), lambda i, p, j: (i, j, 4 * groups + p)),
                     pl.BlockSpec((1, gw), lambda i, p, j: (0, p))]
        args += [hgin, g_hg]
        st_spec = pl.BlockSpec((1, hgh, HG_DK, HG_DV), lambda i, p, j: (i, p, 0, 0))
        st_shape = jax.ShapeDtypeStruct((b, HG_HEADS, HG_DK, HG_DV), F32)
        out_specs = [rows(heads * V_DIM), rows(gw), st_spec, st_spec]
        out_shape = [o_shape, jax.ShapeDtypeStruct((b, t, HG_WIDTH), BF16), st_shape, st_shape]
    else:
        ut_spec = pl.BlockSpec((1, cpt, hgh, HG_DV, HG_DK), lambda i, p, j: (i, j, p, 0, 0))
        dec_spec = pl.BlockSpec((1, cpt, 1, gw), lambda i, p, j: (i, j, 0, p))
        ut_shape = jax.ShapeDtypeStruct((b, n_chunks, HG_HEADS, HG_DV, HG_DK), F32)
        dec_shape = jax.ShapeDtypeStruct((b, n_chunks, 1, HG_WIDTH), F32)
        out_specs = [rows(heads * V_DIM), rows(gw), rows(gw), rows(gw),
                     ut_spec, ut_spec, dec_spec, dec_spec]
        out_shape = [o_shape,
                     jax.ShapeDtypeStruct((b, t, HG_WIDTH), F32),
                     jax.ShapeDtypeStruct((b, t, HG_WIDTH), BF16),
                     jax.ShapeDtypeStruct((b, t, HG_WIDTH), BF16),
                     ut_shape, ut_shape, dec_shape, dec_shape]
    return pl.pallas_call(
        functools.partial(_attn_hg_kernel, nseg=len(kvs), tk=tk, layer=layer, full_seq=full_seq),
        grid=(b, groups, t // tq),
        in_specs=in_specs,
        out_specs=out_specs,
        out_shape=out_shape,
        scratch_shapes=[pltpu.VMEM((tq, n_keys), F32)],
        compiler_params=_params(("arbitrary", "arbitrary", "arbitrary")),
        name=name,
    )(*args)


def _hg_scan_kernel(*refs, has_s0):
    it = iter(refs)
    oi_ref, qhf_ref, qhb_ref, utf_ref, utb_ref, df_ref, db_ref, g_ref, ghg_ref = (next(it) for _ in range(9))
    if has_s0:
        s0f_ref, s0b_ref = next(it), next(it)
    o_ref, sf_ref, sb_ref, sbst_ref = (next(it) for _ in range(4))
    c_len = HG_CHUNK
    n_chunks = utf_ref.shape[1]
    unroll = min(HG_UNROLL, n_chunks)

    def rows_of(c):
        return pl.ds(pl.multiple_of(c * c_len, c_len), c_len)

    def bwd_body(i, st):
        c = n_chunks - 1 - i
        sbst_ref[c] = st.astype(BF16)
        return db_ref[0, c] * st + utb_ref[0, c, 0]

    st_b0 = s0b_ref[0, 0].T if has_s0 else jnp.zeros((HG_DV, HG_DK), F32)
    st_b = lax.fori_loop(0, n_chunks, bwd_body, st_b0, unroll=unroll)
    sb_ref[0, 0] = st_b.T

    ghg = ghg_ref[...]

    def fwd_body(c, st):
        qcat = jnp.concatenate([qhf_ref[0, rows_of(c), :], qhb_ref[0, rows_of(c), :]], axis=1)
        scat = jnp.concatenate([st.astype(BF16), sbst_ref[c]], axis=1)
        o = oi_ref[0, rows_of(c), :] + lax.dot_general(qcat, scat, NT, preferred_element_type=F32)
        on = _rms(o, ghg) * g_ref[0, rows_of(c), :]
        o_ref[0, rows_of(c), :] = on.astype(BF16)
        return df_ref[0, c] * st + utf_ref[0, c, 0]

    st_f0 = s0f_ref[0, 0].T if has_s0 else jnp.zeros((HG_DV, HG_DK), F32)
    st_f = lax.fori_loop(0, n_chunks, fwd_body, st_f0, unroll=unroll)
    sf_ref[0, 0] = st_f.T


def _hg_scan_call(oi, qhf, qhb, utf, utb, dec_f, dec_b, hgin, g_hg, s0, *, name):
    b, t, _ = oi.shape
    n_chunks = t // HG_CHUNK
    has_s0 = s0 is not None
    col = pl.BlockSpec((1, t, LANES), lambda i, h: (i, 0, h))
    ut_spec = pl.BlockSpec((1, n_chunks, 1, HG_DV, HG_DK), lambda i, h: (i, 0, h, 0, 0))
    dec_spec = pl.BlockSpec((1, n_chunks, 1, LANES), lambda i, h: (i, 0, 0, h))
    gate_col = pl.BlockSpec((1, t, LANES), lambda i, h: (i, 0, 4 * HG_HEADS + h))
    in_specs = [col, col, col, ut_spec, ut_spec, dec_spec, dec_spec, gate_col,
                pl.BlockSpec((1, LANES), lambda i, h: (0, h))]
    args = [oi, qhf, qhb, utf, utb, dec_f, dec_b, hgin, g_hg]
    st_spec = pl.BlockSpec((1, 1, HG_DK, HG_DV), lambda i, h: (i, h, 0, 0))
    if has_s0:
        in_specs += [st_spec, st_spec]
        args += list(s0)
    st_shape = jax.ShapeDtypeStruct((b, HG_HEADS, HG_DK, HG_DV), F32)
    return pl.pallas_call(
        functools.partial(_hg_scan_kernel, has_s0=has_s0),
        grid=(b, HG_HEADS),
        in_specs=in_specs,
        out_specs=[col, st_spec, st_spec],
        out_shape=[jax.ShapeDtypeStruct((b, t, HG_WIDTH), BF16), st_shape, st_shape],
        scratch_shapes=[pltpu.VMEM((n_chunks, HG_DV, HG_DK), BF16)],
        compiler_params=_params(("arbitrary", "arbitrary")),
        name=name,
    )(*args)


def _gelu_tanh(x):
    return 0.5 * x * (1.0 + jnp.tanh(0.7978845608028654 * (x + 0.044715 * (x * x * x))))


def _mixffn_kernel(om_ref, omp_ref, omn_ref, oh_ref, ohp_ref, ohn_ref, gt_ref, gtp_ref, gtn_ref,
                   x_ref, xp_ref, xn_ref, mod_ref, wmo_ref, who_ref, wout_ref, gffn_ref,
                   wi_ref, cw_ref, cb_ref, wo_ref, gfin_ref, y_ref, gsc_ref, u_ref, *, tm, seq_len):
    halo = BF16_SUBLANES
    t = pl.program_id(0)

    def ext(p_ref, m_ref, n_ref):
        return jnp.concatenate([p_ref[...], m_ref[...], n_ref[...]], axis=0)

    a = jnp.dot(ext(omp_ref, om_ref, omn_ref), wmo_ref[...], preferred_element_type=F32)
    b = jnp.dot(ext(ohp_ref, oh_ref, ohn_ref), who_ref[...], preferred_element_type=F32)
    gate = ext(gtp_ref, gt_ref, gtn_ref)
    mix = gate[:, 0:D_MODEL] * a + gate[:, D_MODEL:2 * D_MODEL] * b
    y = jnp.dot(mix.astype(BF16), wout_ref[...], preferred_element_type=F32)
    gt1 = mod_ref[0, 2:3, :]
    sh2 = mod_ref[0, 3:4, :]
    sc2 = mod_ref[0, 4:5, :]
    x1 = ext(xp_ref, x_ref, xn_ref) + gt1 * y
    h2 = (_rms(x1, gffn_ref[...]) * (1.0 + sc2) + sh2).astype(BF16)
    h2_own = h2[halo:halo + tm]

    pos = (lax.broadcasted_iota(jnp.int32, (tm, 1), 0) + t * tm) & (seq_len - 1)
    has_prev = pos != 0
    has_next = pos != seq_len - 1
    for j in range(D_FF // FF_CHUNK):
        cs = slice(j * FF_CHUNK, (j + 1) * FF_CHUNK)
        vs = slice(D_FF + j * FF_CHUNK, D_FF + (j + 1) * FF_CHUNK)
        gsc_ref[...] = jnp.dot(h2, wi_ref[:, cs], preferred_element_type=F32)
        g_prev = jnp.where(has_prev, gsc_ref[halo - 1:halo - 1 + tm, :], 0.0)
        g_cur = gsc_ref[halo:halo + tm, :]
        g_next = jnp.where(has_next, gsc_ref[halo + 1:halo + 1 + tm, :], 0.0)
        gconv = cb_ref[:, cs] + g_prev * cw_ref[0:1, cs] + g_cur * cw_ref[1:2, cs] + g_next * cw_ref[2:3, cs]
        val = jnp.dot(h2_own, wi_ref[:, vs], preferred_element_type=F32)
        u_ref[:, cs] = (_gelu_tanh(gconv) * val).astype(BF16)
    f = jnp.dot(u_ref[...], wo_ref[...], preferred_element_type=F32)
    gt2 = mod_ref[0, 5:6, :]
    y_ref[...] = _rms(x1[halo:halo + tm] + gt2 * f, gfin_ref[...])


def _mixffn_call(om, oh, gate, x, mod, pw, *, seq_len, mod_row0, per_seq_mod, tm, name):
    n_tok = x.shape[0]
    tps = max(seq_len // tm, 1)
    halo = BF16_SUBLANES
    n_hblk = n_tok // halo
    hpt = tm // halo
    if per_seq_mod:
        mod_map = lambda t: (mod_row0 + t // tps, 0, 0)
    else:
        mod_map = lambda t: (mod_row0, 0, 0)

    def with_halo(w):
        return [pl.BlockSpec((tm, w), lambda t: (t, 0)),
                pl.BlockSpec((halo, w), lambda t: (jnp.maximum(t * hpt - 1, 0), 0)),
                pl.BlockSpec((halo, w), lambda t: (jnp.minimum((t + 1) * hpt, n_hblk - 1), 0))]

    ws = [pw[k] for k in ("w_mla_o", "w_hg_o", "w_out", "g_ffn", "w_ffn_in", "conv_w", "conv_b",
                          "w_ffn_out", "g_final")]
    return pl.pallas_call(
        functools.partial(_mixffn_kernel, tm=tm, seq_len=seq_len),
        grid=(n_tok // tm,),
        in_specs=(with_halo(MLA_HEADS * V_DIM) + with_halo(HG_WIDTH) + with_halo(2 * D_MODEL)
                  + with_halo(D_MODEL) + [pl.BlockSpec((1, 6, D_MODEL), mod_map)]
                  + [_const_spec(w.shape) for w in ws]),
        out_specs=pl.BlockSpec((tm, D_MODEL), lambda t: (t, 0)),
        out_shape=jax.ShapeDtypeStruct((n_tok, D_MODEL), F32),
        scratch_shapes=[pltpu.VMEM((tm + 2 * halo, FF_CHUNK), F32),
                        pltpu.VMEM((tm, D_FF), BF16)],
        compiler_params=_params(("arbitrary",)),
        name=name,
    )(om, om, om, oh, oh, oh, gate, gate, gate, x, x, x, mod, *ws)


def _rope_tables(seq_len):
    n_freq = QK_ROPE // 4
    inv = jnp.power(ROPE_BASE, -jnp.arange(n_freq, dtype=F32) / n_freq)
    z8 = jnp.zeros((n_freq,), F32)
    z64 = jnp.zeros((QK_NOPE,), F32)
    inv_r = jnp.concatenate([z64] + [inv, z8] * 4)
    inv_c = jnp.concatenate([z64] + [z8, inv] * 4)
    pos = jnp.arange(seq_len, dtype=jnp.int32)
    r_pos = (pos // GRID_W).astype(F32)[:, None]
    c_pos = (pos % GRID_W).astype(F32)[:, None]
    ang = r_pos * inv_r + c_pos * inv_c
    cos, sin = jnp.cos(ang), jnp.sin(ang)
    s = ATTN_SCALE
    half = QK_ROPE // 2
    o16, z16 = jnp.ones((half,), F32), jnp.zeros((half,), F32)
    c_cs = jnp.concatenate([jnp.full((QK_NOPE + QK_ROPE,), s, F32), o16, o16])
    c_sa = jnp.concatenate([z64, -s * o16, z16, -o16, z16])
    c_sb = jnp.concatenate([z64, z16, s * o16, z16, o16])
    return jnp.stack([cos * c_cs, sin * c_sa, sin * c_sb], axis=0)


def _prep_weights(l, g_mix, w_in, g_q, w_uq, g_kv, w_ukv, w_mla_o, g_hg, w_hg_o, w_out, g_ffn,
                  w_ffn_in, conv_w, conv_b, w_ffn_out, g_final):
    d = D_MODEL
    w_in_t = jnp.swapaxes(w_in[l], 0, 1).astype(BF16)
    hd = QK_NOPE + QK_ROPE
    w_uq_p = jnp.concatenate([w_uq[l].reshape(Q_LORA, MLA_HEADS, hd),
                              jnp.zeros((Q_LORA, MLA_HEADS, LANES - hd), F32)], axis=-1)
    wkv = w_ukv[l].reshape(KV_LORA, MLA_HEADS, QK_NOPE + V_DIM)
    zpad = jnp.zeros((KV_LORA, MLA_HEADS, LANES - QK_NOPE), F32)
    w_k_top = jnp.concatenate([wkv[..., :QK_NOPE], zpad], axis=-1).reshape(KV_LORA, MLA_HEADS * LANES)
    place = jnp.zeros((LANES, LANES), F32).at[
        KR_LANE0 + jnp.arange(QK_ROPE), QK_NOPE + jnp.arange(QK_ROPE)].set(1.0)
    w_k_bot = jnp.tile(place, (1, MLA_HEADS))
    w_k = jnp.concatenate([w_k_top, w_k_bot], axis=0)
    w_v = jnp.concatenate([wkv[..., QK_NOPE:], jnp.zeros((KV_LORA, MLA_HEADS, LANES - V_DIM), F32)],
                          axis=-1).reshape(KV_LORA, MLA_HEADS * LANES)
    return {
        "g_mix": g_mix[l].reshape(1, d), "w_in": w_in_t,
        "g_q": g_q[l].reshape(1, Q_LORA), "w_uq": w_uq_p.reshape(Q_LORA, MLA_HEADS * LANES).astype(BF16),
        "g_kv": g_kv[l].reshape(1, KV_LORA), "w_k": w_k.astype(BF16), "w_v": w_v.astype(BF16),
        "w_mla_o": w_mla_o[l].astype(BF16), "w_hg_o": w_hg_o[l].astype(BF16),
        "w_out": w_out[l].astype(BF16), "g_ffn": g_ffn[l].reshape(1, d),
        "w_ffn_in": w_ffn_in[l].astype(BF16),
        "conv_w": conv_w[l], "conv_b": conv_b[l].reshape(1, D_FF),
        "w_ffn_out": w_ffn_out[l].astype(BF16), "g_final": g_final.reshape(1, d),
        "g_hg": g_hg[l].reshape(1, HG_WIDTH),
    }


def _trunk(x, mod, tabs, cache, pw, lb_param, *, layer, batch, seq_len, mod_row0, per_seq_mod,
           emit_cache, tm, tq, tag):
    n_tok = batch * seq_len
    hw = MLA_HEADS * LANES
    outs = _proj_call(x, mod, tabs, pw, seq_len=seq_len, mod_row0=mod_row0, per_seq_mod=per_seq_mod,
                      emit_cache=emit_cache, tm=tm)
    q, k, v, hgin, gate = outs[:5]
    hgin = hgin.reshape(batch, seq_len, 5 * HG_WIDTH)
    kvs = [(k.reshape(batch, seq_len, hw), v.reshape(batch, seq_len, hw))]
    s0 = None
    if cache is not None:
        ckv_c, kr_c, s0f, s0b = cache
        past = ckv_c.shape[1]
        kc, vc = _cachekv_call(ckv_c.reshape(batch * past, KV_LORA), kr_c.reshape(batch * past, QK_ROPE),
                               pw, tm=min(512, batch * past))
        kvs.append((kc.reshape(batch, past, hw), vc.reshape(batch, past, hw)))
        s0 = (s0f, s0b)
    full_seq = seq_len == tq and s0 is None
    o_mla, *hg_out = _attn_hg_call(q.reshape(batch, seq_len, hw), kvs, hgin, lb_param, pw["g_hg"], tq=tq,
                                   tk=min(512, seq_len),
                                   heads=MLA_HEADS if seq_len == tq else ATTN_HEADS_PER_STEP,
                                   layer=layer, full_seq=full_seq, name="attn_hg_" + tag)
    if full_seq:
        o_hg, s_f, s_b = hg_out
    else:
        o_hg, s_f, s_b = _hg_scan_call(*hg_out, hgin, pw["g_hg"], s0, name="hg_scan_" + tag)
    common = dict(seq_len=seq_len, mod_row0=mod_row0, per_seq_mod=per_seq_mod, tm=tm)
    y = _mixffn_call(o_mla.reshape(n_tok, MLA_HEADS * V_DIM), o_hg.reshape(n_tok, HG_WIDTH), gate, x,
                     mod, pw, name="mixffn_" + tag, **common)
    extra = outs[5:] if emit_cache else None
    return y, extra, s_f, s_b


def kernel(x_prompt, x_sample, cache_ckv, cache_krope, state_hgrn_fwd, state_hgrn_bwd, c, c_ctx, w_mod,
           b_mod, g_mix, w_in, g_q, w_uq, g_kv, w_ukv, w_mla_o, lb_param, g_hg, w_hg_o, w_out, g_ffn,
           w_ffn_in, conv_w, conv_b, w_ffn_out, g_final):
    batch, seq, d = x_prompt.shape
    dec_batch, dec_seq, _ = x_sample.shape
    depth = w_in.shape[0]
    assert depth == 1, "single trunk layer"
    l = 0
    cvec = jnp.concatenate([c_ctx[None, :], c, jnp.zeros((8 - 1 - dec_batch, d), F32)], axis=0)
    mod = _mod_call(cvec, w_mod[l], b_mod[l]).reshape(8, 6, d)
    pw = _prep_weights(l, g_mix, w_in, g_q, w_uq, g_kv, w_ukv, w_mla_o, g_hg, w_hg_o, w_out, g_ffn,
                       w_ffn_in, conv_w, conv_b, w_ffn_out, g_final)
    tabs = _rope_tables(dec_seq)

    y_ctx, cache_out, s_f, s_b = _trunk(
        x_prompt.reshape(batch * seq, d), mod, None, None, pw, lb_param, layer=l, batch=batch,
        seq_len=seq, mod_row0=0, per_seq_mod=False, emit_cache=True, tm=TOKEN_TILE, tq=min(Q_TILE, seq),
        tag="ctx")
    cache_l = (cache_ckv[:, l], cache_krope[:, l], state_hgrn_fwd[:, l], state_hgrn_bwd[:, l])
    y_lat, _, _, _ = _trunk(
        x_sample.reshape(dec_batch * dec_seq, d), mod, tabs, cache_l, pw, lb_param, layer=l,
        batch=dec_batch, seq_len=dec_seq, mod_row0=1, per_seq_mod=True, emit_cache=False, tm=TOKEN_TILE,
        tq=min(Q_TILE, dec_seq), tag="lat")

    ckv_new, kr_new = cache_out
    return (y_ctx.reshape(batch, seq, d), y_lat.reshape(dec_batch, dec_seq, d),
            ckv_new.reshape(batch, 1, seq, KV_LORA), kr_new.reshape(batch, 1, seq, QK_ROPE),
            s_f.reshape(batch, 1, HG_HEADS, HG_DK, HG_DV), s_b.reshape(batch, 1, HG_HEADS, HG_DK, HG_DV))
```

```python
import functools

import jax
import jax.numpy as jnp
from jax import lax
from jax.experimental import pallas as pl
from jax.experimental.pallas import tpu as pltpu

F32 = jnp.float32
BF16 = jnp.bfloat16

D_MODEL = 1024
GRID_W = 64
MLA_HEADS = 8
QK_NOPE = 64
QK_ROPE = 32
V_DIM = 64
Q_LORA = 384
KV_LORA = 256
ROPE_BASE = 10000.0
HG_HEADS = 4
HG_DK = 128
HG_DV = 128
HG_WIDTH = HG_HEADS * HG_DV
D_FF = 2816
EPS = 1e-6

LANES = 128
SUBLANES = 8
BF16_SUBLANES = 16
HG_CHUNK = 128
HG_LEVELS = 7
HG_UNROLL = 4
FF_CHUNK = 256
TOKEN_TILE = 512
Q_TILE = 512
ATTN_HEADS_PER_STEP = 8
VMEM_LIMIT = 56 * 1024 * 1024

C_PQ = 0
C_PKV = Q_LORA
C_PKR = C_PKV + KV_LORA
C_HQ = C_PKR + QK_ROPE
C_ZF = C_HQ + HG_WIDTH
C_HG = C_ZF + 3 * HG_WIDTH
C_GA = C_HG + HG_WIDTH
C_END = C_GA + 2 * D_MODEL
KR_LANE0 = LANES - QK_ROPE

ATTN_SCALE = (QK_NOPE + QK_ROPE) ** -0.5 * 1.4426950408889634

NT = (((1,), (1,)), ((), ()))
TN = (((0,), (0,)), ((), ()))


def _rms(x, g):
    return x * lax.rsqrt(jnp.mean(x * x, axis=-1, keepdims=True) + EPS) * g


def _const_spec(shape):
    n = len(shape)
    return pl.BlockSpec(shape, lambda *_: (0,) * n, pipeline_mode=pl.Buffered(1))


def _params(sem):
    return pltpu.CompilerParams(dimension_semantics=sem, vmem_limit_bytes=VMEM_LIMIT)


def _mod_kernel(c_ref, w_ref, b_ref, o_ref):
    c = c_ref[...]
    a = c * jax.nn.sigmoid(c)
    o_ref[...] = jnp.dot(a, w_ref[...], precision=lax.Precision.HIGHEST,
                         preferred_element_type=F32) + b_ref[...]


def _mod_call(cvec, w_mod, b_mod):
    n = w_mod.shape[1]
    tn = 1024
    return pl.pallas_call(
        _mod_kernel,
        grid=(n // tn,),
        in_specs=[pl.BlockSpec((8, D_MODEL), lambda j: (0, 0)),
                  pl.BlockSpec((D_MODEL, tn), lambda j: (0, j)),
                  pl.BlockSpec((1, tn), lambda j: (0, j))],
        out_specs=pl.BlockSpec((8, tn), lambda j: (0, j)),
        out_shape=jax.ShapeDtypeStruct((8, n), F32),
        compiler_params=_params(("arbitrary",)),
        name="mod",
    )(cvec, w_mod, b_mod.reshape(1, n))


def _proj_kernel(*refs, rope, emit_cache):
    it = iter(refs)
    x_ref = next(it)
    mod_ref = next(it)
    tab_ref = next(it) if rope else None
    gmix_ref, win_ref, gq_ref, wuq_ref, gkv_ref, wk_ref, wv_ref = (next(it) for _ in range(7))
    q_ref, k_ref, v_ref, hg_ref, gate_ref = (next(it) for _ in range(5))
    if emit_cache:
        ckv_ref, kr_ref = next(it), next(it)

    sh1 = mod_ref[0, 0:1, :]
    sc1 = mod_ref[0, 1:2, :]
    hb = (_rms(x_ref[...], gmix_ref[...]) * (1.0 + sc1) + sh1).astype(BF16)

    def proj(c0, c1):
        return lax.dot_general(hb, win_ref[c0:c1, :], NT, preferred_element_type=F32)

    if rope:
        cs1, sa1, sb1 = tab_ref[0], tab_ref[1], tab_ref[2]

        def rot(x, reps):
            w = x.shape[1]
            cs = jnp.concatenate([cs1] * reps, axis=1) if reps > 1 else cs1
            sa = jnp.concatenate([sa1] * reps, axis=1) if reps > 1 else sa1
            sb = jnp.concatenate([sb1] * reps, axis=1) if reps > 1 else sb1
            half = QK_ROPE // 2
            return x * cs + pltpu.roll(x, w - half, 1) * sa + pltpu.roll(x, half, 1) * sb

    qn = _rms(proj(C_PQ, C_PKV), gq_ref[...]).astype(BF16)
    q = jnp.dot(qn, wuq_ref[...], preferred_element_type=F32)
    q = rot(q, MLA_HEADS) if rope else q * ATTN_SCALE
    q_ref[...] = q.astype(BF16)

    ckv = _rms(proj(C_PKV, C_PKR), gkv_ref[...])
    pkr = proj(C_HQ - LANES, C_HQ)
    if emit_cache:
        ckv_ref[...] = ckv
        kr_ref[...] = pkr[:, KR_LANE0:]
    ckv_b = ckv.astype(BF16)
    kr = rot(pkr, 1) if rope else pkr
    kin = jnp.concatenate([ckv_b, kr.astype(BF16)], axis=1)
    k_ref[...] = jnp.dot(kin, wk_ref[...], preferred_element_type=F32).astype(BF16)
    lane = lax.broadcasted_iota(jnp.int32, (1, MLA_HEADS * LANES), 1)
    ones = jnp.where((lane & (LANES - 1)) >= V_DIM, 1.0, 0.0)
    v_ref[...] = (jnp.dot(ckv_b, wv_ref[...], preferred_element_type=F32) + ones).astype(BF16)

    hq = proj(C_HQ, C_ZF)
    hg_ref[:, 0:HG_WIDTH] = hq * jax.nn.sigmoid(hq)
    hg_ref[:, HG_WIDTH:4 * HG_WIDTH] = proj(C_ZF, C_HG)
    og = proj(C_HG, C_GA)
    hg_ref[:, 4 * HG_WIDTH:5 * HG_WIDTH] = og * jax.nn.sigmoid(og)

    gate_ref[...] = jax.nn.sigmoid(proj(C_GA, C_END)).astype(BF16)


def _proj_call(x, mod, tabs, pw, *, seq_len, mod_row0, per_seq_mod, emit_cache, tm):
    n_tok = x.shape[0]
    tps = max(seq_len // tm, 1)
    rope = tabs is not None
    if per_seq_mod:
        mod_map = lambda t: (mod_row0 + t // tps, 0, 0)
    else:
        mod_map = lambda t: (mod_row0, 0, 0)
    row = lambda w: pl.BlockSpec((tm, w), lambda t: (t, 0))
    in_specs = [row(D_MODEL), pl.BlockSpec((1, 6, D_MODEL), mod_map)]
    args = [x, mod]
    if rope:
        in_specs.append(pl.BlockSpec((3, tm, LANES), lambda t: (0, t % tps, 0)))
        args.append(tabs)
    for name in ("g_mix", "w_in", "g_q", "w_uq", "g_kv", "w_k", "w_v"):
        in_specs.append(_const_spec(pw[name].shape))
        args.append(pw[name])
    hw = MLA_HEADS * LANES
    out_shape = [jax.ShapeDtypeStruct((n_tok, hw), BF16),
                 jax.ShapeDtypeStruct((n_tok, hw), BF16),
                 jax.ShapeDtypeStruct((n_tok, hw), BF16),
                 jax.ShapeDtypeStruct((n_tok, 5 * HG_WIDTH), F32),
                 jax.ShapeDtypeStruct((n_tok, 2 * D_MODEL), BF16)]
    out_specs = [row(hw), row(hw), row(hw), row(5 * HG_WIDTH), row(2 * D_MODEL)]
    if emit_cache:
        out_shape += [jax.ShapeDtypeStruct((n_tok, KV_LORA), F32),
                      jax.ShapeDtypeStruct((n_tok, QK_ROPE), F32)]
        out_specs += [row(KV_LORA), row(QK_ROPE)]
    return pl.pallas_call(
        functools.partial(_proj_kernel, rope=rope, emit_cache=emit_cache),
        grid=(n_tok // tm,),
        in_specs=in_specs,
        out_specs=out_specs,
        out_shape=out_shape,
        compiler_params=_params(("arbitrary",)),
        name="proj_lat" if rope else "proj_ctx",
    )(*args)


def _cachekv_kernel(ckv_ref, kr_ref, wk_ref, wv_ref, k_ref, v_ref):
    ckv_b = ckv_ref[...].astype(BF16)
    tm = ckv_b.shape[0]
    kr = jnp.concatenate([jnp.zeros((tm, KR_LANE0), BF16), kr_ref[...].astype(BF16)], axis=1)
    kin = jnp.concatenate([ckv_b, kr], axis=1)
    k_ref[...] = jnp.dot(kin, wk_ref[...], preferred_element_type=F32).astype(BF16)
    lane = lax.broadcasted_iota(jnp.int32, (1, MLA_HEADS * LANES), 1)
    ones = jnp.where((lane & (LANES - 1)) >= V_DIM, 1.0, 0.0)
    v_ref[...] = (jnp.dot(ckv_b, wv_ref[...], preferred_element_type=F32) + ones).astype(BF16)


def _cachekv_call(ckv, kr, pw, tm):
    n_tok = ckv.shape[0]
    hw = MLA_HEADS * LANES
    row = lambda w: pl.BlockSpec((tm, w), lambda t: (t, 0))
    return pl.pallas_call(
        _cachekv_kernel,
        grid=(n_tok // tm,),
        in_specs=[row(KV_LORA), row(QK_ROPE), _const_spec(pw["w_k"].shape), _const_spec(pw["w_v"].shape)],
        out_specs=[row(hw), row(hw)],
        out_shape=[jax.ShapeDtypeStruct((n_tok, hw), BF16)] * 2,
        compiler_params=_params(("arbitrary",)),
        name="cache_kv",
    )(ckv, kr, pw["w_k"], pw["w_v"])


def _hg_consts():
    c_len, sub = HG_CHUNK, SUBLANES
    row8 = lax.broadcasted_iota(jnp.int32, (1, sub, LANES), 1)
    upper8 = [(row8 & (1 << j)) != 0 for j in range(3)]
    ti = lax.broadcasted_iota(jnp.int32, (c_len, c_len), 0)
    si = lax.broadcasted_iota(jnp.int32, (c_len, c_len), 1)
    txs = ti ^ si
    lvl = jnp.full((c_len, c_len), -1, jnp.int32)
    for j in range(HG_LEVELS):
        lvl = lvl + jnp.where(txs >= (1 << j), 1, 0)
    return upper8, lvl


def _hg_lower_bounds(lbp, layer):
    e = jnp.exp(lbp - jnp.max(lbp, axis=0, keepdims=True))
    return jnp.sum(e[:layer + 1], axis=0) / jnp.sum(e, axis=0)


def _hg_gates(z, lbv):
    f = lbv + (1.0 - lbv) * jax.nn.sigmoid(z)
    return jnp.log2(f), 1.0 - f


def _hg_rep(slab, n_rows):
    return slab if n_rows == SUBLANES else jnp.concatenate([slab] * (n_rows // SUBLANES), axis=0)


def _hg_halves(j):
    h = 1 << j
    return [(slice(b, b + h), slice(b + h, b + 2 * h)) for b in range(0, HG_CHUNK, 2 * h)]


def _hg_sel8(upper8, j, a, b):
    shape3 = (HG_CHUNK // SUBLANES, SUBLANES, LANES)
    return jnp.where(upper8[j], a.reshape(shape3), b.reshape(shape3)).reshape(HG_CHUNK, LANES)


def _hg_upsweep(x, inclusive, upper8):
    c_len, sub = HG_CHUNK, SUBLANES
    x3 = x.reshape(c_len // sub, sub, LANES)
    pre3 = [x3 if inclusive else jnp.zeros_like(x3)]
    tot3 = [x3]
    for j in range(3):
        h = 1 << j
        t = tot3[-1]
        if 2 * h == sub:
            sib = pltpu.roll(t, h, 1)
        else:
            sib = jnp.where(upper8[j], pltpu.roll(t, h, 1), pltpu.roll(t, sub - h, 1))
        tot3.append(t + sib)
        pre3.append(pre3[-1] + jnp.where(upper8[j], sib, 0.0))
    pre = [p.reshape(c_len, LANES) for p in pre3]
    tot = [t.reshape(c_len, LANES) for t in tot3]
    slabs = {3: [tot[3][r:r + sub] for r in range(0, c_len, sub)]}
    for j in range(3, HG_LEVELS):
        h = 1 << j
        p, s = pre[-1], slabs[j]
        parts = []
        for i, (lo, up) in enumerate(_hg_halves(j)):
            parts += [p[lo], p[up] + _hg_rep(s[2 * i], h)]
        pre.append(jnp.concatenate(parts, axis=0))
        slabs[j + 1] = [s[2 * i] + s[2 * i + 1] for i in range(len(s) // 2)]
    return pre, tot, slabs


def _hg_chunk_terms(q, zf, zb, v, lb_f, lb_b, upper8, lvl):
    c_len, top = HG_CHUNK, HG_LEVELS
    vb = v.astype(BF16)
    lgf, kf = _hg_gates(zf, lb_f)
    lgb, kb = _hg_gates(zb, lb_b)
    pf, tf, sf = _hg_upsweep(lgf, True, upper8)
    xb, tb, sb = _hg_upsweep(lgb, False, upper8)
    a = lax.dot_general(q.astype(BF16), (kf + kb).astype(BF16), NT, preferred_element_type=F32)
    a = jnp.where(lvl == -1, a, 0.0)
    for j in range(HG_LEVELS):
        h = 1 << j
        if j < 3:
            eq = _hg_sel8(upper8, j, pf[j], tb[j] - xb[j])
            ek = _hg_sel8(upper8, j, xb[j], tf[j] - pf[j])
            ksel = _hg_sel8(upper8, j, kb, kf)
        else:
            eqs, eks, kss = [], [], []
            for i, (lo, up) in enumerate(_hg_halves(j)):
                eqs += [_hg_rep(sb[j][2 * i], h) - xb[j][lo], pf[j][up]]
                eks += [_hg_rep(sf[j][2 * i], h) - pf[j][lo], xb[j][up]]
                kss += [kf[lo], kb[up]]
            eq, ek, ksel = (jnp.concatenate(p, axis=0) for p in (eqs, eks, kss))
        qs = (q * jnp.exp2(eq)).astype(BF16)
        ks = (ksel * jnp.exp2(ek)).astype(BF16)
        aj = lax.dot_general(qs, ks, NT, preferred_element_type=F32)
        a = jnp.where(lvl == j, aj, a)
    o_intra = jnp.dot(a.astype(BF16), vb, preferred_element_type=F32)
    tot_f, tot_b = sf[top][0], sb[top][0]
    qhf = (q * jnp.exp2(pf[top])).astype(BF16)
    qhb = (q * jnp.exp2(_hg_rep(tot_b, c_len) - xb[top])).astype(BF16)
    khf = (kf * jnp.exp2(_hg_rep(tot_f, c_len) - pf[top])).astype(BF16)
    khb = (kb * jnp.exp2(xb[top])).astype(BF16)
    utf = lax.dot_general(vb, khf, TN, preferred_element_type=F32)
    utb = lax.dot_general(vb, khb, TN, preferred_element_type=F32)
    return o_intra, qhf, qhb, utf, utb, jnp.exp2(tot_f[0:1, :]), jnp.exp2(tot_b[0:1, :])


def _attn_hg_kernel(*refs, nseg, tk, layer, full_seq):
    it = iter(refs)
    q_ref = next(it)
    kv = [next(it) for _ in range(2 * nseg)]
    hq_ref, zf_ref, zb_ref, hv_ref, lbp_ref = (next(it) for _ in range(5))
    if full_seq:
        g_ref, ghg_ref = next(it), next(it)
        o_ref, ohg_ref, sf_ref, sb_ref = (next(it) for _ in range(4))
    else:
        o_ref, oi_ref, qhf_ref, qhb_ref, utf_ref, utb_ref, df_ref, db_ref = (next(it) for _ in range(8))
    s_ref = next(it)
    tq = q_ref.shape[1]
    heads = q_ref.shape[2] // LANES
    hg_heads = hq_ref.shape[2] // LANES
    lane = lax.broadcasted_iota(jnp.int32, (1, LANES), 1)
    chunks = [(kv[2 * seg], kv[2 * seg + 1], c * tk)
              for seg in range(nseg) for c in range(kv[2 * seg].shape[1] // tk)]

    upper8, lvl = _hg_consts()
    lb = _hg_lower_bounds(lbp_ref[...], layer)
    local = [(ci, h) for ci in range(tq // HG_CHUNK) for h in range(hg_heads)]
    per_head = len(local) // heads
    kept = {}

    def hg_local(ci, h):
        rs = slice(ci * HG_CHUNK, (ci + 1) * HG_CHUNK)
        cs = slice(h * LANES, (h + 1) * LANES)
        terms = _hg_chunk_terms(
            hq_ref[0, rs, cs], zf_ref[0, rs, cs], zb_ref[0, rs, cs], hv_ref[0, rs, cs],
            lb[0:1, cs], lb[1:2, cs], upper8, lvl)
        if full_seq:
            kept[(ci, h)] = terms
            return
        oi, qhf, qhb, utf, utb, dec_f, dec_b = terms
        oi_ref[0, rs, cs] = oi
        qhf_ref[0, rs, cs] = qhf
        qhb_ref[0, rs, cs] = qhb
        utf_ref[0, ci, h] = utf
        utb_ref[0, ci, h] = utb
        df_ref[0, ci, :, cs] = dec_f
        db_ref[0, ci, :, cs] = dec_b

    outs = []
    for h in range(heads):
        hs = slice(h * LANES, (h + 1) * LANES)
        qh = q_ref[0, :, hs]
        mx = jnp.full((tq, LANES), -jnp.inf, F32)
        for i, (k_ref, _, off) in enumerate(chunks):
            s = lax.dot_general(qh, k_ref[0, off:off + tk, hs], NT, preferred_element_type=F32)
            s_ref[:, i * tk:(i + 1) * tk] = s
            for w in range(tk // LANES):
                mx = jnp.maximum(mx, s[:, w * LANES:(w + 1) * LANES])
        m = jnp.broadcast_to(jnp.max(mx, axis=1, keepdims=True), (tq, LANES))
        mb = jnp.concatenate([m] * (tk // LANES), axis=1)
        acc = jnp.zeros((tq, LANES), F32)
        for i, (_, v_ref, off) in enumerate(chunks):
            p = jnp.exp2(s_ref[:, i * tk:(i + 1) * tk] - mb).astype(BF16)
            acc = acc + jnp.dot(p, v_ref[0, off:off + tk, hs], preferred_element_type=F32)
        outs.append(acc / pltpu.roll(acc, V_DIM, 1))
        for ci, hh in local[h * per_head:(h + 1) * per_head]:
            hg_local(ci, hh)
    for pair in range(heads // 2):
        a = outs[2 * pair]
        b = pltpu.roll(outs[2 * pair + 1], V_DIM, 1)
        o_ref[0, :, pair * LANES:(pair + 1) * LANES] = jnp.where(lane < V_DIM, a, b).astype(BF16)

    if full_seq:
        n = tq // HG_CHUNK
        for h in range(hg_heads):
            cs = slice(h * LANES, (h + 1) * LANES)
            enter_b, st = [None] * n, None
            for c in reversed(range(n)):
                enter_b[c] = st
                _, _, _, _, utb, _, dec_b = kept[(c, h)]
                st = utb if st is None else dec_b * st + utb
            sb_ref[0, h] = st.T
            st = None
            for c in range(n):
                rs = slice(c * HG_CHUNK, (c + 1) * HG_CHUNK)
                oi, qhf, qhb, utf, _, dec_f, _ = kept[(c, h)]
                if st is not None:
                    oi = oi + lax.dot_general(qhf, st.astype(BF16), NT, preferred_element_type=F32)
                if enter_b[c] is not None:
                    oi = oi + lax.dot_general(qhb, enter_b[c].astype(BF16), NT, preferred_element_type=F32)
                ohg_ref[0, rs, cs] = (_rms(oi, ghg_ref[:, cs]) * g_ref[0, rs, cs]).astype(BF16)
                st = utf if st is None else dec_f * st + utf
            sf_ref[0, h] = st.T


def _attn_hg_call(q, kvs, hgin, lb_param, g_hg, *, tq, tk, heads, layer, full_seq, name):
    b, t, _ = q.shape
    n_chunks = t // HG_CHUNK
    cpt = tq // HG_CHUNK
    groups = MLA_HEADS // heads
    hgh = HG_HEADS // groups
    assert heads % 2 == 0 and hgh >= 1 and (cpt * hgh) % heads == 0
    aw, gw = heads * LANES, hgh * LANES
    in_specs = [pl.BlockSpec((1, tq, aw), lambda i, p, j: (i, j, p))]
    args = [q]
    n_keys = 0
    for k, v in kvs:
        n_keys += k.shape[1]
        for a in (k, v):
            mode = dict(pipeline_mode=pl.Buffered(1)) if groups == 1 and t // tq > 1 else {}
            in_specs.append(pl.BlockSpec((1, a.shape[1], aw), lambda i, p, j: (i, 0, p), **mode))
            args.append(a)
    for g in range(4):
        in_specs.append(pl.BlockSpec((1, tq, gw), lambda i, p, j, g=g: (i, j, g * groups + p)))
        args.append(hgin)
    in_specs.append(pl.BlockSpec((lb_param.shape[0], 2, gw), lambda i, p, j: (0, 0, p)))
    args.append(lb_param)
    rows = lambda w: pl.BlockSpec((1, tq, w), lambda i, p, j: (i, j, p))
    o_shape = jax.ShapeDtypeStruct((b, t, MLA_HEADS * V_DIM), BF16)
    if full_seq:
        assert t == tq
        in_specs += [pl.BlockSpec((1, tq, gw), lambda i, p, j: (i, j, 4 * groups + p)),
                     pl.BlockSpec((1, gw), lambda i, p, j: (0, p))]
        args += [hgin, g_hg]
        st_spec = pl.BlockSpec((1, hgh, HG_DK, HG_DV), lambda i, p, j: (i, p, 0, 0))
        st_shape = jax.ShapeDtypeStruct((b, HG_HEADS, HG_DK, HG_DV), F32)
        out_specs = [rows(heads * V_DIM), rows(gw), st_spec, st_spec]
        out_shape = [o_shape, jax.ShapeDtypeStruct((b, t, HG_WIDTH), BF16), st_shape, st_shape]
    else:
        ut_spec = pl.BlockSpec((1, cpt, hgh, HG_DV, HG_DK), lambda i, p, j: (i, j, p, 0, 0))
        dec_spec = pl.BlockSpec((1, cpt, 1, gw), lambda i, p, j: (i, j, 0, p))
        ut_shape = jax.ShapeDtypeStruct((b, n_chunks, HG_HEADS, HG_DV, HG_DK), F32)
        dec_shape = jax.ShapeDtypeStruct((b, n_chunks, 1, HG_WIDTH), F32)
        out_specs = [rows(heads * V_DIM), rows(gw), rows(gw), rows(gw),
                     ut_spec, ut_spec, dec_spec, dec_spec]
        out_shape = [o_shape,
                     jax.ShapeDtypeStruct((b, t, HG_WIDTH), F32),
                     jax.ShapeDtypeStruct((b, t, HG_WIDTH), BF16),
                     jax.ShapeDtypeStruct((b, t, HG_WIDTH), BF16),
                     ut_shape, ut_shape, dec_shape, dec_shape]
    return pl.pallas_call(
        functools.partial(_attn_hg_kernel, nseg=len(kvs), tk=tk, layer=layer, full_seq=full_seq),
        grid=(b, groups, t // tq),
        in_specs=in_specs,
        out_specs=out_specs,
        out_shape=out_shape,
        scratch_shapes=[pltpu.VMEM((tq, n_keys), F32)],
        compiler_params=_params(("arbitrary", "arbitrary", "arbitrary")),
        name=name,
    )(*args)


def _hg_scan_kernel(*refs, has_s0):
    it = iter(refs)
    oi_ref, qhf_ref, qhb_ref, utf_ref, utb_ref, df_ref, db_ref, g_ref, ghg_ref = (next(it) for _ in range(9))
    if has_s0:
        s0f_ref, s0b_ref = next(it), next(it)
    o_ref, sf_ref, sb_ref, sbst_ref = (next(it) for _ in range(4))
    c_len = HG_CHUNK
    n_chunks = utf_ref.shape[1]
    unroll = min(HG_UNROLL, n_chunks)

    def rows_of(c):
        return pl.ds(pl.multiple_of(c * c_len, c_len), c_len)

    def bwd_body(i, st):
        c = n_chunks - 1 - i
        sbst_ref[c] = st.astype(BF16)
        return db_ref[0, c] * st + utb_ref[0, c, 0]

    st_b0 = s0b_ref[0, 0].T if has_s0 else jnp.zeros((HG_DV, HG_DK), F32)
    st_b = lax.fori_loop(0, n_chunks, bwd_body, st_b0, unroll=unroll)
    sb_ref[0, 0] = st_b.T

    ghg = ghg_ref[...]

    def fwd_body(c, st):
        qcat = jnp.concatenate([qhf_ref[0, rows_of(c), :], qhb_ref[0, rows_of(c), :]], axis=1)
        scat = jnp.concatenate([st.astype(BF16), sbst_ref[c]], axis=1)
        o = oi_ref[0, rows_of(c), :] + lax.dot_general(qcat, scat, NT, preferred_element_type=F32)
        on = _rms(o, ghg) * g_ref[0, rows_of(c), :]
        o_ref[0, rows_of(c), :] = on.astype(BF16)
        return df_ref[0, c] * st + utf_ref[0, c, 0]

    st_f0 = s0f_ref[0, 0].T if has_s0 else jnp.zeros((HG_DV, HG_DK), F32)
    st_f = lax.fori_loop(0, n_chunks, fwd_body, st_f0, unroll=unroll)
    sf_ref[0, 0] = st_f.T


def _hg_scan_call(oi, qhf, qhb, utf, utb, dec_f, dec_b, hgin, g_hg, s0, *, name):
    b, t, _ = oi.shape
    n_chunks = t // HG_CHUNK
    has_s0 = s0 is not None
    col = pl.BlockSpec((1, t, LANES), lambda i, h: (i, 0, h))
    ut_spec = pl.BlockSpec((1, n_chunks, 1, HG_DV, HG_DK), lambda i, h: (i, 0, h, 0, 0))
    dec_spec = pl.BlockSpec((1, n_chunks, 1, LANES), lambda i, h: (i, 0, 0, h))
    gate_col = pl.BlockSpec((1, t, LANES), lambda i, h: (i, 0, 4 * HG_HEADS + h))
    in_specs = [col, col, col, ut_spec, ut_spec, dec_spec, dec_spec, gate_col,
                pl.BlockSpec((1, LANES), lambda i, h: (0, h))]
    args = [oi, qhf, qhb, utf, utb, dec_f, dec_b, hgin, g_hg]
    st_spec = pl.BlockSpec((1, 1, HG_DK, HG_DV), lambda i, h: (i, h, 0, 0))
    if has_s0:
        in_specs += [st_spec, st_spec]
        args += list(s0)
    st_shape = jax.ShapeDtypeStruct((b, HG_HEADS, HG_DK, HG_DV), F32)
    return pl.pallas_call(
        functools.partial(_hg_scan_kernel, has_s0=has_s0),
        grid=(b, HG_HEADS),
        in_specs=in_specs,
        out_specs=[col, st_spec, st_spec],
        out_shape=[jax.ShapeDtypeStruct((b, t, HG_WIDTH), BF16), st_shape, st_shape],
        scratch_shapes=[pltpu.VMEM((n_chunks, HG_DV, HG_DK), BF16)],
        compiler_params=_params(("arbitrary", "arbitrary")),
        name=name,
    )(*args)


def _gelu_tanh(x):
    return 0.5 * x * (1.0 + jnp.tanh(0.7978845608028654 * (x + 0.044715 * (x * x * x))))


def _mixffn_kernel(om_ref, omp_ref, omn_ref, oh_ref, ohp_ref, ohn_ref, gt_ref, gtp_ref, gtn_ref,
                   x_ref, xp_ref, xn_ref, mod_ref, wmo_ref, who_ref, wout_ref, gffn_ref,
                   wi_ref, cw_ref, cb_ref, wo_ref, gfin_ref, y_ref, gsc_ref, u_ref, *, tm, seq_len):
    halo = BF16_SUBLANES
    t = pl.program_id(0)

    def ext(p_ref, m_ref, n_ref):
        return jnp.concatenate([p_ref[...], m_ref[...], n_ref[...]], axis=0)

    a = jnp.dot(ext(omp_ref, om_ref, omn_ref), wmo_ref[...], preferred_element_type=F32)
    b = jnp.dot(ext(ohp_ref, oh_ref, ohn_ref), who_ref[...], preferred_element_type=F32)
    gate = ext(gtp_ref, gt_ref, gtn_ref)
    mix = gate[:, 0:D_MODEL] * a + gate[:, D_MODEL:2 * D_MODEL] * b
    y = jnp.dot(mix.astype(BF16), wout_ref[...], preferred_element_type=F32)
    gt1 = mod_ref[0, 2:3, :]
    sh2 = mod_ref[0, 3:4, :]
    sc2 = mod_ref[0, 4:5, :]
    x1 = ext(xp_ref, x_ref, xn_ref) + gt1 * y
    h2 = (_rms(x1, gffn_ref[...]) * (1.0 + sc2) + sh2).astype(BF16)
    h2_own = h2[halo:halo + tm]

    pos = (lax.broadcasted_iota(jnp.int32, (tm, 1), 0) + t * tm) & (seq_len - 1)
    has_prev = pos != 0
    has_next = pos != seq_len - 1
    for j in range(D_FF // FF_CHUNK):
        cs = slice(j * FF_CHUNK, (j + 1) * FF_CHUNK)
        vs = slice(D_FF + j * FF_CHUNK, D_FF + (j + 1) * FF_CHUNK)
        gsc_ref[...] = jnp.dot(h2, wi_ref[:, cs], preferred_element_type=F32)
        g_prev = jnp.where(has_prev, gsc_ref[halo - 1:halo - 1 + tm, :], 0.0)
        g_cur = gsc_ref[halo:halo + tm, :]
        g_next = jnp.where(has_next, gsc_ref[halo + 1:halo + 1 + tm, :], 0.0)
        gconv = cb_ref[:, cs] + g_prev * cw_ref[0:1, cs] + g_cur * cw_ref[1:2, cs] + g_next * cw_ref[2:3, cs]
        val = jnp.dot(h2_own, wi_ref[:, vs], preferred_element_type=F32)
        u_ref[:, cs] = (_gelu_tanh(gconv) * val).astype(BF16)
    f = jnp.dot(u_ref[...], wo_ref[...], preferred_element_type=F32)
    gt2 = mod_ref[0, 5:6, :]
    y_ref[...] = _rms(x1[halo:halo + tm] + gt2 * f, gfin_ref[...])


def _mixffn_call(om, oh, gate, x, mod, pw, *, seq_len, mod_row0, per_seq_mod, tm, name):
    n_tok = x.shape[0]
    tps = max(seq_len // tm, 1)
    halo = BF16_SUBLANES
    n_hblk = n_tok // halo
    hpt = tm // halo
    if per_seq_mod:
        mod_map = lambda t: (mod_row0 + t // tps, 0, 0)
    else:
        mod_map = lambda t: (mod_row0, 0, 0)

    def with_halo(w):
        return [pl.BlockSpec((tm, w), lambda t: (t, 0)),
                pl.BlockSpec((halo, w), lambda t: (jnp.maximum(t * hpt - 1, 0), 0)),
                pl.BlockSpec((halo, w), lambda t: (jnp.minimum((t + 1) * hpt, n_hblk - 1), 0))]

    ws = [pw[k] for k in ("w_mla_o", "w_hg_o", "w_out", "g_ffn", "w_ffn_in", "conv_w", "conv_b",
                          "w_ffn_out", "g_final")]
    return pl.pallas_call(
        functools.partial(_mixffn_kernel, tm=tm, seq_len=seq_len),
        grid=(n_tok // tm,),
        in_specs=(with_halo(MLA_HEADS * V_DIM) + with_halo(HG_WIDTH) + with_halo(2 * D_MODEL)
                  + with_halo(D_MODEL) + [pl.BlockSpec((1, 6, D_MODEL), mod_map)]
                  + [_const_spec(w.shape) for w in ws]),
        out_specs=pl.BlockSpec((tm, D_MODEL), lambda t: (t, 0)),
        out_shape=jax.ShapeDtypeStruct((n_tok, D_MODEL), F32),
        scratch_shapes=[pltpu.VMEM((tm + 2 * halo, FF_CHUNK), F32),
                        pltpu.VMEM((tm, D_FF), BF16)],
        compiler_params=_params(("arbitrary",)),
        name=name,
    )(om, om, om, oh, oh, oh, gate, gate, gate, x, x, x, mod, *ws)


def _rope_tables(seq_len):
    n_freq = QK_ROPE // 4
    inv = jnp.power(ROPE_BASE, -jnp.arange(n_freq, dtype=F32) / n_freq)
    z8 = jnp.zeros((n_freq,), F32)
    z64 = jnp.zeros((QK_NOPE,), F32)
    inv_r = jnp.concatenate([z64] + [inv, z8] * 4)
    inv_c = jnp.concatenate([z64] + [z8, inv] * 4)
    pos = jnp.arange(seq_len, dtype=jnp.int32)
    r_pos = (pos // GRID_W).astype(F32)[:, None]
    c_pos = (pos % GRID_W).astype(F32)[:, None]
    ang = r_pos * inv_r + c_pos * inv_c
    cos, sin = jnp.cos(ang), jnp.sin(ang)
    s = ATTN_SCALE
    half = QK_ROPE // 2
    o16, z16 = jnp.ones((half,), F32), jnp.zeros((half,), F32)
    c_cs = jnp.concatenate([jnp.full((QK_NOPE + QK_ROPE,), s, F32), o16, o16])
    c_sa = jnp.concatenate([z64, -s * o16, z16, -o16, z16])
    c_sb = jnp.concatenate([z64, z16, s * o16, z16, o16])
    return jnp.stack([cos * c_cs, sin * c_sa, sin * c_sb], axis=0)


def _prep_weights(l, g_mix, w_in, g_q, w_uq, g_kv, w_ukv, w_mla_o, g_hg, w_hg_o, w_out, g_ffn,
                  w_ffn_in, conv_w, conv_b, w_ffn_out, g_final):
    d = D_MODEL
    w_in_t = jnp.swapaxes(w_in[l], 0, 1).astype(BF16)
    hd = QK_NOPE + QK_ROPE
    w_uq_p = jnp.concatenate([w_uq[l].reshape(Q_LORA, MLA_HEADS, hd),
                              jnp.zeros((Q_LORA, MLA_HEADS, LANES - hd), F32)], axis=-1)
    wkv = w_ukv[l].reshape(KV_LORA, MLA_HEADS, QK_NOPE + V_DIM)
    zpad = jnp.zeros((KV_LORA, MLA_HEADS, LANES - QK_NOPE), F32)
    w_k_top = jnp.concatenate([wkv[..., :QK_NOPE], zpad], axis=-1).reshape(KV_LORA, MLA_HEADS * LANES)
    place = jnp.zeros((LANES, LANES), F32).at[
        KR_LANE0 + jnp.arange(QK_ROPE), QK_NOPE + jnp.arange(QK_ROPE)].set(1.0)
    w_k_bot = jnp.tile(place, (1, MLA_HEADS))
    w_k = jnp.concatenate([w_k_top, w_k_bot], axis=0)
    w_v = jnp.concatenate([wkv[..., QK_NOPE:], jnp.zeros((KV_LORA, MLA_HEADS, LANES - V_DIM), F32)],
                          axis=-1).reshape(KV_LORA, MLA_HEADS * LANES)
    return {
        "g_mix": g_mix[l].reshape(1, d), "w_in": w_in_t,
        "g_q": g_q[l].reshape(1, Q_LORA), "w_uq": w_uq_p.reshape(Q_LORA, MLA_HEADS * LANES).astype(BF16),
        "g_kv": g_kv[l].reshape(1, KV_LORA), "w_k": w_k.astype(BF16), "w_v": w_v.astype(BF16),
        "w_mla_o": w_mla_o[l].astype(BF16), "w_hg_o": w_hg_o[l].astype(BF16),
        "w_out": w_out[l].astype(BF16), "g_ffn": g_ffn[l].reshape(1, d),
        "w_ffn_in": w_ffn_in[l].astype(BF16),
        "conv_w": conv_w[l], "conv_b": conv_b[l].reshape(1, D_FF),
        "w_ffn_out": w_ffn_out[l].astype(BF16), "g_final": g_final.reshape(1, d),
        "g_hg": g_hg[l].reshape(1, HG_WIDTH),
    }


def _trunk(x, mod, tabs, cache, pw, lb_param, *, layer, batch, seq_len, mod_row0, per_seq_mod,
           emit_cache, tm, tq, tag):
    n_tok = batch * seq_len
    hw = MLA_HEADS * LANES
    outs = _proj_call(x, mod, tabs, pw, seq_len=seq_len, mod_row0=mod_row0, per_seq_mod=per_seq_mod,
                      emit_cache=emit_cache, tm=tm)
    q, k, v, hgin, gate = outs[:5]
    hgin = hgin.reshape(batch, seq_len, 5 * HG_WIDTH)
    kvs = [(k.reshape(batch, seq_len, hw), v.reshape(batch, seq_len, hw))]
    s0 = None
    if cache is not None:
        ckv_c, kr_c, s0f, s0b = cache
        past = ckv_c.shape[1]
        kc, vc = _cachekv_call(ckv_c.reshape(batch * past, KV_LORA), kr_c.reshape(batch * past, QK_ROPE),
                               pw, tm=min(512, batch * past))
        kvs.append((kc.reshape(batch, past, hw), vc.reshape(batch, past, hw)))
        s0 = (s0f, s0b)
    full_seq = seq_len == tq and s0 is None
    o_mla, *hg_out = _attn_hg_call(q.reshape(batch, seq_len, hw), kvs, hgin, lb_param, pw["g_hg"], tq=tq,
                                   tk=min(512, seq_len),
                                   heads=MLA_HEADS if seq_len == tq else ATTN_HEADS_PER_STEP,
                                   layer=layer, full_seq=full_seq, name="attn_hg_" + tag)
    if full_seq:
        o_hg, s_f, s_b = hg_out
    else:
        o_hg, s_f, s_b = _hg_scan_call(*hg_out, hgin, pw["g_hg"], s0, name="hg_scan_" + tag)
    common = dict(seq_len=seq_len, mod_row0=mod_row0, per_seq_mod=per_seq_mod, tm=tm)
    y = _mixffn_call(o_mla.reshape(n_tok, MLA_HEADS * V_DIM), o_hg.reshape(n_tok, HG_WIDTH), gate, x,
                     mod, pw, name="mixffn_" + tag, **common)
    extra = outs[5:] if emit_cache else None
    return y, extra, s_f, s_b


def kernel(x_prompt, x_sample, cache_ckv, cache_krope, state_hgrn_fwd, state_hgrn_bwd, c, c_ctx, w_mod,
           b_mod, g_mix, w_in, g_q, w_uq, g_kv, w_ukv, w_mla_o, lb_param, g_hg, w_hg_o, w_out, g_ffn,
           w_ffn_in, conv_w, conv_b, w_ffn_out, g_final):
    batch, seq, d = x_prompt.shape
    dec_batch, dec_seq, _ = x_sample.shape
    depth = w_in.shape[0]
    assert depth == 1, "single trunk layer"
    l = 0
    cvec = jnp.concatenate([c_ctx[None, :], c, jnp.zeros((8 - 1 - dec_batch, d), F32)], axis=0)
    mod = _mod_call(cvec, w_mod[l], b_mod[l]).reshape(8, 6, d)
    pw = _prep_weights(l, g_mix, w_in, g_q, w_uq, g_kv, w_ukv, w_mla_o, g_hg, w_hg_o, w_out, g_ffn,
                       w_ffn_in, conv_w, conv_b, w_ffn_out, g_final)
    tabs = _rope_tables(dec_seq)

    y_ctx, cache_out, s_f, s_b = _trunk(
        x_prompt.reshape(batch * seq, d), mod, None, None, pw, lb_param, layer=l, batch=batch,
        seq_len=seq, mod_row0=0, per_seq_mod=False, emit_cache=True, tm=TOKEN_TILE, tq=min(Q_TILE, seq),
        tag="ctx")
    cache_l = (cache_ckv[:, l], cache_krope[:, l], state_hgrn_fwd[:, l], state_hgrn_bwd[:, l])
    y_lat, _, _, _ = _trunk(
        x_sample.reshape(dec_batch * dec_seq, d), mod, tabs, cache_l, pw, lb_param, layer=l,
        batch=dec_batch, seq_len=dec_seq, mod_row0=1, per_seq_mod=True, emit_cache=False, tm=TOKEN_TILE,
        tq=min(Q_TILE, dec_seq), tag="lat")

    ckv_new, kr_new = cache_out
    return (y_ctx.reshape(batch, seq, d), y_lat.reshape(dec_batch, dec_seq, d),
            ckv_new.reshape(batch, 1, seq, KV_LORA), kr_new.reshape(batch, 1, seq, QK_ROPE),
            s_f.reshape(batch, 1, HG_HEADS, HG_DK, HG_DV), s_b.reshape(batch, 1, HG_HEADS, HG_DK, HG_DV))
```

```python
import functools

import jax
import jax.numpy as jnp
from jax import lax
from jax.experimental import pallas as pl
from jax.experimental.pallas import tpu as pltpu

F32 = jnp.float32
BF16 = jnp.bfloat16

D_MODEL = 1024
GRID_W = 64
MLA_HEADS = 8
QK_NOPE = 64
QK_ROPE = 32
V_DIM = 64
Q_LORA = 384
KV_LORA = 256
ROPE_BASE = 10000.0
HG_HEADS = 4
HG_DK = 128
HG_DV = 128
HG_WIDTH = HG_HEADS * HG_DV
D_FF = 2816
EPS = 1e-6

LANES = 128
SUBLANES = 8
BF16_SUBLANES = 16
HG_CHUNK = 128
HG_LEVELS = 7
HG_UNROLL = 4
FF_CHUNK = 256
TOKEN_TILE = 512
Q_TILE = 512
ATTN_HEADS_PER_STEP = 8
VMEM_LIMIT = 56 * 1024 * 1024

C_PQ = 0
C_PKV = Q_LORA
C_PKR = C_PKV + KV_LORA
C_HQ = C_PKR + QK_ROPE
C_ZF = C_HQ + HG_WIDTH
C_HG = C_ZF + 3 * HG_WIDTH
C_GA = C_HG + HG_WIDTH
C_END = C_GA + 2 * D_MODEL
KR_LANE0 = LANES - QK_ROPE

ATTN_SCALE = (QK_NOPE + QK_ROPE) ** -0.5 * 1.4426950408889634

NT = (((1,), (1,)), ((), ()))
TN = (((0,), (0,)), ((), ()))


def _rms(x, g):
    return x * lax.rsqrt(jnp.mean(x * x, axis=-1, keepdims=True) + EPS) * g


def _const_spec(shape):
    n = len(shape)
    return pl.BlockSpec(shape, lambda *_: (0,) * n, pipeline_mode=pl.Buffered(1))


def _params(sem):
    return pltpu.CompilerParams(dimension_semantics=sem, vmem_limit_bytes=VMEM_LIMIT)


def _mod_kernel(c_ref, w_ref, b_ref, o_ref):
    c = c_ref[...]
    a = c * jax.nn.sigmoid(c)
    o_ref[...] = jnp.dot(a, w_ref[...], precision=lax.Precision.HIGHEST,
                         preferred_element_type=F32) + b_ref[...]


def _mod_call(cvec, w_mod, b_mod):
    n = w_mod.shape[1]
    tn = 2048
    return pl.pallas_call(
        _mod_kernel,
        grid=(n // tn,),
        in_specs=[pl.BlockSpec((8, D_MODEL), lambda j: (0, 0)),
                  pl.BlockSpec((D_MODEL, tn), lambda j: (0, j)),
                  pl.BlockSpec((1, tn), lambda j: (0, j))],
        out_specs=pl.BlockSpec((8, tn), lambda j: (0, j)),
        out_shape=jax.ShapeDtypeStruct((8, n), F32),
        compiler_params=_params(("arbitrary",)),
        name="mod",
    )(cvec, w_mod, b_mod.reshape(1, n))


def _place_rope_key(k_nope, kr):
    lane = lax.broadcasted_iota(jnp.int32, (1, LANES), 1)
    kr = jnp.where(lane >= QK_NOPE, jnp.where(lane < QK_NOPE + QK_ROPE, kr, 0.0), 0.0)
    return k_nope + jnp.concatenate([kr] * MLA_HEADS, axis=1)


def _proj_kernel(*refs, rope, emit_cache):
    it = iter(refs)
    x_ref = next(it)
    mod_ref = next(it)
    tab_ref = next(it) if rope else None
    gmix_ref, win_ref, gq_ref, wuq_ref, gkv_ref, wk_ref, wv_ref = (next(it) for _ in range(7))
    q_ref, k_ref, v_ref, hg_ref, gate_ref = (next(it) for _ in range(5))
    if emit_cache:
        ckv_ref, kr_ref = next(it), next(it)

    sh1 = mod_ref[0, 0:1, :]
    sc1 = mod_ref[0, 1:2, :]
    hb = (_rms(x_ref[...], gmix_ref[...]) * (1.0 + sc1) + sh1).astype(BF16)

    def proj(c0, c1):
        return lax.dot_general(hb, win_ref[c0:c1, :], NT, preferred_element_type=F32)

    if rope:
        cs1, sa1, sb1 = tab_ref[0], tab_ref[1], tab_ref[2]

        def rot(x, reps):
            w = x.shape[1]
            cs = jnp.concatenate([cs1] * reps, axis=1) if reps > 1 else cs1
            sa = jnp.concatenate([sa1] * reps, axis=1) if reps > 1 else sa1
            sb = jnp.concatenate([sb1] * reps, axis=1) if reps > 1 else sb1
            half = QK_ROPE // 2
            return x * cs + pltpu.roll(x, w - half, 1) * sa + pltpu.roll(x, half, 1) * sb

    qn = _rms(proj(C_PQ, C_PKV), gq_ref[...]).astype(BF16)
    q = jnp.dot(qn, wuq_ref[...], preferred_element_type=F32)
    q = rot(q, MLA_HEADS) if rope else q * ATTN_SCALE
    q_ref[...] = q.astype(BF16)

    ckv = _rms(proj(C_PKV, C_PKR), gkv_ref[...])
    pkr = proj(C_HQ - LANES, C_HQ)
    if emit_cache:
        ckv_ref[...] = ckv
        kr_ref[...] = pkr[:, KR_LANE0:]
    ckv_b = ckv.astype(BF16)
    kr = rot(pkr, 1) if rope else pkr
    k_ref[...] = _place_rope_key(jnp.dot(ckv_b, wk_ref[...], preferred_element_type=F32),
                                 pltpu.roll(kr, LANES - QK_ROPE, 1)).astype(BF16)
    lane = lax.broadcasted_iota(jnp.int32, (1, MLA_HEADS * LANES), 1)
    ones = jnp.where((lane & (LANES - 1)) >= V_DIM, 1.0, 0.0)
    v_ref[...] = (jnp.dot(ckv_b, wv_ref[...], preferred_element_type=F32) + ones).astype(BF16)

    gate_ref[...] = jax.nn.sigmoid(proj(C_GA, C_END)).astype(BF16)

    hq = proj(C_HQ, C_ZF)
    hg_ref[:, 0:HG_WIDTH] = hq * jax.nn.sigmoid(hq)
    og = proj(C_HG, C_GA)
    hg_ref[:, 4 * HG_WIDTH:5 * HG_WIDTH] = og * jax.nn.sigmoid(og)
    hg_ref[:, HG_WIDTH:4 * HG_WIDTH] = proj(C_ZF, C_HG)


def _proj_call(x, mod, tabs, pw, *, seq_len, mod_row0, per_seq_mod, emit_cache, tm):
    n_tok = x.shape[0]
    tps = max(seq_len // tm, 1)
    rope = tabs is not None
    if per_seq_mod:
        mod_map = lambda t: (mod_row0 + t // tps, 0, 0)
    else:
        mod_map = lambda t: (mod_row0, 0, 0)
    row = lambda w: pl.BlockSpec((tm, w), lambda t: (t, 0))
    in_specs = [row(D_MODEL), pl.BlockSpec((1, 6, D_MODEL), mod_map)]
    args = [x, mod]
    if rope:
        in_specs.append(pl.BlockSpec((3, tm, LANES), lambda t: (0, t % tps, 0)))
        args.append(tabs)
    for name in ("g_mix", "w_in", "g_q", "w_uq", "g_kv", "w_k", "w_v"):
        in_specs.append(_const_spec(pw[name].shape))
        args.append(pw[name])
    hw = MLA_HEADS * LANES
    out_shape = [jax.ShapeDtypeStruct((n_tok, hw), BF16),
                 jax.ShapeDtypeStruct((n_tok, hw), BF16),
                 jax.ShapeDtypeStruct((n_tok, hw), BF16),
                 jax.ShapeDtypeStruct((n_tok, 5 * HG_WIDTH), F32),
                 jax.ShapeDtypeStruct((n_tok, 2 * D_MODEL), BF16)]
    out_specs = [row(hw), row(hw), row(hw), row(5 * HG_WIDTH), row(2 * D_MODEL)]
    if emit_cache:
        out_shape += [jax.ShapeDtypeStruct((n_tok, KV_LORA), F32),
                      jax.ShapeDtypeStruct((n_tok, QK_ROPE), F32)]
        out_specs += [row(KV_LORA), row(QK_ROPE)]
    return pl.pallas_call(
        functools.partial(_proj_kernel, rope=rope, emit_cache=emit_cache),
        grid=(n_tok // tm,),
        in_specs=in_specs,
        out_specs=out_specs,
        out_shape=out_shape,
        compiler_params=_params(("arbitrary",)),
        name="proj_lat" if rope else "proj_ctx",
    )(*args)


def _cachekv_kernel(ckv_ref, kr_ref, wk_ref, wv_ref, k_ref, v_ref):
    ckv_b = ckv_ref[...].astype(BF16)
    tm = ckv_b.shape[0]
    kr = jnp.concatenate([jnp.zeros((tm, QK_NOPE), F32), kr_ref[...],
                          jnp.zeros((tm, LANES - QK_NOPE - QK_ROPE), F32)], axis=1)
    k_ref[...] = _place_rope_key(jnp.dot(ckv_b, wk_ref[...], preferred_element_type=F32), kr).astype(BF16)
    lane = lax.broadcasted_iota(jnp.int32, (1, MLA_HEADS * LANES), 1)
    ones = jnp.where((lane & (LANES - 1)) >= V_DIM, 1.0, 0.0)
    v_ref[...] = (jnp.dot(ckv_b, wv_ref[...], preferred_element_type=F32) + ones).astype(BF16)


def _cachekv_call(ckv, kr, pw, tm):
    n_tok = ckv.shape[0]
    hw = MLA_HEADS * LANES
    row = lambda w: pl.BlockSpec((tm, w), lambda t: (t, 0))
    return pl.pallas_call(
        _cachekv_kernel,
        grid=(n_tok // tm,),
        in_specs=[row(KV_LORA), row(QK_ROPE), _const_spec(pw["w_k"].shape), _const_spec(pw["w_v"].shape)],
        out_specs=[row(hw), row(hw)],
        out_shape=[jax.ShapeDtypeStruct((n_tok, hw), BF16)] * 2,
        compiler_params=_params(("arbitrary",)),
        name="cache_kv",
    )(ckv, kr, pw["w_k"], pw["w_v"])


def _hg_consts():
    c_len, sub = HG_CHUNK, SUBLANES
    row8 = lax.broadcasted_iota(jnp.int32, (1, sub, LANES), 1)
    upper8 = [(row8 & (1 << j)) != 0 for j in range(3)]
    ti = lax.broadcasted_iota(jnp.int32, (c_len, c_len), 0)
    si = lax.broadcasted_iota(jnp.int32, (c_len, c_len), 1)
    txs = ti ^ si
    lvl = jnp.full((c_len, c_len), -1, jnp.int32)
    for j in range(HG_LEVELS):
        lvl = lvl + jnp.where(txs >= (1 << j), 1, 0)
    return upper8, lvl


def _hg_lower_bounds(lbp, layer):
    e = jnp.exp(lbp - jnp.max(lbp, axis=0, keepdims=True))
    return jnp.sum(e[:layer + 1], axis=0) / jnp.sum(e, axis=0)


def _hg_gates(z, lbv):
    f = lbv + (1.0 - lbv) * jax.nn.sigmoid(z)
    return jnp.log2(f), 1.0 - f


def _hg_rep(slab, n_rows):
    return slab if n_rows == SUBLANES else jnp.concatenate([slab] * (n_rows // SUBLANES), axis=0)


def _hg_halves(j):
    h = 1 << j
    return [(slice(b, b + h), slice(b + h, b + 2 * h)) for b in range(0, HG_CHUNK, 2 * h)]


def _hg_sel8(upper8, j, a, b):
    shape3 = (HG_CHUNK // SUBLANES, SUBLANES, LANES)
    return jnp.where(upper8[j], a.reshape(shape3), b.reshape(shape3)).reshape(HG_CHUNK, LANES)


def _hg_upsweep(x, inclusive, upper8):
    c_len, sub = HG_CHUNK, SUBLANES
    x3 = x.reshape(c_len // sub, sub, LANES)
    pre3 = [x3 if inclusive else jnp.zeros_like(x3)]
    tot3 = [x3]
    for j in range(3):
        h = 1 << j
        t = tot3[-1]
        if 2 * h == sub:
            sib = pltpu.roll(t, h, 1)
        else:
            sib = jnp.where(upper8[j], pltpu.roll(t, h, 1), pltpu.roll(t, sub - h, 1))
        tot3.append(t + sib)
        pre3.append(pre3[-1] + jnp.where(upper8[j], sib, 0.0))
    pre = [p.reshape(c_len, LANES) for p in pre3]
    tot = [t.reshape(c_len, LANES) for t in tot3]
    slabs = {3: [tot[3][r:r + sub] for r in range(0, c_len, sub)]}
    for j in range(3, HG_LEVELS):
        h = 1 << j
        p, s = pre[-1], slabs[j]
        parts = []
        for i, (lo, up) in enumerate(_hg_halves(j)):
            parts += [p[lo], p[up] + _hg_rep(s[2 * i], h)]
        pre.append(jnp.concatenate(parts, axis=0))
        slabs[j + 1] = [s[2 * i] + s[2 * i + 1] for i in range(len(s) // 2)]
    return pre, tot, slabs


def _hg_chunk_terms(q, zf, zb, v, lb_f, lb_b, upper8, lvl):
    c_len, top = HG_CHUNK, HG_LEVELS
    vb = v.astype(BF16)
    lgf, kf = _hg_gates(zf, lb_f)
    lgb, kb = _hg_gates(zb, lb_b)
    pf, tf, sf = _hg_upsweep(lgf, True, upper8)
    xb, tb, sb = _hg_upsweep(lgb, False, upper8)
    a = lax.dot_general(q.astype(BF16), (kf + kb).astype(BF16), NT, preferred_element_type=F32)
    a = jnp.where(lvl == -1, a, 0.0)
    for j in range(HG_LEVELS):
        h = 1 << j
        if j < 3:
            eq = _hg_sel8(upper8, j, pf[j], tb[j] - xb[j])
            ek = _hg_sel8(upper8, j, xb[j], tf[j] - pf[j])
            ksel = _hg_sel8(upper8, j, kb, kf)
        else:
            eqs, eks, kss = [], [], []
            for i, (lo, up) in enumerate(_hg_halves(j)):
                eqs += [_hg_rep(sb[j][2 * i], h) - xb[j][lo], pf[j][up]]
                eks += [_hg_rep(sf[j][2 * i], h) - pf[j][lo], xb[j][up]]
                kss += [kf[lo], kb[up]]
            eq, ek, ksel = (jnp.concatenate(p, axis=0) for p in (eqs, eks, kss))
        qs = (q * jnp.exp2(eq)).astype(BF16)
        ks = (ksel * jnp.exp2(ek)).astype(BF16)
        aj = lax.dot_general(qs, ks, NT, preferred_element_type=F32)
        a = jnp.where(lvl == j, aj, a)
    o_intra = jnp.dot(a.astype(BF16), vb, preferred_element_type=F32)
    tot_f, tot_b = sf[top][0], sb[top][0]
    qhf = (q * jnp.exp2(pf[top])).astype(BF16)
    qhb = (q * jnp.exp2(_hg_rep(tot_b, c_len) - xb[top])).astype(BF16)
    khf = (kf * jnp.exp2(_hg_rep(tot_f, c_len) - pf[top])).astype(BF16)
    khb = (kb * jnp.exp2(xb[top])).astype(BF16)
    utf = lax.dot_general(vb, khf, TN, preferred_element_type=F32)
    utb = lax.dot_general(vb, khb, TN, preferred_element_type=F32)
    return o_intra, qhf, qhb, utf, utb, jnp.exp2(tot_f[0:1, :]), jnp.exp2(tot_b[0:1, :])


def _attn_hg_kernel(*refs, nseg, tk, layer, full_seq):
    it = iter(refs)
    q_ref = next(it)
    kv = [next(it) for _ in range(2 * nseg)]
    hq_ref, zf_ref, zb_ref, hv_ref, lbp_ref = (next(it) for _ in range(5))
    if full_seq:
        g_ref, ghg_ref = next(it), next(it)
        o_ref, ohg_ref, sf_ref, sb_ref = (next(it) for _ in range(4))
    else:
        o_ref, oi_ref, qhf_ref, qhb_ref, utf_ref, utb_ref, df_ref, db_ref = (next(it) for _ in range(8))
    s_ref = next(it)
    tq = q_ref.shape[1]
    heads = q_ref.shape[2] // LANES
    hg_heads = hq_ref.shape[2] // LANES
    lane = lax.broadcasted_iota(jnp.int32, (1, LANES), 1)
    chunks = [(kv[2 * seg], kv[2 * seg + 1], c * tk)
              for seg in range(nseg) for c in range(kv[2 * seg].shape[1] // tk)]

    upper8, lvl = _hg_consts()
    lb = _hg_lower_bounds(lbp_ref[...], layer)
    local = [(ci, h) for ci in range(tq // HG_CHUNK) for h in range(hg_heads)]
    per_head = len(local) // heads
    kept = {}

    def hg_local(ci, h):
        rs = slice(ci * HG_CHUNK, (ci + 1) * HG_CHUNK)
        cs = slice(h * LANES, (h + 1) * LANES)
        terms = _hg_chunk_terms(
            hq_ref[0, rs, cs], zf_ref[0, rs, cs], zb_ref[0, rs, cs], hv_ref[0, rs, cs],
            lb[0:1, cs], lb[1:2, cs], upper8, lvl)
        if full_seq:
            kept[(ci, h)] = terms
            return
        oi, qhf, qhb, utf, utb, dec_f, dec_b = terms
        oi_ref[0, rs, cs] = oi
        qhf_ref[0, rs, cs] = qhf
        qhb_ref[0, rs, cs] = qhb
        utf_ref[0, ci, h] = utf
        utb_ref[0, ci, h] = utb
        df_ref[0, ci, :, cs] = dec_f
        db_ref[0, ci, :, cs] = dec_b

    outs = []
    for h in range(heads):
        hs = slice(h * LANES, (h + 1) * LANES)
        qh = q_ref[0, :, hs]
        mx = jnp.full((tq, LANES), -jnp.inf, F32)
        for i, (k_ref, _, off) in enumerate(chunks):
            s = lax.dot_general(qh, k_ref[0, off:off + tk, hs], NT, preferred_element_type=F32)
            s_ref[:, i * tk:(i + 1) * tk] = s
            for w in range(tk // LANES):
                mx = jnp.maximum(mx, s[:, w * LANES:(w + 1) * LANES])
        m = jnp.broadcast_to(jnp.max(mx, axis=1, keepdims=True), (tq, LANES))
        mb = jnp.concatenate([m] * (tk // LANES), axis=1)
        acc = jnp.zeros((tq, LANES), F32)
        for i, (_, v_ref, off) in enumerate(chunks):
            p = jnp.exp2(s_ref[:, i * tk:(i + 1) * tk] - mb).astype(BF16)
            acc = acc + jnp.dot(p, v_ref[0, off:off + tk, hs], preferred_element_type=F32)
        outs.append(acc / pltpu.roll(acc, V_DIM, 1))
        for ci, hh in local[h * per_head:(h + 1) * per_head]:
            hg_local(ci, hh)
    for pair in range(heads // 2):
        a = outs[2 * pair]
        b = pltpu.roll(outs[2 * pair + 1], V_DIM, 1)
        o_ref[0, :, pair * LANES:(pair + 1) * LANES] = jnp.where(lane < V_DIM, a, b).astype(BF16)

    if full_seq:
        n = tq // HG_CHUNK
        for h in range(hg_heads):
            cs = slice(h * LANES, (h + 1) * LANES)
            enter_b, st = [None] * n, None
            for c in reversed(range(n)):
                enter_b[c] = st
                _, _, _, _, utb, _, dec_b = kept[(c, h)]
                st = utb if st is None else dec_b * st + utb
            sb_ref[0, h] = st.T
            st = None
            for c in range(n):
                rs = slice(c * HG_CHUNK, (c + 1) * HG_CHUNK)
                oi, qhf, qhb, utf, _, dec_f, _ = kept[(c, h)]
                if st is not None:
                    oi = oi + lax.dot_general(qhf, st.astype(BF16), NT, preferred_element_type=F32)
                if enter_b[c] is not None:
                    oi = oi + lax.dot_general(qhb, enter_b[c].astype(BF16), NT, preferred_element_type=F32)
                ohg_ref[0, rs, cs] = (_rms(oi, ghg_ref[:, cs]) * g_ref[0, rs, cs]).astype(BF16)
                st = utf if st is None else dec_f * st + utf
            sf_ref[0, h] = st.T


def _attn_hg_call(q, kvs, hgin, lb_param, g_hg, *, tq, tk, heads, layer, full_seq, name):
    b, t, _ = q.shape
    n_chunks = t // HG_CHUNK
    cpt = tq // HG_CHUNK
    groups = MLA_HEADS // heads
    hgh = HG_HEADS // groups
    assert heads % 2 == 0 and hgh >= 1 and (cpt * hgh) % heads == 0
    aw, gw = heads * LANES, hgh * LANES
    in_specs = [pl.BlockSpec((1, tq, aw), lambda i, p, j: (i, j, p))]
    args = [q]
    n_keys = 0
    for k, v in kvs:
        n_keys += k.shape[1]
        for a in (k, v):
            mode = dict(pipeline_mode=pl.Buffered(1)) if groups == 1 and t // tq > 1 else {}
            in_specs.append(pl.BlockSpec((1, a.shape[1], aw), lambda i, p, j: (i, 0, p), **mode))
            args.append(a)
    for g in range(4):
        in_specs.append(pl.BlockSpec((1, tq, gw), lambda i, p, j, g=g: (i, j, g * groups + p)))
        args.append(hgin)
    in_specs.append(pl.BlockSpec((lb_param.shape[0], 2, gw), lambda i, p, j: (0, 0, p)))
    args.append(lb_param)
    rows = lambda w: pl.BlockSpec((1, tq, w), lambda i, p, j: (i, j, p))
    o_shape = jax.ShapeDtypeStruct((b, t, MLA_HEADS * V_DIM), BF16)
    if full_seq:
        assert t == tq
        in_specs += [pl.BlockSpec((1, tq, gw), lambda i, p, j: (i, j, 4 * groups + p)),
                     pl.BlockSpec((1, gw), lambda i, p, j: (0, p))]
        args += [hgin, g_hg]
        st_spec = pl.BlockSpec((1, hgh, HG_DK, HG_DV), lambda i, p, j: (i, p, 0, 0))
        st_shape = jax.ShapeDtypeStruct((b, HG_HEADS, HG_DK, HG_DV), F32)
        out_specs = [rows(heads * V_DIM), rows(gw), st_spec, st_spec]
        out_shape = [o_shape, jax.ShapeDtypeStruct((b, t, HG_WIDTH), BF16), st_shape, st_shape]
    else:
        ut_spec = pl.BlockSpec((1, cpt, hgh, HG_DV, HG_DK), lambda i, p, j: (i, j, p, 0, 0))
        dec_spec = pl.BlockSpec((1, cpt, 1, gw), lambda i, p, j: (i, j, 0, p))
        ut_shape = jax.ShapeDtypeStruct((b, n_chunks, HG_HEADS, HG_DV, HG_DK), F32)
        dec_shape = jax.ShapeDtypeStruct((b, n_chunks, 1, HG_WIDTH), F32)
        out_specs = [rows(heads * V_DIM), rows(gw), rows(gw), rows(gw),
                     ut_spec, ut_spec, dec_spec, dec_spec]
        out_shape = [o_shape,
                     jax.ShapeDtypeStruct((b, t, HG_WIDTH), F32),
                     jax.ShapeDtypeStruct((b, t, HG_WIDTH), BF16),
                     jax.ShapeDtypeStruct((b, t, HG_WIDTH), BF16),
                     ut_shape, ut_shape, dec_shape, dec_shape]
    return pl.pallas_call(
        functools.partial(_attn_hg_kernel, nseg=len(kvs), tk=tk, layer=layer, full_seq=full_seq),
        grid=(b, groups, t // tq),
        in_specs=in_specs,
        out_specs=out_specs,
        out_shape=out_shape,
        scratch_shapes=[pltpu.VMEM((tq, n_keys), F32)],
        compiler_params=_params(("arbitrary", "arbitrary", "arbitrary")),
        name=name,
    )(*args)


def _hg_scan_kernel(*refs, has_s0):
    it = iter(refs)
    oi_ref, qhf_ref, qhb_ref, utf_ref, utb_ref, df_ref, db_ref, g_ref, ghg_ref = (next(it) for _ in range(9))
    if has_s0:
        s0f_ref, s0b_ref = next(it), next(it)
    o_ref, sf_ref, sb_ref, sbst_ref = (next(it) for _ in range(4))
    c_len = HG_CHUNK
    n_chunks = utf_ref.shape[1]
    unroll = min(HG_UNROLL, n_chunks)

    def rows_of(c):
        return pl.ds(pl.multiple_of(c * c_len, c_len), c_len)

    def bwd_body(i, st):
        c = n_chunks - 1 - i
        sbst_ref[c] = st.astype(BF16)
        return db_ref[0, c] * st + utb_ref[0, c, 0]

    st_b0 = s0b_ref[0, 0].T if has_s0 else jnp.zeros((HG_DV, HG_DK), F32)
    st_b = lax.fori_loop(0, n_chunks, bwd_body, st_b0, unroll=unroll)
    sb_ref[0, 0] = st_b.T

    ghg = ghg_ref[...]

    def fwd_body(c, st):
        qcat = jnp.concatenate([qhf_ref[0, rows_of(c), :], qhb_ref[0, rows_of(c), :]], axis=1)
        scat = jnp.concatenate([st.astype(BF16), sbst_ref[c]], axis=1)
        o = oi_ref[0, rows_of(c), :] + lax.dot_general(qcat, scat, NT, preferred_element_type=F32)
        on = _rms(o, ghg) * g_ref[0, rows_of(c), :]
        o_ref[0, rows_of(c), :] = on.astype(BF16)
        return df_ref[0, c] * st + utf_ref[0, c, 0]

    st_f0 = s0f_ref[0, 0].T if has_s0 else jnp.zeros((HG_DV, HG_DK), F32)
    st_f = lax.fori_loop(0, n_chunks, fwd_body, st_f0, unroll=unroll)
    sf_ref[0, 0] = st_f.T


def _hg_scan_call(oi, qhf, qhb, utf, utb, dec_f, dec_b, hgin, g_hg, s0, *, name):
    b, t, _ = oi.shape
    n_chunks = t // HG_CHUNK
    has_s0 = s0 is not None
    col = pl.BlockSpec((1, t, LANES), lambda i, h: (i, 0, h))
    ut_spec = pl.BlockSpec((1, n_chunks, 1, HG_DV, HG_DK), lambda i, h: (i, 0, h, 0, 0))
    dec_spec = pl.BlockSpec((1, n_chunks, 1, LANES), lambda i, h: (i, 0, 0, h))
    gate_col = pl.BlockSpec((1, t, LANES), lambda i, h: (i, 0, 4 * HG_HEADS + h))
    in_specs = [col, col, col, ut_spec, ut_spec, dec_spec, dec_spec, gate_col,
                pl.BlockSpec((1, LANES), lambda i, h: (0, h))]
    args = [oi, qhf, qhb, utf, utb, dec_f, dec_b, hgin, g_hg]
    st_spec = pl.BlockSpec((1, 1, HG_DK, HG_DV), lambda i, h: (i, h, 0, 0))
    if has_s0:
        in_specs += [st_spec, st_spec]
        args += list(s0)
    st_shape = jax.ShapeDtypeStruct((b, HG_HEADS, HG_DK, HG_DV), F32)
    return pl.pallas_call(
        functools.partial(_hg_scan_kernel, has_s0=has_s0),
        grid=(b, HG_HEADS),
        in_specs=in_specs,
        out_specs=[col, st_spec, st_spec],
        out_shape=[jax.ShapeDtypeStruct((b, t, HG_WIDTH), BF16), st_shape, st_shape],
        scratch_shapes=[pltpu.VMEM((n_chunks, HG_DV, HG_DK), BF16)],
        compiler_params=_params(("arbitrary", "arbitrary")),
        name=name,
    )(*args)


def _gelu_tanh(x):
    return 0.5 * x * (1.0 + jnp.tanh(0.7978845608028654 * (x + 0.044715 * (x * x * x))))


def _mixffn_kernel(om_ref, omp_ref, omn_ref, oh_ref, ohp_ref, ohn_ref, gt_ref, gtp_ref, gtn_ref,
                   x_ref, xp_ref, xn_ref, mod_ref, wmo_ref, who_ref, wout_ref, gffn_ref,
                   wi_ref, cw_ref, cb_ref, wo_ref, gfin_ref, y_ref, gsc_ref, u_ref, *, tm, seq_len):
    halo = BF16_SUBLANES
    t = pl.program_id(0)

    def ext(p_ref, m_ref, n_ref):
        return jnp.concatenate([p_ref[...], m_ref[...], n_ref[...]], axis=0)

    a = jnp.dot(ext(omp_ref, om_ref, omn_ref), wmo_ref[...], preferred_element_type=F32)
    b = jnp.dot(ext(ohp_ref, oh_ref, ohn_ref), who_ref[...], preferred_element_type=F32)
    gate = ext(gtp_ref, gt_ref, gtn_ref)
    mix = gate[:, 0:D_MODEL] * a + gate[:, D_MODEL:2 * D_MODEL] * b
    y = jnp.dot(mix.astype(BF16), wout_ref[...], preferred_element_type=F32)
    gt1 = mod_ref[0, 2:3, :]
    sh2 = mod_ref[0, 3:4, :]
    sc2 = mod_ref[0, 4:5, :]
    x1 = ext(xp_ref, x_ref, xn_ref) + gt1 * y
    h2 = (_rms(x1, gffn_ref[...]) * (1.0 + sc2) + sh2).astype(BF16)
    h2_own = h2[halo:halo + tm]

    pos = (lax.broadcasted_iota(jnp.int32, (tm, 1), 0) + t * tm) & (seq_len - 1)
    has_prev = pos != 0
    has_next = pos != seq_len - 1
    for j in range(D_FF // FF_CHUNK):
        cs = slice(j * FF_CHUNK, (j + 1) * FF_CHUNK)
        vs = slice(D_FF + j * FF_CHUNK, D_FF + (j + 1) * FF_CHUNK)
        gsc_ref[...] = jnp.dot(h2, wi_ref[:, cs], preferred_element_type=F32)
        g_prev = jnp.where(has_prev, gsc_ref[halo - 1:halo - 1 + tm, :], 0.0)
        g_cur = gsc_ref[halo:halo + tm, :]
        g_next = jnp.where(has_next, gsc_ref[halo + 1:halo + 1 + tm, :], 0.0)
        gconv = cb_ref[:, cs] + g_prev * cw_ref[0:1, cs] + g_cur * cw_ref[1:2, cs] + g_next * cw_ref[2:3, cs]
        val = jnp.dot(h2_own, wi_ref[:, vs], preferred_element_type=F32)
        u_ref[:, cs] = (_gelu_tanh(gconv) * val).astype(BF16)
    f = jnp.dot(u_ref[...], wo_ref[...], preferred_element_type=F32)
    gt2 = mod_ref[0, 5:6, :]
    y_ref[...] = _rms(x1[halo:halo + tm] + gt2 * f, gfin_ref[...])


def _mixffn_call(om, oh, gate, x, mod, pw, *, seq_len, mod_row0, per_seq_mod, tm, name):
    n_tok = x.shape[0]
    tps = max(seq_len // tm, 1)
    halo = BF16_SUBLANES
    n_hblk = n_tok // halo
    hpt = tm // halo
    if per_seq_mod:
        mod_map = lambda t: (mod_row0 + t // tps, 0, 0)
    else:
        mod_map = lambda t: (mod_row0, 0, 0)

    def with_halo(w):
        return [pl.BlockSpec((tm, w), lambda t: (t, 0)),
                pl.BlockSpec((halo, w), lambda t: (jnp.maximum(t * hpt - 1, 0), 0)),
                pl.BlockSpec((halo, w), lambda t: (jnp.minimum((t + 1) * hpt, n_hblk - 1), 0))]

    ws = [pw[k] for k in ("w_mla_o", "w_hg_o", "w_out", "g_ffn", "w_ffn_in", "conv_w", "conv_b",
                          "w_ffn_out", "g_final")]
    return pl.pallas_call(
        functools.partial(_mixffn_kernel, tm=tm, seq_len=seq_len),
        grid=(n_tok // tm,),
        in_specs=(with_halo(MLA_HEADS * V_DIM) + with_halo(HG_WIDTH) + with_halo(2 * D_MODEL)
                  + with_halo(D_MODEL) + [pl.BlockSpec((1, 6, D_MODEL), mod_map)]
                  + [_const_spec(w.shape) for w in ws]),
        out_specs=pl.BlockSpec((tm, D_MODEL), lambda t: (t, 0)),
        out_shape=jax.ShapeDtypeStruct((n_tok, D_MODEL), F32),
        scratch_shapes=[pltpu.VMEM((tm + 2 * halo, FF_CHUNK), F32),
                        pltpu.VMEM((tm, D_FF), BF16)],
        compiler_params=_params(("arbitrary",)),
        name=name,
    )(om, om, om, oh, oh, oh, gate, gate, gate, x, x, x, mod, *ws)


def _rope_tables(seq_len):
    n_freq = QK_ROPE // 4
    inv = jnp.power(ROPE_BASE, -jnp.arange(n_freq, dtype=F32) / n_freq)
    z8 = jnp.zeros((n_freq,), F32)
    z64 = jnp.zeros((QK_NOPE,), F32)
    inv_r = jnp.concatenate([z64] + [inv, z8] * 4)
    inv_c = jnp.concatenate([z64] + [z8, inv] * 4)
    pos = jnp.arange(seq_len, dtype=jnp.int32)
    r_pos = (pos // GRID_W).astype(F32)[:, None]
    c_pos = (pos % GRID_W).astype(F32)[:, None]
    ang = r_pos * inv_r + c_pos * inv_c
    cos, sin = jnp.cos(ang), jnp.sin(ang)
    s = ATTN_SCALE
    half = QK_ROPE // 2
    o16, z16 = jnp.ones((half,), F32), jnp.zeros((half,), F32)
    c_cs = jnp.concatenate([jnp.full((QK_NOPE + QK_ROPE,), s, F32), o16, o16])
    c_sa = jnp.concatenate([z64, -s * o16, z16, -o16, z16])
    c_sb = jnp.concatenate([z64, z16, s * o16, z16, o16])
    return jnp.stack([cos * c_cs, sin * c_sa, sin * c_sb], axis=0)


def _prep_weights(l, g_mix, w_in, g_q, w_uq, g_kv, w_ukv, w_mla_o, g_hg, w_hg_o, w_out, g_ffn,
                  w_ffn_in, conv_w, conv_b, w_ffn_out, g_final):
    d = D_MODEL
    w_in_t = jnp.swapaxes(w_in[l], 0, 1).astype(BF16)
    hd = QK_NOPE + QK_ROPE
    w_uq_p = jnp.concatenate([w_uq[l].reshape(Q_LORA, MLA_HEADS, hd),
                              jnp.zeros((Q_LORA, MLA_HEADS, LANES - hd), F32)], axis=-1)
    wkv = w_ukv[l].reshape(KV_LORA, MLA_HEADS, QK_NOPE + V_DIM)
    zpad = jnp.zeros((KV_LORA, MLA_HEADS, LANES - QK_NOPE), F32)
    w_k = jnp.concatenate([wkv[..., :QK_NOPE], zpad], axis=-1).reshape(KV_LORA, MLA_HEADS * LANES)
    w_v =jnp.concatenate([wkv[..., QK_NOPE:], jnp.zeros((KV_LORA, MLA_HEADS, LANES - V_DIM), F32)],
                          axis=-1).reshape(KV_LORA, MLA_HEADS * LANES)
    return {
        "g_mix": g_mix[l].reshape(1, d), "w_in": w_in_t,
        "g_q": g_q[l].reshape(1, Q_LORA), "w_uq": w_uq_p.reshape(Q_LORA, MLA_HEADS * LANES).astype(BF16),
        "g_kv": g_kv[l].reshape(1, KV_LORA), "w_k": w_k.astype(BF16), "w_v": w_v.astype(BF16),
        "w_mla_o": w_mla_o[l].astype(BF16), "w_hg_o": w_hg_o[l].astype(BF16),
        "w_out": w_out[l].astype(BF16), "g_ffn": g_ffn[l].reshape(1, d),
        "w_ffn_in": w_ffn_in[l].astype(BF16),
        "conv_w": conv_w[l], "conv_b": conv_b[l].reshape(1, D_FF),
        "w_ffn_out": w_ffn_out[l].astype(BF16), "g_final": g_final.reshape(1, d),
        "g_hg": g_hg[l].reshape(1, HG_WIDTH),
    }


def _trunk(x, mod, tabs, cache, pw, lb_param, *, layer, batch, seq_len, mod_row0, per_seq_mod,
           emit_cache, tm, tq, tag):
    n_tok = batch * seq_len
    hw = MLA_HEADS * LANES
    outs = _proj_call(x, mod, tabs, pw, seq_len=seq_len, mod_row0=mod_row0, per_seq_mod=per_seq_mod,
                      emit_cache=emit_cache, tm=tm)
    q, k, v, hgin, gate = outs[:5]
    hgin = hgin.reshape(batch, seq_len, 5 * HG_WIDTH)
    kvs = [(k.reshape(batch, seq_len, hw), v.reshape(batch, seq_len, hw))]
    s0 = None
    if cache is not None:
        ckv_c, kr_c, s0f, s0b = cache
        past = ckv_c.shape[1]
        kc, vc = _cachekv_call(ckv_c.reshape(batch * past, KV_LORA), kr_c.reshape(batch * past, QK_ROPE),
                               pw, tm=min(512, batch * past))
        kvs.append((kc.reshape(batch, past, hw), vc.reshape(batch, past, hw)))
        s0 = (s0f, s0b)
    full_seq = seq_len == tq and s0 is None
    o_mla, *hg_out = _attn_hg_call(q.reshape(batch, seq_len, hw), kvs, hgin, lb_param, pw["g_hg"], tq=tq,
                                   tk=min(512, seq_len),
                                   heads=MLA_HEADS if seq_len == tq else ATTN_HEADS_PER_STEP,
                                   layer=layer, full_seq=full_seq, name="attn_hg_" + tag)
    if full_seq:
        o_hg, s_f, s_b = hg_out
    else:
        o_hg, s_f, s_b = _hg_scan_call(*hg_out, hgin, pw["g_hg"], s0, name="hg_scan_" + tag)
    common = dict(seq_len=seq_len, mod_row0=mod_row0, per_seq_mod=per_seq_mod, tm=tm)
    y = _mixffn_call(o_mla.reshape(n_tok, MLA_HEADS * V_DIM), o_hg.reshape(n_tok, HG_WIDTH), gate, x,
                     mod, pw, name="mixffn_" + tag, **common)
    extra = outs[5:] if emit_cache else None
    return y, extra, s_f, s_b


def kernel(x_prompt, x_sample, cache_ckv, cache_krope, state_hgrn_fwd, state_hgrn_bwd, c, c_ctx, w_mod,
           b_mod, g_mix, w_in, g_q, w_uq, g_kv, w_ukv, w_mla_o, lb_param, g_hg, w_hg_o, w_out, g_ffn,
           w_ffn_in, conv_w, conv_b, w_ffn_out, g_final):
    batch, seq, d = x_prompt.shape
    dec_batch, dec_seq, _ = x_sample.shape
    depth = w_in.shape[0]
    assert depth == 1, "single trunk layer"
    l = 0
    cvec = jnp.concatenate([c_ctx[None, :], c, jnp.zeros((8 - 1 - dec_batch, d), F32)], axis=0)
    mod = _mod_call(cvec, w_mod[l], b_mod[l]).reshape(8, 6, d)
    pw = _prep_weights(l, g_mix, w_in, g_q, w_uq, g_kv, w_ukv, w_mla_o, g_hg, w_hg_o, w_out, g_ffn,
                       w_ffn_in, conv_w, conv_b, w_ffn_out, g_final)
    tabs = _rope_tables(dec_seq)

    y_ctx, cache_out, s_f, s_b = _trunk(
        x_prompt.reshape(batch * seq, d), mod, None, None, pw, lb_param, layer=l, batch=batch,
        seq_len=seq, mod_row0=0, per_seq_mod=False, emit_cache=True, tm=TOKEN_TILE, tq=min(Q_TILE, seq),
        tag="ctx")
    cache_l = (cache_ckv[:, l], cache_krope[:, l], state_hgrn_fwd[:, l], state_hgrn_bwd[:, l])
    y_lat, _, _, _ = _trunk(
        x_sample.reshape(dec_batch * dec_seq, d), mod, tabs, cache_l, pw, lb_param, layer=l,
        batch=dec_batch, seq_len=dec_seq, mod_row0=1, per_seq_mod=True, emit_cache=False, tm=TOKEN_TILE,
        tq=min(Q_TILE, dec_seq), tag="lat")

    ckv_new, kr_new = cache_out
    return (y_ctx.reshape(batch, seq, d), y_lat.reshape(dec_batch, dec_seq, d),
            ckv_new.reshape(batch, 1, seq, KV_LORA), kr_new.reshape(batch, 1, seq, QK_ROPE),
            s_f.reshape(batch, 1, HG_HEADS, HG_DK, HG_DV), s_b.reshape(batch, 1, HG_HEADS, HG_DK, HG_DV))
```

```python
import functools

import jax
import jax.numpy as jnp
from jax import lax
from jax.experimental import pallas as pl
from jax.experimental.pallas import tpu as pltpu

F32 = jnp.float32
BF16 = jnp.bfloat16

D_MODEL = 1024
GRID_W = 64
MLA_HEADS = 8
QK_NOPE = 64
QK_ROPE = 32
V_DIM = 64
Q_LORA = 384
KV_LORA = 256
ROPE_BASE = 10000.0
HG_HEADS = 4
HG_DK = 128
HG_DV = 128
HG_WIDTH = HG_HEADS * HG_DV
D_FF = 2816
EPS = 1e-6

LANES = 128
SUBLANES = 8
BF16_SUBLANES = 16
HG_CHUNK = 128
HG_LEVELS = 7
HG_PAIR = 2 * LANES
HG_UNROLL = 4
FF_CHUNK = 256
TOKEN_TILE = 512
Q_TILE = 512
ATTN_HEADS_PER_STEP = 8
VMEM_LIMIT = 56 * 1024 * 1024

C_PQ = 0
C_PKV = Q_LORA
C_PKR = C_PKV + KV_LORA
C_HQ = C_PKR + QK_ROPE
C_ZF = C_HQ + HG_WIDTH
C_HG = C_ZF + 3 * HG_WIDTH
C_GA = C_HG + HG_WIDTH
C_END = C_GA + 2 * D_MODEL
KR_LANE0 = LANES - QK_ROPE

ATTN_SCALE = (QK_NOPE + QK_ROPE) ** -0.5 * 1.4426950408889634

NT = (((1,), (1,)), ((), ()))
TN = (((0,), (0,)), ((), ()))


def _rms(x, g):
    return x * lax.rsqrt(jnp.mean(x * x, axis=-1, keepdims=True) + EPS) * g


def _const_spec(shape):
    n = len(shape)
    return pl.BlockSpec(shape, lambda *_: (0,) * n, pipeline_mode=pl.Buffered(1))


def _params(sem):
    return pltpu.CompilerParams(dimension_semantics=sem, vmem_limit_bytes=VMEM_LIMIT)


def _mod_kernel(c_ref, w_ref, b_ref, o_ref):
    c = c_ref[...]
    a = c * jax.nn.sigmoid(c)
    o_ref[...] = jnp.dot(a, w_ref[...], precision=lax.Precision.HIGHEST,
                         preferred_element_type=F32) + b_ref[...]


def _mod_call(cvec, w_mod, b_mod):
    n = w_mod.shape[1]
    tn = 1024
    return pl.pallas_call(
        _mod_kernel,
        grid=(n // tn,),
        in_specs=[pl.BlockSpec((8, D_MODEL), lambda j: (0, 0)),
                  pl.BlockSpec((D_MODEL, tn), lambda j: (0, j)),
                  pl.BlockSpec((1, tn), lambda j: (0, j))],
        out_specs=pl.BlockSpec((8, tn), lambda j: (0, j)),
        out_shape=jax.ShapeDtypeStruct((8, n), F32),
        compiler_params=_params(("arbitrary",)),
        name="mod",
    )(cvec, w_mod, b_mod.reshape(1, n))


def _place_rope_key(k_nope, kr):
    lane = lax.broadcasted_iota(jnp.int32, (1, LANES), 1)
    kr = jnp.where(lane >= QK_NOPE, jnp.where(lane < QK_NOPE + QK_ROPE, kr, 0.0), 0.0)
    return k_nope + jnp.concatenate([kr] * MLA_HEADS, axis=1)


def _proj_kernel(*refs, rope, emit_cache):
    it = iter(refs)
    x_ref = next(it)
    mod_ref = next(it)
    tab_ref = next(it) if rope else None
    gmix_ref, win_ref, gq_ref, wuq_ref, gkv_ref, wk_ref, wv_ref = (next(it) for _ in range(7))
    q_ref, k_ref, v_ref, hg_ref, gate_ref = (next(it) for _ in range(5))
    if emit_cache:
        ckv_ref, kr_ref = next(it), next(it)

    sh1 = mod_ref[0, 0:1, :]
    sc1 = mod_ref[0, 1:2, :]
    hb = (_rms(x_ref[...], gmix_ref[...]) * (1.0 + sc1) + sh1).astype(BF16)

    def proj(c0, c1):
        return lax.dot_general(hb, win_ref[c0:c1, :], NT, preferred_element_type=F32)

    if rope:
        cs1, sa1, sb1 = tab_ref[0], tab_ref[1], tab_ref[2]

        def rot(x, reps):
            w = x.shape[1]
            cs = jnp.concatenate([cs1] * reps, axis=1) if reps > 1 else cs1
            sa = jnp.concatenate([sa1] * reps, axis=1) if reps > 1 else sa1
            sb = jnp.concatenate([sb1] * reps, axis=1) if reps > 1 else sb1
            half = QK_ROPE // 2
            return x * cs + pltpu.roll(x, w - half, 1) * sa + pltpu.roll(x, half, 1) * sb

    qn = _rms(proj(C_PQ, C_PKV), gq_ref[...]).astype(BF16)
    q = jnp.dot(qn, wuq_ref[...], preferred_element_type=F32)
    q = rot(q, MLA_HEADS) if rope else q * ATTN_SCALE
    q_ref[...] = q.astype(BF16)

    ckv = _rms(proj(C_PKV, C_PKR), gkv_ref[...])
    pkr = proj(C_HQ - LANES, C_HQ)
    if emit_cache:
        ckv_ref[...] = ckv
        kr_ref[...] = pkr[:, KR_LANE0:]
    ckv_b = ckv.astype(BF16)
    kr = rot(pkr, 1) if rope else pkr
    k_ref[...] = _place_rope_key(jnp.dot(ckv_b, wk_ref[...], preferred_element_type=F32),
                                 pltpu.roll(kr, LANES - QK_ROPE, 1)).astype(BF16)
    lane = lax.broadcasted_iota(jnp.int32, (1, MLA_HEADS * LANES), 1)
    ones = jnp.where((lane & (LANES - 1)) >= V_DIM, 1.0, 0.0)
    v_ref[...] = (jnp.dot(ckv_b, wv_ref[...], preferred_element_type=F32) + ones).astype(BF16)

    gate_ref[...] = jax.nn.sigmoid(proj(C_GA, C_END)).astype(BF16)

    hq = proj(C_HQ, C_ZF)
    hg_ref[:, 0:HG_WIDTH] = hq * jax.nn.sigmoid(hq)
    og = proj(C_HG, C_GA)
    hg_ref[:, 4 * HG_WIDTH:5 * HG_WIDTH] = og * jax.nn.sigmoid(og)
    hg_ref[:, HG_WIDTH:4 * HG_WIDTH] = proj(C_ZF, C_HG)


def _proj_call(x, mod, tabs, pw, *, seq_len, mod_row0, per_seq_mod, emit_cache, tm):
    n_tok = x.shape[0]
    tps = max(seq_len // tm, 1)
    rope = tabs is not None
    if per_seq_mod:
        mod_map = lambda t: (mod_row0 + t // tps, 0, 0)
    else:
        mod_map = lambda t: (mod_row0, 0, 0)
    row = lambda w: pl.BlockSpec((tm, w), lambda t: (t, 0))
    in_specs = [row(D_MODEL), pl.BlockSpec((1, 6, D_MODEL), mod_map)]
    args = [x, mod]
    if rope:
        in_specs.append(pl.BlockSpec((3, tm, LANES), lambda t: (0, t % tps, 0)))
        args.append(tabs)
    for name in ("g_mix", "w_in", "g_q", "w_uq", "g_kv", "w_k", "w_v"):
        in_specs.append(_const_spec(pw[name].shape))
        args.append(pw[name])
    hw = MLA_HEADS * LANES
    out_shape = [jax.ShapeDtypeStruct((n_tok, hw), BF16),
                 jax.ShapeDtypeStruct((n_tok, hw), BF16),
                 jax.ShapeDtypeStruct((n_tok, hw), BF16),
                 jax.ShapeDtypeStruct((n_tok, 5 * HG_WIDTH), F32),
                 jax.ShapeDtypeStruct((n_tok, 2 * D_MODEL), BF16)]
    out_specs = [row(hw), row(hw), row(hw), row(5 * HG_WIDTH), row(2 * D_MODEL)]
    if emit_cache:
        out_shape += [jax.ShapeDtypeStruct((n_tok, KV_LORA), F32),
                      jax.ShapeDtypeStruct((n_tok, QK_ROPE), F32)]
        out_specs += [row(KV_LORA), row(QK_ROPE)]
    return pl.pallas_call(
        functools.partial(_proj_kernel, rope=rope, emit_cache=emit_cache),
        grid=(n_tok // tm,),
        in_specs=in_specs,
        out_specs=out_specs,
        out_shape=out_shape,
        compiler_params=_params(("arbitrary",)),
        name="proj_lat" if rope else "proj_ctx",
    )(*args)


def _cachekv_kernel(ckv_ref, kr_ref, wk_ref, wv_ref, k_ref, v_ref):
    ckv_b = ckv_ref[...].astype(BF16)
    tm = ckv_b.shape[0]
    kr = jnp.concatenate([jnp.zeros((tm, QK_NOPE), F32), kr_ref[...],
                          jnp.zeros((tm, LANES - QK_NOPE - QK_ROPE), F32)], axis=1)
    k_ref[...] = _place_rope_key(jnp.dot(ckv_b, wk_ref[...], preferred_element_type=F32), kr).astype(BF16)
    lane = lax.broadcasted_iota(jnp.int32, (1, MLA_HEADS * LANES), 1)
    ones = jnp.where((lane & (LANES - 1)) >= V_DIM, 1.0, 0.0)
    v_ref[...] = (jnp.dot(ckv_b, wv_ref[...], preferred_element_type=F32) + ones).astype(BF16)


def _cachekv_call(ckv, kr, pw, tm):
    n_tok = ckv.shape[0]
    hw = MLA_HEADS * LANES
    row = lambda w: pl.BlockSpec((tm, w), lambda t: (t, 0))
    return pl.pallas_call(
        _cachekv_kernel,
        grid=(n_tok // tm,),
        in_specs=[row(KV_LORA), row(QK_ROPE), _const_spec(pw["w_k"].shape), _const_spec(pw["w_v"].shape)],
        out_specs=[row(hw), row(hw)],
        out_shape=[jax.ShapeDtypeStruct((n_tok, hw), BF16)] * 2,
        compiler_params=_params(("arbitrary",)),
        name="cache_kv",
    )(ckv, kr, pw["w_k"], pw["w_v"])


def _hg_consts():
    c_len, sub = HG_CHUNK, SUBLANES
    row8 = lax.broadcasted_iota(jnp.int32, (1, sub, HG_PAIR), 1)
    upper8 = [(row8 & (1 << j)) != 0 for j in range(3)]
    ti = lax.broadcasted_iota(jnp.int32, (c_len, c_len), 0)
    si = lax.broadcasted_iota(jnp.int32, (c_len, c_len), 1)
    txs = ti ^ si
    lvl = jnp.full((c_len, c_len), -1, jnp.int32)
    for j in range(HG_LEVELS):
        lvl = lvl + jnp.where(txs >= (1 << j), 1, 0)
    return upper8, jnp.concatenate([lvl, lvl], axis=1)


def _hg_lower_bounds(lbp, layer):
    e = jnp.exp(lbp - jnp.max(lbp, axis=0, keepdims=True))
    return jnp.sum(e[:layer + 1], axis=0) / jnp.sum(e, axis=0)


def _hg_gates(z, lbv):
    f = lbv + (1.0 - lbv) * jax.nn.sigmoid(z)
    return jnp.log2(f), 1.0 - f


def _hg_rep(slab, n_rows):
    return slab if n_rows == SUBLANES else jnp.concatenate([slab] * (n_rows // SUBLANES), axis=0)


def _hg_halves(j):
    h = 1 << j
    return [(slice(b, b + h), slice(b + h, b + 2 * h)) for b in range(0, HG_CHUNK, 2 * h)]


def _hg_sel8(upper8, j, a, b):
    shape3 = (HG_CHUNK // SUBLANES, SUBLANES, a.shape[1])
    return jnp.where(upper8[j], a.reshape(shape3), b.reshape(shape3)).reshape(a.shape)


def _hg_blockdiag(x):
    z = jnp.zeros((x.shape[0], LANES), x.dtype)
    return jnp.concatenate([jnp.concatenate([x[:, :LANES], z], axis=1),
                            jnp.concatenate([z, x[:, LANES:]], axis=1)], axis=0)


def _hg_upsweep(x, inclusive, upper8):
    c_len, sub, width = HG_CHUNK, SUBLANES, x.shape[1]
    x3 = x.reshape(c_len // sub, sub, width)
    pre3 = [x3 if inclusive else jnp.zeros_like(x3)]
    tot3 = [x3]
    for j in range(3):
        h = 1 << j
        t = tot3[-1]
        if 2 * h == sub:
            sib = pltpu.roll(t, h, 1)
        else:
            sib = jnp.where(upper8[j], pltpu.roll(t, h, 1), pltpu.roll(t, sub - h, 1))
        tot3.append(t + sib)
        pre3.append(pre3[-1] + jnp.where(upper8[j], sib, 0.0))
    pre = [p.reshape(c_len, width) for p in pre3]
    tot = [t.reshape(c_len, width) for t in tot3]
    slabs = {3: [tot[3][r:r + sub] for r in range(0, c_len, sub)]}
    for j in range(3, HG_LEVELS):
        h = 1 << j
        p, s = pre[-1], slabs[j]
        parts = []
        for i, (lo, up) in enumerate(_hg_halves(j)):
            parts += [p[lo], p[up] + _hg_rep(s[2 * i], h)]
        pre.append(jnp.concatenate(parts, axis=0))
        slabs[j + 1] = [s[2 * i] + s[2 * i + 1] for i in range(len(s) // 2)]
    return pre, tot, slabs


def _hg_chunk_terms(q, zf, zb, v, lb_f, lb_b, upper8, lvl):
    c_len, top = HG_CHUNK, HG_LEVELS
    vb = v.astype(BF16)
    lgf, kf = _hg_gates(zf, lb_f)
    lgb, kb = _hg_gates(zb, lb_b)
    pf, tf, sf = _hg_upsweep(lgf, True, upper8)
    xb, tb, sb = _hg_upsweep(lgb, False, upper8)
    a = lax.dot_general(q.astype(BF16), _hg_blockdiag((kf + kb).astype(BF16)), NT,
                        preferred_element_type=F32)
    a = jnp.where(lvl == -1, a, 0.0)
    for j in range(HG_LEVELS):
        h = 1 << j
        if j < 3:
            eq = _hg_sel8(upper8, j, pf[j], tb[j] - xb[j])
            ek = _hg_sel8(upper8, j, xb[j], tf[j] - pf[j])
            ksel = _hg_sel8(upper8, j, kb, kf)
        else:
            eqs, eks, kss = [], [], []
            for i, (lo, up) in enumerate(_hg_halves(j)):
                eqs += [_hg_rep(sb[j][2 * i], h) - xb[j][lo], pf[j][up]]
                eks += [_hg_rep(sf[j][2 * i], h) - pf[j][lo], xb[j][up]]
                kss += [kf[lo], kb[up]]
            eq, ek, ksel = (jnp.concatenate(p, axis=0) for p in (eqs, eks, kss))
        qs = (q * jnp.exp2(eq)).astype(BF16)
        ks = (ksel * jnp.exp2(ek)).astype(BF16)
        aj = lax.dot_general(qs, _hg_blockdiag(ks), NT, preferred_element_type=F32)
        a = jnp.where(lvl == j, aj, a)
    o_intra = jnp.dot(a.astype(BF16), _hg_blockdiag(vb), preferred_element_type=F32)
    tot_f, tot_b = sf[top][0], sb[top][0]
    qhf = (q * jnp.exp2(pf[top])).astype(BF16)
    qhb = (q * jnp.exp2(_hg_rep(tot_b, c_len) - xb[top])).astype(BF16)
    khf = (kf * jnp.exp2(_hg_rep(tot_f, c_len) - pf[top])).astype(BF16)
    khb = (kb * jnp.exp2(xb[top])).astype(BF16)
    uts = []
    for hh in range(2):
        hl = slice(hh * LANES, (hh + 1) * LANES)
        uts.append(lax.dot_general(vb[:, hl], jnp.concatenate([khf[:, hl], khb[:, hl]], axis=1), TN,
                                   preferred_element_type=F32))
    return o_intra, qhf, qhb, uts, jnp.exp2(tot_f[0:1, :]), jnp.exp2(tot_b[0:1, :])


def _attn_hg_kernel(*refs, nseg, tk, layer, full_seq):
    it = iter(refs)
    q_ref = next(it)
    kv = [next(it) for _ in range(2 * nseg)]
    hq_ref, zf_ref, zb_ref, hv_ref, lbp_ref = (next(it) for _ in range(5))
    if full_seq:
        g_ref, ghg_ref = next(it), next(it)
        o_ref, ohg_ref, sf_ref, sb_ref = (next(it) for _ in range(4))
    else:
        o_ref, oi_ref, qhf_ref, qhb_ref, utf_ref, utb_ref, df_ref, db_ref = (next(it) for _ in range(8))
    s_ref = next(it)
    tq = q_ref.shape[1]
    heads = q_ref.shape[2] // LANES
    hg_heads = hq_ref.shape[2] // LANES
    lane = lax.broadcasted_iota(jnp.int32, (1, LANES), 1)
    chunks = [(kv[2 * seg], kv[2 * seg + 1], c * tk)
              for seg in range(nseg) for c in range(kv[2 * seg].shape[1] // tk)]

    upper8, lvl = _hg_consts()
    lb = _hg_lower_bounds(lbp_ref[...], layer)
    local = [(ci, p) for ci in range(tq // HG_CHUNK) for p in range(hg_heads // 2)]
    kept = {}

    def hg_local(ci, p):
        rs = slice(ci * HG_CHUNK, (ci + 1) * HG_CHUNK)
        cs = slice(p * HG_PAIR, (p + 1) * HG_PAIR)
        oi, qhf, qhb, uts, dec_f, dec_b = _hg_chunk_terms(
            hq_ref[0, rs, cs], zf_ref[0, rs, cs], zb_ref[0, rs, cs], hv_ref[0, rs, cs],
            lb[0:1, cs], lb[1:2, cs], upper8, lvl)
        for hh in range(2):
            h = 2 * p + hh
            hl = slice(hh * LANES, (hh + 1) * LANES)
            utf, utb = uts[hh][:, :LANES], uts[hh][:, LANES:]
            if full_seq:
                kept[(ci, h)] = (oi[:, hl], qhf[:, hl], qhb[:, hl], utf, utb, dec_f[:, hl], dec_b[:, hl])
            else:
                utf_ref[0, ci, h] = utf
                utb_ref[0, ci, h] = utb
        if not full_seq:
            oi_ref[0, rs, cs] = oi
            qhf_ref[0, rs, cs] = qhf
            qhb_ref[0, rs, cs] = qhb
            df_ref[0, ci, :, cs] = dec_f
            db_ref[0, ci, :, cs] = dec_b

    outs = []
    for h in range(heads):
        hs = slice(h * LANES, (h + 1) * LANES)
        qh = q_ref[0, :, hs]
        mx = jnp.full((tq, LANES), -jnp.inf, F32)
        for i, (k_ref, _, off) in enumerate(chunks):
            s = lax.dot_general(qh, k_ref[0, off:off + tk, hs], NT, preferred_element_type=F32)
            s_ref[:, i * tk:(i + 1) * tk] = s
            for w in range(tk // LANES):
                mx = jnp.maximum(mx, s[:, w * LANES:(w + 1) * LANES])
        m = jnp.broadcast_to(jnp.max(mx, axis=1, keepdims=True), (tq, LANES))
        mb = jnp.concatenate([m] * (tk // LANES), axis=1)
        acc = jnp.zeros((tq, LANES), F32)
        for i, (_, v_ref, off) in enumerate(chunks):
            p = jnp.exp2(s_ref[:, i * tk:(i + 1) * tk] - mb).astype(BF16)
            acc = acc + jnp.dot(p, v_ref[0, off:off + tk, hs], preferred_element_type=F32)
        outs.append(acc / pltpu.roll(acc, V_DIM, 1))
        for ci, p in local[h * len(local) // heads:(h + 1) * len(local) // heads]:
            hg_local(ci, p)
    for pair in range(heads // 2):
        a = outs[2 * pair]
        b = pltpu.roll(outs[2 * pair + 1], V_DIM, 1)
        o_ref[0, :, pair * LANES:(pair + 1) * LANES] = jnp.where(lane < V_DIM, a, b).astype(BF16)

    if full_seq:
        n = tq // HG_CHUNK
        for h in range(hg_heads):
            cs = slice(h * LANES, (h + 1) * LANES)
            enter_b, st = [None] * n, None
            for c in reversed(range(n)):
                enter_b[c] = st
                _, _, _, _, utb, _, dec_b = kept[(c, h)]
                st = utb if st is None else dec_b * st + utb
            sb_ref[0, h] = st.T
            st = None
            for c in range(n):
                rs = slice(c * HG_CHUNK, (c + 1) * HG_CHUNK)
                oi, qhf, qhb, utf, _, dec_f, _ = kept[(c, h)]
                if st is not None:
                    oi = oi + lax.dot_general(qhf, st.astype(BF16), NT, preferred_element_type=F32)
                if enter_b[c] is not None:
                    oi = oi + lax.dot_general(qhb, enter_b[c].astype(BF16), NT, preferred_element_type=F32)
                ohg_ref[0, rs, cs] = (_rms(oi, ghg_ref[:, cs]) * g_ref[0, rs, cs]).astype(BF16)
                st = utf if st is None else dec_f * st + utf
            sf_ref[0, h] = st.T


def _attn_hg_call(q, kvs, hgin, lb_param, g_hg, *, tq, tk, heads, layer, full_seq, name):
    b, t, _ = q.shape
    n_chunks = t // HG_CHUNK
    cpt = tq // HG_CHUNK
    groups = MLA_HEADS // heads
    hgh = HG_HEADS // groups
    assert heads % 2 == 0 and hgh % 2 == 0
    aw, gw = heads * LANES, hgh * LANES
    in_specs = [pl.BlockSpec((1, tq, aw), lambda i, p, j: (i, j, p))]
    args = [q]
    n_keys = 0
    for k, v in kvs:
        n_keys += k.shape[1]
        for a in (k, v):
            mode = dict(pipeline_mode=pl.Buffered(1)) if groups == 1 and t // tq > 1 else {}
            in_specs.append(pl.BlockSpec((1, a.shape[1], aw), lambda i, p, j: (i, 0, p), **mode))
            args.append(a)
    for g in range(4):
        in_specs.append(pl.BlockSpec((1, tq, gw), lambda i, p, j, g=g: (i, j, g * groups + p)))
        args.append(hgin)
    in_specs.append(pl.BlockSpec((lb_param.shape[0], 2, gw), lambda i, p, j: (0, 0, p)))
    args.append(lb_param)
    rows = lambda w: pl.BlockSpec((1, tq, w), lambda i, p, j: (i, j, p))
    o_shape = jax.ShapeDtypeStruct((b, t, MLA_HEADS * V_DIM), BF16)
    if full_seq:
        assert t == tq
        in_specs += [pl.BlockSpec((1, tq, gw), lambda i, p, j: (i, j, 4 * groups + p)),
                     pl.BlockSpec((1, gw), lambda i, p, j: (0, p))]
        args += [hgin, g_hg]
        st_spec = pl.BlockSpec((1, hgh, HG_DK, HG_DV), lambda i, p, j: (i, p, 0, 0))
        st_shape = jax.ShapeDtypeStruct((b, HG_HEADS, HG_DK, HG_DV), F32)
        out_specs = [rows(heads * V_DIM), rows(gw), st_spec, st_spec]
        out_shape = [o_shape, jax.ShapeDtypeStruct((b, t, HG_WIDTH), BF16), st_shape, st_shape]
    else:
        ut_spec = pl.BlockSpec((1, cpt, hgh, HG_DV, HG_DK), lambda i, p, j: (i, j, p, 0, 0))
        dec_spec = pl.BlockSpec((1, cpt, 1, gw), lambda i, p, j: (i, j, 0, p))
        ut_shape = jax.ShapeDtypeStruct((b, n_chunks, HG_HEADS, HG_DV, HG_DK), F32)
        dec_shape = jax.ShapeDtypeStruct((b, n_chunks, 1, HG_WIDTH), F32)
        out_specs = [rows(heads * V_DIM), rows(gw), rows(gw), rows(gw),
                     ut_spec, ut_spec, dec_spec, dec_spec]
        out_shape = [o_shape,
                     jax.ShapeDtypeStruct((b, t, HG_WIDTH), F32),
                     jax.ShapeDtypeStruct((b, t, HG_WIDTH), BF16),
                     jax.ShapeDtypeStruct((b, t, HG_WIDTH), BF16),
                     ut_shape, ut_shape, dec_shape, dec_shape]
    return pl.pallas_call(
        functools.partial(_attn_hg_kernel, nseg=len(kvs), tk=tk, layer=layer, full_seq=full_seq),
        grid=(b, groups, t // tq),
        in_specs=in_specs,
        out_specs=out_specs,
        out_shape=out_shape,
        scratch_shapes=[pltpu.VMEM((tq, n_keys), F32)],
        compiler_params=_params(("arbitrary", "arbitrary", "arbitrary")),
        name=name,
    )(*args)


def _hg_scan_kernel(*refs, has_s0):
    it = iter(refs)
    oi_ref, qhf_ref, qhb_ref, utf_ref, utb_ref, df_ref, db_ref, g_ref, ghg_ref = (next(it) for _ in range(9))
    if has_s0:
        s0f_ref, s0b_ref = next(it), next(it)
    o_ref, sf_ref, sb_ref, sbst_ref = (next(it) for _ in range(4))
    c_len = HG_CHUNK
    n_chunks = utf_ref.shape[1]
    unroll = min(HG_UNROLL, n_chunks)

    def rows_of(c):
        return pl.ds(pl.multiple_of(c * c_len, c_len), c_len)

    def bwd_body(i, st):
        c = n_chunks - 1 - i
        sbst_ref[c] = st.astype(BF16)
        return db_ref[0, c] * st + utb_ref[0, c, 0]

    st_b0 = s0b_ref[0, 0].T if has_s0 else jnp.zeros((HG_DV, HG_DK), F32)
    st_b = lax.fori_loop(0, n_chunks, bwd_body, st_b0, unroll=unroll)
    sb_ref[0, 0] = st_b.T

    ghg = ghg_ref[...]

    def fwd_body(c, st):
        qcat = jnp.concatenate([qhf_ref[0, rows_of(c), :], qhb_ref[0, rows_of(c), :]], axis=1)
        scat = jnp.concatenate([st.astype(BF16), sbst_ref[c]], axis=1)
        o = oi_ref[0, rows_of(c), :] + lax.dot_general(qcat, scat, NT, preferred_element_type=F32)
        on = _rms(o, ghg) * g_ref[0, rows_of(c), :]
        o_ref[0, rows_of(c), :] = on.astype(BF16)
        return df_ref[0, c] * st + utf_ref[0, c, 0]

    st_f0 = s0f_ref[0, 0].T if has_s0 else jnp.zeros((HG_DV, HG_DK), F32)
    st_f = lax.fori_loop(0, n_chunks, fwd_body, st_f0, unroll=unroll)
    sf_ref[0, 0] = st_f.T


def _hg_scan_call(oi, qhf, qhb, utf, utb, dec_f, dec_b, hgin, g_hg, s0, *, name):
    b, t, _ = oi.shape
    n_chunks = t // HG_CHUNK
    has_s0 = s0 is not None
    col = pl.BlockSpec((1, t, LANES), lambda i, h: (i, 0, h))
    ut_spec = pl.BlockSpec((1, n_chunks, 1, HG_DV, HG_DK), lambda i, h: (i, 0, h, 0, 0))
    dec_spec = pl.BlockSpec((1, n_chunks, 1, LANES), lambda i, h: (i, 0, 0, h))
    gate_col = pl.BlockSpec((1, t, LANES), lambda i, h: (i, 0, 4 * HG_HEADS + h))
    in_specs = [col, col, col, ut_spec, ut_spec, dec_spec, dec_spec, gate_col,
                pl.BlockSpec((1, LANES), lambda i, h: (0, h))]
    args = [oi, qhf, qhb, utf, utb, dec_f, dec_b, hgin, g_hg]
    st_spec = pl.BlockSpec((1, 1, HG_DK, HG_DV), lambda i, h: (i, h, 0, 0))
    if has_s0:
        in_specs += [st_spec, st_spec]
        args += list(s0)
    st_shape = jax.ShapeDtypeStruct((b, HG_HEADS, HG_DK, HG_DV), F32)
    return pl.pallas_call(
        functools.partial(_hg_scan_kernel, has_s0=has_s0),
        grid=(b, HG_HEADS),
        in_specs=in_specs,
        out_specs=[col, st_spec, st_spec],
        out_shape=[jax.ShapeDtypeStruct((b, t, HG_WIDTH), BF16), st_shape, st_shape],
        scratch_shapes=[pltpu.VMEM((n_chunks, HG_DV, HG_DK), BF16)],
        compiler_params=_params(("arbitrary", "arbitrary")),
        name=name,
    )(*args)


def _gelu_tanh(x):
    return 0.5 * x * (1.0 + jnp.tanh(0.7978845608028654 * (x + 0.044715 * (x * x * x))))


def _mixffn_kernel(om_ref, omp_ref, omn_ref, oh_ref, ohp_ref, ohn_ref, gt_ref, gtp_ref, gtn_ref,
                   x_ref, xp_ref, xn_ref, mod_ref, wmo_ref, who_ref, wout_ref, gffn_ref,
                   wi_ref, cw_ref, cb_ref, wo_ref, gfin_ref, y_ref, gsc_ref, u_ref, *, tm, seq_len):
    halo = BF16_SUBLANES
    t = pl.program_id(0)

    def ext(p_ref, m_ref, n_ref):
        return jnp.concatenate([p_ref[...], m_ref[...], n_ref[...]], axis=0)

    a = jnp.dot(ext(omp_ref, om_ref, omn_ref), wmo_ref[...], preferred_element_type=F32)
    b = jnp.dot(ext(ohp_ref, oh_ref, ohn_ref), who_ref[...], preferred_element_type=F32)
    gate = ext(gtp_ref, gt_ref, gtn_ref)
    mix = gate[:, 0:D_MODEL] * a + gate[:, D_MODEL:2 * D_MODEL] * b
    y = jnp.dot(mix.astype(BF16), wout_ref[...], preferred_element_type=F32)
    gt1 = mod_ref[0, 2:3, :]
    sh2 = mod_ref[0, 3:4, :]
    sc2 = mod_ref[0, 4:5, :]
    x1 = ext(xp_ref, x_ref, xn_ref) + gt1 * y
    h2 = (_rms(x1, gffn_ref[...]) * (1.0 + sc2) + sh2).astype(BF16)
    h2_own = h2[halo:halo + tm]

    pos = (lax.broadcasted_iota(jnp.int32, (tm, 1), 0) + t * tm) & (seq_len - 1)
    has_prev = pos != 0
    has_next = pos != seq_len - 1
    for j in range(D_FF // FF_CHUNK):
        cs = slice(j * FF_CHUNK, (j + 1) * FF_CHUNK)
        vs = slice(D_FF + j * FF_CHUNK, D_FF + (j + 1) * FF_CHUNK)
        gsc_ref[...] = jnp.dot(h2, wi_ref[:, cs], preferred_element_type=F32)
        g_prev = jnp.where(has_prev, gsc_ref[halo - 1:halo - 1 + tm, :], 0.0)
        g_cur = gsc_ref[halo:halo + tm, :]
        g_next = jnp.where(has_next, gsc_ref[halo + 1:halo + 1 + tm, :], 0.0)
        gconv = cb_ref[:, cs] + g_prev * cw_ref[0:1, cs] + g_cur * cw_ref[1:2, cs] + g_next * cw_ref[2:3, cs]
        val = jnp.dot(h2_own, wi_ref[:, vs], preferred_element_type=F32)
        u_ref[:, cs] = (_gelu_tanh(gconv) * val).astype(BF16)
    f = jnp.dot(u_ref[...], wo_ref[...], preferred_element_type=F32)
    gt2 = mod_ref[0, 5:6, :]
    y_ref[...] = _rms(x1[halo:halo + tm] + gt2 * f, gfin_ref[...])


def _mixffn_call(om, oh, gate, x, mod, pw, *, seq_len, mod_row0, per_seq_mod, tm, name):
    n_tok = x.shape[0]
    tps = max(seq_len // tm, 1)
    halo = BF16_SUBLANES
    n_hblk = n_tok // halo
    hpt = tm // halo
    if per_seq_mod:
        mod_map = lambda t: (mod_row0 + t // tps, 0, 0)
    else:
        mod_map = lambda t: (mod_row0, 0, 0)

    def with_halo(w):
        return [pl.BlockSpec((tm, w), lambda t: (t, 0)),
                pl.BlockSpec((halo, w), lambda t: (jnp.maximum(t * hpt - 1, 0), 0)),
                pl.BlockSpec((halo, w), lambda t: (jnp.minimum((t + 1) * hpt, n_hblk - 1), 0))]

    ws = [pw[k] for k in ("w_mla_o", "w_hg_o", "w_out", "g_ffn", "w_ffn_in", "conv_w", "conv_b",
                          "w_ffn_out", "g_final")]
    return pl.pallas_call(
        functools.partial(_mixffn_kernel, tm=tm, seq_len=seq_len),
        grid=(n_tok // tm,),
        in_specs=(with_halo(MLA_HEADS * V_DIM) + with_halo(HG_WIDTH) + with_halo(2 * D_MODEL)
                  + with_halo(D_MODEL) + [pl.BlockSpec((1, 6, D_MODEL), mod_map)]
                  + [_const_spec(w.shape) for w in ws]),
        out_specs=pl.BlockSpec((tm, D_MODEL), lambda t: (t, 0)),
        out_shape=jax.ShapeDtypeStruct((n_tok, D_MODEL), F32),
        scratch_shapes=[pltpu.VMEM((tm + 2 * halo, FF_CHUNK), F32),
                        pltpu.VMEM((tm, D_FF), BF16)],
        compiler_params=_params(("arbitrary",)),
        name=name,
    )(om, om, om, oh, oh, oh, gate, gate, gate, x, x, x, mod, *ws)


def _rope_tables(seq_len):
    n_freq = QK_ROPE // 4
    inv = jnp.power(ROPE_BASE, -jnp.arange(n_freq, dtype=F32) / n_freq)
    z8 = jnp.zeros((n_freq,), F32)
    z64 = jnp.zeros((QK_NOPE,), F32)
    inv_r = jnp.concatenate([z64] + [inv, z8] * 4)
    inv_c = jnp.concatenate([z64] + [z8, inv] * 4)
    pos = jnp.arange(seq_len, dtype=jnp.int32)
    r_pos = (pos // GRID_W).astype(F32)[:, None]
    c_pos = (pos % GRID_W).astype(F32)[:, None]
    ang = r_pos * inv_r + c_pos * inv_c
    cos, sin = jnp.cos(ang), jnp.sin(ang)
    s = ATTN_SCALE
    half = QK_ROPE // 2
    o16, z16 = jnp.ones((half,), F32), jnp.zeros((half,), F32)
    c_cs = jnp.concatenate([jnp.full((QK_NOPE + QK_ROPE,), s, F32), o16, o16])
    c_sa = jnp.concatenate([z64, -s * o16, z16, -o16, z16])
    c_sb = jnp.concatenate([z64, z16, s * o16, z16, o16])
    return jnp.stack([cos * c_cs, sin * c_sa, sin * c_sb], axis=0)


def _prep_weights(l, g_mix, w_in, g_q, w_uq, g_kv, w_ukv, w_mla_o, g_hg, w_hg_o, w_out, g_ffn,
                  w_ffn_in, conv_w, conv_b, w_ffn_out, g_final):
    d = D_MODEL
    w_in_t = jnp.swapaxes(w_in[l], 0, 1).astype(BF16)
    hd = QK_NOPE + QK_ROPE
    w_uq_p = jnp.concatenate([w_uq[l].reshape(Q_LORA, MLA_HEADS, hd),
                              jnp.zeros((Q_LORA, MLA_HEADS, LANES - hd), F32)], axis=-1)
    wkv = w_ukv[l].reshape(KV_LORA, MLA_HEADS, QK_NOPE + V_DIM)
    zpad = jnp.zeros((KV_LORA, MLA_HEADS, LANES - QK_NOPE), F32)
    w_k = jnp.concatenate([wkv[..., :QK_NOPE], zpad], axis=-1).reshape(KV_LORA, MLA_HEADS * LANES)
    w_v =jnp.concatenate([wkv[..., QK_NOPE:], jnp.zeros((KV_LORA, MLA_HEADS, LANES - V_DIM), F32)],
                          axis=-1).reshape(KV_LORA, MLA_HEADS * LANES)
    return {
        "g_mix": g_mix[l].reshape(1, d), "w_in": w_in_t,
        "g_q": g_q[l].reshape(1, Q_LORA), "w_uq": w_uq_p.reshape(Q_LORA, MLA_HEADS * LANES).astype(BF16),
        "g_kv": g_kv[l].reshape(1, KV_LORA), "w_k": w_k.astype(BF16), "w_v": w_v.astype(BF16),
        "w_mla_o": w_mla_o[l].astype(BF16), "w_hg_o": w_hg_o[l].astype(BF16),
        "w_out": w_out[l].astype(BF16), "g_ffn": g_ffn[l].reshape(1, d),
        "w_ffn_in": w_ffn_in[l].astype(BF16),
        "conv_w": conv_w[l], "conv_b": conv_b[l].reshape(1, D_FF),
        "w_ffn_out": w_ffn_out[l].astype(BF16), "g_final": g_final.reshape(1, d),
        "g_hg": g_hg[l].reshape(1, HG_WIDTH),
    }


def _trunk(x, mod, tabs, cache, pw, lb_param, *, layer, batch, seq_len, mod_row0, per_seq_mod,
           emit_cache, tm, tq, tag):
    n_tok = batch * seq_len
    hw = MLA_HEADS * LANES
    outs = _proj_call(x, mod, tabs, pw, seq_len=seq_len, mod_row0=mod_row0, per_seq_mod=per_seq_mod,
                      emit_cache=emit_cache, tm=tm)
    q, k, v, hgin, gate = outs[:5]
    hgin = hgin.reshape(batch, seq_len, 5 * HG_WIDTH)
    kvs = [(k.reshape(batch, seq_len, hw), v.reshape(batch, seq_len, hw))]
    s0 = None
    if cache is not None:
        ckv_c, kr_c, s0f, s0b = cache
        past = ckv_c.shape[1]
        kc, vc = _cachekv_call(ckv_c.reshape(batch * past, KV_LORA), kr_c.reshape(batch * past, QK_ROPE),
                               pw, tm=min(512, batch * past))
        kvs.append((kc.reshape(batch, past, hw), vc.reshape(batch, past, hw)))
        s0 = (s0f, s0b)
    full_seq = seq_len == tq and s0 is None
    o_mla, *hg_out = _attn_hg_call(q.reshape(batch, seq_len, hw), kvs, hgin, lb_param, pw["g_hg"], tq=tq,
                                   tk=min(512, seq_len),
                                   heads=MLA_HEADS if seq_len == tq else ATTN_HEADS_PER_STEP,
                                   layer=layer, full_seq=full_seq, name="attn_hg_" + tag)
    if full_seq:
        o_hg, s_f, s_b = hg_out
    else:
        o_hg, s_f, s_b = _hg_scan_call(*hg_out, hgin, pw["g_hg"], s0, name="hg_scan_" + tag)
    common = dict(seq_len=seq_len, mod_row0=mod_row0, per_seq_mod=per_seq_mod, tm=tm)
    y = _mixffn_call(o_mla.reshape(n_tok, MLA_HEADS * V_DIM), o_hg.reshape(n_tok, HG_WIDTH), gate, x,
                     mod, pw, name="mixffn_" + tag, **common)
    extra = outs[5:] if emit_cache else None
    return y, extra, s_f, s_b


def kernel(x_prompt, x_sample, cache_ckv, cache_krope, state_hgrn_fwd, state_hgrn_bwd, c, c_ctx, w_mod,
           b_mod, g_mix, w_in, g_q, w_uq, g_kv, w_ukv, w_mla_o, lb_param, g_hg, w_hg_o, w_out, g_ffn,
           w_ffn_in, conv_w, conv_b, w_ffn_out, g_final):
    batch, seq, d = x_prompt.shape
    dec_batch, dec_seq, _ = x_sample.shape
    depth = w_in.shape[0]
    assert depth == 1, "single trunk layer"
    l = 0
    cvec = jnp.concatenate([c_ctx[None, :], c, jnp.zeros((8 - 1 - dec_batch, d), F32)], axis=0)
    mod = _mod_call(cvec, w_mod[l], b_mod[l]).reshape(8, 6, d)
    pw = _prep_weights(l, g_mix, w_in, g_q, w_uq, g_kv, w_ukv, w_mla_o, g_hg, w_hg_o, w_out, g_ffn,
                       w_ffn_in, conv_w, conv_b, w_ffn_out, g_final)
    tabs = _rope_tables(dec_seq)

    y_ctx, cache_out, s_f, s_b = _trunk(
        x_prompt.reshape(batch * seq, d), mod, None, None, pw, lb_param, layer=l, batch=batch,
        seq_len=seq, mod_row0=0, per_seq_mod=False, emit_cache=True, tm=TOKEN_TILE, tq=min(Q_TILE, seq),
        tag="ctx")
    cache_l = (cache_ckv[:, l], cache_krope[:, l], state_hgrn_fwd[:, l], state_hgrn_bwd[:, l])
    y_lat, _, _, _ = _trunk(
        x_sample.reshape(dec_batch * dec_seq, d), mod, tabs, cache_l, pw, lb_param, layer=l,
        batch=dec_batch, seq_len=dec_seq, mod_row0=1, per_seq_mod=True, emit_cache=False, tm=TOKEN_TILE,
        tq=min(Q_TILE, dec_seq), tag="lat")

    ckv_new, kr_new = cache_out
    return (y_ctx.reshape(batch, seq, d), y_lat.reshape(dec_batch, dec_seq, d),
            ckv_new.reshape(batch, 1, seq, KV_LORA), kr_new.reshape(batch, 1, seq, QK_ROPE),
            s_f.reshape(batch, 1, HG_HEADS, HG_DK, HG_DV), s_b.reshape(batch, 1, HG_HEADS, HG_DK, HG_DV))
```

```python
import functools

import jax
import jax.numpy as jnp
from jax import lax
from jax.experimental import pallas as pl
from jax.experimental.pallas import tpu as pltpu

F32 = jnp.float32
BF16 = jnp.bfloat16

D_MODEL = 1024
GRID_W = 64
MLA_HEADS = 8
QK_NOPE = 64
QK_ROPE = 32
V_DIM = 64
Q_LORA = 384
KV_LORA = 256
ROPE_BASE = 10000.0
HG_HEADS = 4
HG_DK = 128
HG_DV = 128
HG_WIDTH = HG_HEADS * HG_DV
D_FF = 2816
EPS = 1e-6

LANES = 128
SUBLANES = 8
BF16_SUBLANES = 16
HG_CHUNK = 128
HG_LEVELS = 7
HG_PAIR = 2 * LANES
HG_UNROLL = 4
FF_CHUNK = 256
TOKEN_TILE = 512
Q_TILE = 512
ATTN_HEADS_PER_STEP = 8
VMEM_LIMIT = 56 * 1024 * 1024

C_PQ = 0
C_PKV = Q_LORA
C_PKR = C_PKV + KV_LORA
C_HQ = C_PKR + QK_ROPE
C_ZF = C_HQ + HG_WIDTH
C_HG = C_ZF + 3 * HG_WIDTH
C_GA = C_HG + HG_WIDTH
C_END = C_GA + 2 * D_MODEL
KR_LANE0 = LANES - QK_ROPE

ATTN_SCALE = (QK_NOPE + QK_ROPE) ** -0.5 * 1.4426950408889634

NT = (((1,), (1,)), ((), ()))
TN = (((0,), (0,)), ((), ()))


def _rms(x, g):
    return x * lax.rsqrt(jnp.mean(x * x, axis=-1, keepdims=True) + EPS) * g


def _const_spec(shape):
    n = len(shape)
    return pl.BlockSpec(shape, lambda *_: (0,) * n, pipeline_mode=pl.Buffered(1))


def _params(sem):
    return pltpu.CompilerParams(dimension_semantics=sem, vmem_limit_bytes=VMEM_LIMIT)


def _mod_kernel(c_ref, w_ref, b_ref, o_ref):
    c = c_ref[...]
    a = c * jax.nn.sigmoid(c)
    o_ref[...] = jnp.dot(a, w_ref[...], precision=lax.Precision.HIGHEST,
                         preferred_element_type=F32) + b_ref[...]


def _mod_call(cvec, w_mod, b_mod):
    n = w_mod.shape[1]
    tn = 1024
    return pl.pallas_call(
        _mod_kernel,
        grid=(n // tn,),
        in_specs=[pl.BlockSpec((8, D_MODEL), lambda j: (0, 0)),
                  pl.BlockSpec((D_MODEL, tn), lambda j: (0, j)),
                  pl.BlockSpec((1, tn), lambda j: (0, j))],
        out_specs=pl.BlockSpec((8, tn), lambda j: (0, j)),
        out_shape=jax.ShapeDtypeStruct((8, n), F32),
        compiler_params=_params(("arbitrary",)),
        name="mod",
    )(cvec, w_mod, b_mod.reshape(1, n))


def _place_rope_key(k_nope, kr):
    lane = lax.broadcasted_iota(jnp.int32, (1, LANES), 1)
    kr = jnp.where(lane >= QK_NOPE, jnp.where(lane < QK_NOPE + QK_ROPE, kr, 0.0), 0.0)
    return k_nope + jnp.concatenate([kr] * MLA_HEADS, axis=1)


def _proj_kernel(*refs, rope, emit_cache):
    it = iter(refs)
    x_ref = next(it)
    mod_ref = next(it)
    tab_ref = next(it) if rope else None
    gmix_ref, win_ref, gq_ref, wuq_ref, gkv_ref, wk_ref, wv_ref = (next(it) for _ in range(7))
    q_ref, k_ref, v_ref, hg_ref, gate_ref = (next(it) for _ in range(5))
    if emit_cache:
        ckv_ref, kr_ref = next(it), next(it)

    sh1 = mod_ref[0, 0:1, :]
    sc1 = mod_ref[0, 1:2, :]
    hb = (_rms(x_ref[...], gmix_ref[...]) * (1.0 + sc1) + sh1).astype(BF16)

    def proj(c0, c1):
        return lax.dot_general(hb, win_ref[c0:c1, :], NT, preferred_element_type=F32)

    if rope:
        cs1, sa1, sb1 = tab_ref[0], tab_ref[1], tab_ref[2]

        def rot(x, reps):
            w = x.shape[1]
            cs = jnp.concatenate([cs1] * reps, axis=1) if reps > 1 else cs1
            sa = jnp.concatenate([sa1] * reps, axis=1) if reps > 1 else sa1
            sb = jnp.concatenate([sb1] * reps, axis=1) if reps > 1 else sb1
            half = QK_ROPE // 2
            return x * cs + pltpu.roll(x, w - half, 1) * sa + pltpu.roll(x, half, 1) * sb

    qn = _rms(proj(C_PQ, C_PKV), gq_ref[...]).astype(BF16)
    q = jnp.dot(qn, wuq_ref[...], preferred_element_type=F32)
    q = rot(q, MLA_HEADS) if rope else q * ATTN_SCALE
    q_ref[...] = q.astype(BF16)

    ckv = _rms(proj(C_PKV, C_PKR), gkv_ref[...])
    pkr = proj(C_HQ - LANES, C_HQ)
    if emit_cache:
        ckv_ref[...] = ckv
        kr_ref[...] = pkr[:, KR_LANE0:]
    ckv_b = ckv.astype(BF16)
    kr = rot(pkr, 1) if rope else pkr
    k_ref[...] = _place_rope_key(jnp.dot(ckv_b, wk_ref[...], preferred_element_type=F32),
                                 pltpu.roll(kr, LANES - QK_ROPE, 1)).astype(BF16)
    lane = lax.broadcasted_iota(jnp.int32, (1, MLA_HEADS * LANES), 1)
    ones = jnp.where((lane & (LANES - 1)) >= V_DIM, 1.0, 0.0)
    v_ref[...] = (jnp.dot(ckv_b, wv_ref[...], preferred_element_type=F32) + ones).astype(BF16)

    gate_ref[...] = jax.nn.sigmoid(proj(C_GA, C_END)).astype(BF16)

    hq = proj(C_HQ, C_ZF)
    hg_ref[:, 0:HG_WIDTH] = hq * jax.nn.sigmoid(hq)
    og = proj(C_HG, C_GA)
    hg_ref[:, 4 * HG_WIDTH:5 * HG_WIDTH] = og * jax.nn.sigmoid(og)
    hg_ref[:, HG_WIDTH:4 * HG_WIDTH] = proj(C_ZF, C_HG)


def _proj_call(x, mod, tabs, pw, *, seq_len, mod_row0, per_seq_mod, emit_cache, tm):
    n_tok = x.shape[0]
    tps = max(seq_len // tm, 1)
    rope = tabs is not None
    if per_seq_mod:
        mod_map = lambda t: (mod_row0 + t // tps, 0, 0)
    else:
        mod_map = lambda t: (mod_row0, 0, 0)
    row = lambda w: pl.BlockSpec((tm, w), lambda t: (t, 0))
    in_specs = [row(D_MODEL), pl.BlockSpec((1, 6, D_MODEL), mod_map)]
    args = [x, mod]
    if rope:
        in_specs.append(pl.BlockSpec((3, tm, LANES), lambda t: (0, t % tps, 0)))
        args.append(tabs)
    for name in ("g_mix", "w_in", "g_q", "w_uq", "g_kv", "w_k", "w_v"):
        in_specs.append(_const_spec(pw[name].shape))
        args.append(pw[name])
    hw = MLA_HEADS * LANES
    out_shape = [jax.ShapeDtypeStruct((n_tok, hw), BF16),
                 jax.ShapeDtypeStruct((n_tok, hw), BF16),
                 jax.ShapeDtypeStruct((n_tok, hw), BF16),
                 jax.ShapeDtypeStruct((n_tok, 5 * HG_WIDTH), F32),
                 jax.ShapeDtypeStruct((n_tok, 2 * D_MODEL), BF16)]
    out_specs = [row(hw), row(hw), row(hw), row(5 * HG_WIDTH), row(2 * D_MODEL)]
    if emit_cache:
        out_shape += [jax.ShapeDtypeStruct((n_tok, KV_LORA), F32),
                      jax.ShapeDtypeStruct((n_tok, QK_ROPE), F32)]
        out_specs += [row(KV_LORA), row(QK_ROPE)]
    return pl.pallas_call(
        functools.partial(_proj_kernel, rope=rope, emit_cache=emit_cache),
        grid=(n_tok // tm,),
        in_specs=in_specs,
        out_specs=out_specs,
        out_shape=out_shape,
        compiler_params=_params(("arbitrary",)),
        name="proj_lat" if rope else "proj_ctx",
    )(*args)


def _cachekv_kernel(ckv_ref, kr_ref, wk_ref, wv_ref, k_ref, v_ref):
    ckv_b = ckv_ref[...].astype(BF16)
    tm = ckv_b.shape[0]
    kr = jnp.concatenate([jnp.zeros((tm, QK_NOPE), F32), kr_ref[...],
                          jnp.zeros((tm, LANES - QK_NOPE - QK_ROPE), F32)], axis=1)
    k_ref[...] = _place_rope_key(jnp.dot(ckv_b, wk_ref[...], preferred_element_type=F32), kr).astype(BF16)
    lane = lax.broadcasted_iota(jnp.int32, (1, MLA_HEADS * LANES), 1)
    ones = jnp.where((lane & (LANES - 1)) >= V_DIM, 1.0, 0.0)
    v_ref[...] = (jnp.dot(ckv_b, wv_ref[...], preferred_element_type=F32) + ones).astype(BF16)


def _cachekv_call(ckv, kr, pw, tm):
    n_tok = ckv.shape[0]
    hw = MLA_HEADS * LANES
    row = lambda w: pl.BlockSpec((tm, w), lambda t: (t, 0))
    return pl.pallas_call(
        _cachekv_kernel,
        grid=(n_tok // tm,),
        in_specs=[row(KV_LORA), row(QK_ROPE), _const_spec(pw["w_k"].shape), _const_spec(pw["w_v"].shape)],
        out_specs=[row(hw), row(hw)],
        out_shape=[jax.ShapeDtypeStruct((n_tok, hw), BF16)] * 2,
        compiler_params=_params(("arbitrary",)),
        name="cache_kv",
    )(ckv, kr, pw["w_k"], pw["w_v"])


def _hg_consts():
    c_len, sub = HG_CHUNK, SUBLANES
    row8 = lax.broadcasted_iota(jnp.int32, (1, sub, HG_PAIR), 1)
    upper8 = [(row8 & (1 << j)) != 0 for j in range(3)]
    ti = lax.broadcasted_iota(jnp.int32, (c_len, c_len), 0)
    si = lax.broadcasted_iota(jnp.int32, (c_len, c_len), 1)
    txs = ti ^ si
    lvl = jnp.full((c_len, c_len), -1, jnp.int32)
    for j in range(HG_LEVELS):
        lvl = lvl + jnp.where(txs >= (1 << j), 1, 0)
    return upper8, jnp.concatenate([lvl, lvl], axis=1)


def _hg_lower_bounds(lbp, layer):
    e = jnp.exp(lbp - jnp.max(lbp, axis=0, keepdims=True))
    return jnp.sum(e[:layer + 1], axis=0) / jnp.sum(e, axis=0)


def _hg_gates(z, lbv):
    f = lbv + (1.0 - lbv) * jax.nn.sigmoid(z)
    return jnp.log2(f), 1.0 - f


def _hg_rep(slab, n_rows):
    return slab if n_rows == SUBLANES else jnp.concatenate([slab] * (n_rows // SUBLANES), axis=0)


def _hg_halves(j):
    h = 1 << j
    return [(slice(b, b + h), slice(b + h, b + 2 * h)) for b in range(0, HG_CHUNK, 2 * h)]


def _hg_sel8(upper8, j, a, b):
    shape3 = (HG_CHUNK // SUBLANES, SUBLANES, a.shape[1])
    return jnp.where(upper8[j], a.reshape(shape3), b.reshape(shape3)).reshape(a.shape)


def _hg_blockdiag(x):
    z = jnp.zeros((x.shape[0], LANES), x.dtype)
    return jnp.concatenate([jnp.concatenate([x[:, :LANES], z], axis=1),
                            jnp.concatenate([z, x[:, LANES:]], axis=1)], axis=0)


def _hg_upsweep(x, inclusive, upper8):
    c_len, sub, width = HG_CHUNK, SUBLANES, x.shape[1]
    x3 = x.reshape(c_len // sub, sub, width)
    pre3 = [x3 if inclusive else jnp.zeros_like(x3)]
    tot3 = [x3]
    for j in range(3):
        h = 1 << j
        t = tot3[-1]
        if 2 * h == sub:
            sib = pltpu.roll(t, h, 1)
        else:
            sib = jnp.where(upper8[j], pltpu.roll(t, h, 1), pltpu.roll(t, sub - h, 1))
        tot3.append(t + sib)
        pre3.append(pre3[-1] + jnp.where(upper8[j], sib, 0.0))
    pre = [p.reshape(c_len, width) for p in pre3]
    tot = [t.reshape(c_len, width) for t in tot3]
    slabs = {3: [tot[3][r:r + sub] for r in range(0, c_len, sub)]}
    for j in range(3, HG_LEVELS):
        h = 1 << j
        p, s = pre[-1], slabs[j]
        parts = []
        for i, (lo, up) in enumerate(_hg_halves(j)):
            parts += [p[lo], p[up] + _hg_rep(s[2 * i], h)]
        pre.append(jnp.concatenate(parts, axis=0))
        slabs[j + 1] = [s[2 * i] + s[2 * i + 1] for i in range(len(s) // 2)]
    return pre, tot, slabs


def _hg_chunk_terms(q, zf, zb, v, lb_f, lb_b, upper8, lvl):
    c_len, top = HG_CHUNK, HG_LEVELS
    vb = v.astype(BF16)
    lgf, kf = _hg_gates(zf, lb_f)
    lgb, kb = _hg_gates(zb, lb_b)
    pf, tf, sf = _hg_upsweep(lgf, True, upper8)
    xb, tb, sb = _hg_upsweep(lgb, False, upper8)
    a = lax.dot_general(q.astype(BF16), _hg_blockdiag((kf + kb).astype(BF16)), NT,
                        preferred_element_type=F32)
    a = jnp.where(lvl == -1, a, 0.0)
    for j in range(HG_LEVELS):
        h = 1 << j
        if j < 3:
            eq = _hg_sel8(upper8, j, pf[j], tb[j] - xb[j])
            ek = _hg_sel8(upper8, j, xb[j], tf[j] - pf[j])
            ksel = _hg_sel8(upper8, j, kb, kf)
        else:
            eqs, eks, kss = [], [], []
            for i, (lo, up) in enumerate(_hg_halves(j)):
                eqs += [_hg_rep(sb[j][2 * i], h) - xb[j][lo], pf[j][up]]
                eks += [_hg_rep(sf[j][2 * i], h) - pf[j][lo], xb[j][up]]
                kss += [kf[lo], kb[up]]
            eq, ek, ksel = (jnp.concatenate(p, axis=0) for p in (eqs, eks, kss))
        qs = (q * jnp.exp2(eq)).astype(BF16)
        ks = (ksel * jnp.exp2(ek)).astype(BF16)
        aj = lax.dot_general(qs, _hg_blockdiag(ks), NT, preferred_element_type=F32)
        a = jnp.where(lvl == j, aj, a)
    o_intra = jnp.dot(a.astype(BF16), _hg_blockdiag(vb), preferred_element_type=F32)
    tot_f, tot_b = sf[top][0], sb[top][0]
    qhf = (q * jnp.exp2(pf[top])).astype(BF16)
    qhb = (q * jnp.exp2(_hg_rep(tot_b, c_len) - xb[top])).astype(BF16)
    khf = (kf * jnp.exp2(_hg_rep(tot_f, c_len) - pf[top])).astype(BF16)
    khb = (kb * jnp.exp2(xb[top])).astype(BF16)
    uts = []
    for hh in range(2):
        hl = slice(hh * LANES, (hh + 1) * LANES)
        uts.append(lax.dot_general(vb[:, hl], jnp.concatenate([khf[:, hl], khb[:, hl]], axis=1), TN,
                                   preferred_element_type=F32))
    return o_intra, qhf, qhb, uts, jnp.exp2(tot_f[0:1, :]), jnp.exp2(tot_b[0:1, :])


def _attn_hg_kernel(*refs, nseg, tk, layer, full_seq):
    it = iter(refs)
    q_ref = next(it)
    kv = [next(it) for _ in range(2 * nseg)]
    hq_ref, zf_ref, zb_ref, hv_ref, lbp_ref = (next(it) for _ in range(5))
    if full_seq:
        g_ref, ghg_ref = next(it), next(it)
        o_ref, ohg_ref, sf_ref, sb_ref = (next(it) for _ in range(4))
    else:
        o_ref, oi_ref, qhf_ref, qhb_ref, utf_ref, utb_ref, df_ref, db_ref = (next(it) for _ in range(8))
    s_ref = next(it)
    tq = q_ref.shape[1]
    heads = q_ref.shape[2] // LANES
    hg_heads = hq_ref.shape[2] // LANES
    lane = lax.broadcasted_iota(jnp.int32, (1, LANES), 1)
    chunks = [(kv[2 * seg], kv[2 * seg + 1], c * tk)
              for seg in range(nseg) for c in range(kv[2 * seg].shape[1] // tk)]

    upper8, lvl = _hg_consts()
    lb = _hg_lower_bounds(lbp_ref[...], layer)
    local = [(ci, p) for ci in range(tq // HG_CHUNK) for p in range(hg_heads // 2)]
    kept = {}

    def hg_local(ci, p):
        rs = slice(ci * HG_CHUNK, (ci + 1) * HG_CHUNK)
        cs = slice(p * HG_PAIR, (p + 1) * HG_PAIR)
        oi, qhf, qhb, uts, dec_f, dec_b = _hg_chunk_terms(
            hq_ref[0, rs, cs], zf_ref[0, rs, cs], zb_ref[0, rs, cs], hv_ref[0, rs, cs],
            lb[0:1, cs], lb[1:2, cs], upper8, lvl)
        for hh in range(2):
            h = 2 * p + hh
            hl = slice(hh * LANES, (hh + 1) * LANES)
            utf, utb = uts[hh][:, :LANES], uts[hh][:, LANES:]
            if full_seq:
                kept[(ci, h)] = (oi[:, hl], qhf[:, hl], qhb[:, hl], utf, utb, dec_f[:, hl], dec_b[:, hl])
            else:
                utf_ref[0, ci, h] = utf
                utb_ref[0, ci, h] = utb
        if not full_seq:
            oi_ref[0, rs, cs] = oi
            qhf_ref[0, rs, cs] = qhf
            qhb_ref[0, rs, cs] = qhb
            df_ref[0, ci, :, cs] = dec_f
            db_ref[0, ci, :, cs] = dec_b

    outs = []
    for h in range(heads):
        hs = slice(h * LANES, (h + 1) * LANES)
        qh = q_ref[0, :, hs]
        mx = jnp.full((tq, LANES), -jnp.inf, F32)
        for i, (k_ref, _, off) in enumerate(chunks):
            s = lax.dot_general(qh, k_ref[0, off:off + tk, hs], NT, preferred_element_type=F32)
            s_ref[:, i * tk:(i + 1) * tk] = s
            for w in range(tk // LANES):
                mx = jnp.maximum(mx, s[:, w * LANES:(w + 1) * LANES])
        m = jnp.broadcast_to(jnp.max(mx, axis=1, keepdims=True), (tq, LANES))
        mb = jnp.concatenate([m] * (tk // LANES), axis=1)
        acc = jnp.zeros((tq, LANES), F32)
        for i, (_, v_ref, off) in enumerate(chunks):
            p = jnp.exp2(s_ref[:, i * tk:(i + 1) * tk] - mb).astype(BF16)
            acc = acc + jnp.dot(p, v_ref[0, off:off + tk, hs], preferred_element_type=F32)
        outs.append(acc / pltpu.roll(acc, V_DIM, 1))
    for ci, p in local:
        hg_local(ci, p)
    for pair in range(heads // 2):
        a = outs[2 * pair]
        b = pltpu.roll(outs[2 * pair + 1], V_DIM, 1)
        o_ref[0, :, pair * LANES:(pair + 1) * LANES] = jnp.where(lane < V_DIM, a, b).astype(BF16)

    if full_seq:
        n = tq // HG_CHUNK
        for h in range(hg_heads):
            cs = slice(h * LANES, (h + 1) * LANES)
            enter_b, st = [None] * n, None
            for c in reversed(range(n)):
                enter_b[c] = st
                _, _, _, _, utb, _, dec_b = kept[(c, h)]
                st = utb if st is None else dec_b * st + utb
            sb_ref[0, h] = st.T
            st = None
            for c in range(n):
                rs = slice(c * HG_CHUNK, (c + 1) * HG_CHUNK)
                oi, qhf, qhb, utf, _, dec_f, _ = kept[(c, h)]
                if st is not None:
                    oi = oi + lax.dot_general(qhf, st.astype(BF16), NT, preferred_element_type=F32)
                if enter_b[c] is not None:
                    oi = oi + lax.dot_general(qhb, enter_b[c].astype(BF16), NT, preferred_element_type=F32)
                ohg_ref[0, rs, cs] = (_rms(oi, ghg_ref[:, cs]) * g_ref[0, rs, cs]).astype(BF16)
                st = utf if st is None else dec_f * st + utf
            sf_ref[0, h] = st.T


def _attn_hg_call(q, kvs, hgin, lb_param, g_hg, *, tq, tk, heads, layer, full_seq, name):
    b, t, _ = q.shape
    n_chunks = t // HG_CHUNK
    cpt = tq // HG_CHUNK
    groups = MLA_HEADS // heads
    hgh = HG_HEADS // groups
    assert heads % 2 == 0 and hgh % 2 == 0
    aw, gw = heads * LANES, hgh * LANES
    in_specs = [pl.BlockSpec((1, tq, aw), lambda i, p, j: (i, j, p))]
    args = [q]
    n_keys = 0
    for k, v in kvs:
        n_keys += k.shape[1]
        for a in (k, v):
            mode = dict(pipeline_mode=pl.Buffered(1)) if groups == 1 and t // tq > 1 else {}
            in_specs.append(pl.BlockSpec((1, a.shape[1], aw), lambda i, p, j: (i, 0, p), **mode))
            args.append(a)
    for g in range(4):
        in_specs.append(pl.BlockSpec((1, tq, gw), lambda i, p, j, g=g: (i, j, g * groups + p)))
        args.append(hgin)
    in_specs.append(pl.BlockSpec((lb_param.shape[0], 2, gw), lambda i, p, j: (0, 0, p)))
    args.append(lb_param)
    rows = lambda w: pl.BlockSpec((1, tq, w), lambda i, p, j: (i, j, p))
    o_shape = jax.ShapeDtypeStruct((b, t, MLA_HEADS * V_DIM), BF16)
    if full_seq:
        assert t == tq
        in_specs += [pl.BlockSpec((1, tq, gw), lambda i, p, j: (i, j, 4 * groups + p)),
                     pl.BlockSpec((1, gw), lambda i, p, j: (0, p))]
        args += [hgin, g_hg]
        st_spec = pl.BlockSpec((1, hgh, HG_DK, HG_DV), lambda i, p, j: (i, p, 0, 0))
        st_shape = jax.ShapeDtypeStruct((b, HG_HEADS, HG_DK, HG_DV), F32)
        out_specs = [rows(heads * V_DIM), rows(gw), st_spec, st_spec]
        out_shape = [o_shape, jax.ShapeDtypeStruct((b, t, HG_WIDTH), BF16), st_shape, st_shape]
    else:
        ut_spec = pl.BlockSpec((1, cpt, hgh, HG_DV, HG_DK), lambda i, p, j: (i, j, p, 0, 0))
        dec_spec = pl.BlockSpec((1, cpt, 1, gw), lambda i, p, j: (i, j, 0, p))
        ut_shape = jax.ShapeDtypeStruct((b, n_chunks, HG_HEADS, HG_DV, HG_DK), F32)
        dec_shape = jax.ShapeDtypeStruct((b, n_chunks, 1, HG_WIDTH), F32)
        out_specs = [rows(heads * V_DIM), rows(gw), rows(gw), rows(gw),
                     ut_spec, ut_spec, dec_spec, dec_spec]
        out_shape = [o_shape,
                     jax.ShapeDtypeStruct((b, t, HG_WIDTH), F32),
                     jax.ShapeDtypeStruct((b, t, HG_WIDTH), BF16),
                     jax.ShapeDtypeStruct((b, t, HG_WIDTH), BF16),
                     ut_shape, ut_shape, dec_shape, dec_shape]
    return pl.pallas_call(
        functools.partial(_attn_hg_kernel, nseg=len(kvs), tk=tk, layer=layer, full_seq=full_seq),
        grid=(b, groups, t // tq),
        in_specs=in_specs,
        out_specs=out_specs,
        out_shape=out_shape,
        scratch_shapes=[pltpu.VMEM((tq, n_keys), F32)],
        compiler_params=_params(("arbitrary", "arbitrary", "arbitrary")),
        name=name,
    )(*args)


def _hg_scan_kernel(*refs, has_s0):
    it = iter(refs)
    oi_ref, qhf_ref, qhb_ref, utf_ref, utb_ref, df_ref, db_ref, g_ref, ghg_ref = (next(it) for _ in range(9))
    if has_s0:
        s0f_ref, s0b_ref = next(it), next(it)
    o_ref, sf_ref, sb_ref, sbst_ref = (next(it) for _ in range(4))
    c_len = HG_CHUNK
    n_chunks = utf_ref.shape[1]
    unroll = min(HG_UNROLL, n_chunks)

    def rows_of(c):
        return pl.ds(pl.multiple_of(c * c_len, c_len), c_len)

    def bwd_body(i, st):
        c = n_chunks - 1 - i
        sbst_ref[c] = st.astype(BF16)
        return db_ref[0, c] * st + utb_ref[0, c, 0]

    st_b0 = s0b_ref[0, 0].T if has_s0 else jnp.zeros((HG_DV, HG_DK), F32)
    st_b = lax.fori_loop(0, n_chunks, bwd_body, st_b0, unroll=unroll)
    sb_ref[0, 0] = st_b.T

    ghg = ghg_ref[...]

    def fwd_body(c, st):
        qcat = jnp.concatenate([qhf_ref[0, rows_of(c), :], qhb_ref[0, rows_of(c), :]], axis=1)
        scat = jnp.concatenate([st.astype(BF16), sbst_ref[c]], axis=1)
        o = oi_ref[0, rows_of(c), :] + lax.dot_general(qcat, scat, NT, preferred_element_type=F32)
        on = _rms(o, ghg) * g_ref[0, rows_of(c), :]
        o_ref[0, rows_of(c), :] = on.astype(BF16)
        return df_ref[0, c] * st + utf_ref[0, c, 0]

    st_f0 = s0f_ref[0, 0].T if has_s0 else jnp.zeros((HG_DV, HG_DK), F32)
    st_f = lax.fori_loop(0, n_chunks, fwd_body, st_f0, unroll=unroll)
    sf_ref[0, 0] = st_f.T


def _hg_scan_call(oi, qhf, qhb, utf, utb, dec_f, dec_b, hgin, g_hg, s0, *, name):
    b, t, _ = oi.shape
    n_chunks = t // HG_CHUNK
    has_s0 = s0 is not None
    col = pl.BlockSpec((1, t, LANES), lambda i, h: (i, 0, h))
    ut_spec = pl.BlockSpec((1, n_chunks, 1, HG_DV, HG_DK), lambda i, h: (i, 0, h, 0, 0))
    dec_spec = pl.BlockSpec((1, n_chunks, 1, LANES), lambda i, h: (i, 0, 0, h))
    gate_col = pl.BlockSpec((1, t, LANES), lambda i, h: (i, 0, 4 * HG_HEADS + h))
    in_specs = [col, col, col, ut_spec, ut_spec, dec_spec, dec_spec, gate_col,
                pl.BlockSpec((1, LANES), lambda i, h: (0, h))]
    args = [oi, qhf, qhb, utf, utb, dec_f, dec_b, hgin, g_hg]
    st_spec = pl.BlockSpec((1, 1, HG_DK, HG_DV), lambda i, h: (i, h, 0, 0))
    if has_s0:
        in_specs += [st_spec, st_spec]
        args += list(s0)
    st_shape = jax.ShapeDtypeStruct((b, HG_HEADS, HG_DK, HG_DV), F32)
    return pl.pallas_call(
        functools.partial(_hg_scan_kernel, has_s0=has_s0),
        grid=(b, HG_HEADS),
        in_specs=in_specs,
        out_specs=[col, st_spec, st_spec],
        out_shape=[jax.ShapeDtypeStruct((b, t, HG_WIDTH), BF16), st_shape, st_shape],
        scratch_shapes=[pltpu.VMEM((n_chunks, HG_DV, HG_DK), BF16)],
        compiler_params=_params(("arbitrary", "arbitrary")),
        name=name,
    )(*args)


def _gelu_tanh(x):
    return 0.5 * x * (1.0 + jnp.tanh(0.7978845608028654 * (x + 0.044715 * (x * x * x))))


def _mixffn_kernel(om_ref, omp_ref, omn_ref, oh_ref, ohp_ref, ohn_ref, gt_ref, gtp_ref, gtn_ref,
                   x_ref, xp_ref, xn_ref, mod_ref, wmo_ref, who_ref, wout_ref, gffn_ref,
                   wi_ref, cw_ref, cb_ref, wo_ref, gfin_ref, y_ref, gsc_ref, u_ref, *, tm, seq_len):
    halo = BF16_SUBLANES
    t = pl.program_id(0)

    def ext(p_ref, m_ref, n_ref):
        return jnp.concatenate([p_ref[...], m_ref[...], n_ref[...]], axis=0)

    a = jnp.dot(ext(omp_ref, om_ref, omn_ref), wmo_ref[...], preferred_element_type=F32)
    b = jnp.dot(ext(ohp_ref, oh_ref, ohn_ref), who_ref[...], preferred_element_type=F32)
    gate = ext(gtp_ref, gt_ref, gtn_ref)
    mix = gate[:, 0:D_MODEL] * a + gate[:, D_MODEL:2 * D_MODEL] * b
    y = jnp.dot(mix.astype(BF16), wout_ref[...], preferred_element_type=F32)
    gt1 = mod_ref[0, 2:3, :]
    sh2 = mod_ref[0, 3:4, :]
    sc2 = mod_ref[0, 4:5, :]
    x1 = ext(xp_ref, x_ref, xn_ref) + gt1 * y
    h2 = (_rms(x1, gffn_ref[...]) * (1.0 + sc2) + sh2).astype(BF16)
    h2_own = h2[halo:halo + tm]

    pos = (lax.broadcasted_iota(jnp.int32, (tm, 1), 0) + t * tm) & (seq_len - 1)
    has_prev = pos != 0
    has_next = pos != seq_len - 1
    for j in range(D_FF // FF_CHUNK):
        cs = slice(j * FF_CHUNK, (j + 1) * FF_CHUNK)
        vs = slice(D_FF + j * FF_CHUNK, D_FF + (j + 1) * FF_CHUNK)
        gsc_ref[...] = jnp.dot(h2, wi_ref[:, cs], preferred_element_type=F32)
        g_prev = jnp.where(has_prev, gsc_ref[halo - 1:halo - 1 + tm, :], 0.0)
        g_cur = gsc_ref[halo:halo + tm, :]
        g_next = jnp.where(has_next, gsc_ref[halo + 1:halo + 1 + tm, :], 0.0)
        gconv = cb_ref[:, cs] + g_prev * cw_ref[0:1, cs] + g_cur * cw_ref[1:2, cs] + g_next * cw_ref[2:3, cs]
        val = jnp.dot(h2_own, wi_ref[:, vs], preferred_element_type=F32)
        u_ref[:, cs] = (_gelu_tanh(gconv) * val).astype(BF16)
    f = jnp.dot(u_ref[...], wo_ref[...], preferred_element_type=F32)
    gt2 = mod_ref[0, 5:6, :]
    y_ref[...] = _rms(x1[halo:halo + tm] + gt2 * f, gfin_ref[...])


def _mixffn_call(om, oh, gate, x, mod, pw, *, seq_len, mod_row0, per_seq_mod, tm, name):
    n_tok = x.shape[0]
    tps = max(seq_len // tm, 1)
    halo = BF16_SUBLANES
    n_hblk = n_tok // halo
    hpt = tm // halo
    if per_seq_mod:
        mod_map = lambda t: (mod_row0 + t // tps, 0, 0)
    else:
        mod_map = lambda t: (mod_row0, 0, 0)

    def with_halo(w):
        return [pl.BlockSpec((tm, w), lambda t: (t, 0)),
                pl.BlockSpec((halo, w), lambda t: (jnp.maximum(t * hpt - 1, 0), 0)),
                pl.BlockSpec((halo, w), lambda t: (jnp.minimum((t + 1) * hpt, n_hblk - 1), 0))]

    ws = [pw[k] for k in ("w_mla_o", "w_hg_o", "w_out", "g_ffn", "w_ffn_in", "conv_w", "conv_b",
                          "w_ffn_out", "g_final")]
    return pl.pallas_call(
        functools.partial(_mixffn_kernel, tm=tm, seq_len=seq_len),
        grid=(n_tok // tm,),
        in_specs=(with_halo(MLA_HEADS * V_DIM) + with_halo(HG_WIDTH) + with_halo(2 * D_MODEL)
                  + with_halo(D_MODEL) + [pl.BlockSpec((1, 6, D_MODEL), mod_map)]
                  + [_const_spec(w.shape) for w in ws]),
        out_specs=pl.BlockSpec((tm, D_MODEL), lambda t: (t, 0)),
        out_shape=jax.ShapeDtypeStruct((n_tok, D_MODEL), F32),
        scratch_shapes=[pltpu.VMEM((tm + 2 * halo, FF_CHUNK), F32),
                        pltpu.VMEM((tm, D_FF), BF16)],
        compiler_params=_params(("arbitrary",)),
        name=name,
    )(om, om, om, oh, oh, oh, gate, gate, gate, x, x, x, mod, *ws)


def _rope_tables(seq_len):
    n_freq = QK_ROPE // 4
    inv = jnp.power(ROPE_BASE, -jnp.arange(n_freq, dtype=F32) / n_freq)
    z8 = jnp.zeros((n_freq,), F32)
    z64 = jnp.zeros((QK_NOPE,), F32)
    inv_r = jnp.concatenate([z64] + [inv, z8] * 4)
    inv_c = jnp.concatenate([z64] + [z8, inv] * 4)
    pos = jnp.arange(seq_len, dtype=jnp.int32)
    r_pos = (pos // GRID_W).astype(F32)[:, None]
    c_pos = (pos % GRID_W).astype(F32)[:, None]
    ang = r_pos * inv_r + c_pos * inv_c
    cos, sin = jnp.cos(ang), jnp.sin(ang)
    s = ATTN_SCALE
    half = QK_ROPE // 2
    o16, z16 = jnp.ones((half,), F32), jnp.zeros((half,), F32)
    c_cs = jnp.concatenate([jnp.full((QK_NOPE + QK_ROPE,), s, F32), o16, o16])
    c_sa = jnp.concatenate([z64, -s * o16, z16, -o16, z16])
    c_sb = jnp.concatenate([z64, z16, s * o16, z16, o16])
    return jnp.stack([cos * c_cs, sin * c_sa, sin * c_sb], axis=0)


def _prep_weights(l, g_mix, w_in, g_q, w_uq, g_kv, w_ukv, w_mla_o, g_hg, w_hg_o, w_out, g_ffn,
                  w_ffn_in, conv_w, conv_b, w_ffn_out, g_final):
    d = D_MODEL
    w_in_t = jnp.swapaxes(w_in[l], 0, 1).astype(BF16)
    hd = QK_NOPE + QK_ROPE
    w_uq_p = jnp.concatenate([w_uq[l].reshape(Q_LORA, MLA_HEADS, hd),
                              jnp.zeros((Q_LORA, MLA_HEADS, LANES - hd), F32)], axis=-1)
    wkv = w_ukv[l].reshape(KV_LORA, MLA_HEADS, QK_NOPE + V_DIM)
    zpad = jnp.zeros((KV_LORA, MLA_HEADS, LANES - QK_NOPE), F32)
    w_k = jnp.concatenate([wkv[..., :QK_NOPE], zpad], axis=-1).reshape(KV_LORA, MLA_HEADS * LANES)
    w_v =jnp.concatenate([wkv[..., QK_NOPE:], jnp.zeros((KV_LORA, MLA_HEADS, LANES - V_DIM), F32)],
                          axis=-1).reshape(KV_LORA, MLA_HEADS * LANES)
    return {
        "g_mix": g_mix[l].reshape(1, d), "w_in": w_in_t,
        "g_q": g_q[l].reshape(1, Q_LORA), "w_uq": w_uq_p.reshape(Q_LORA, MLA_HEADS * LANES).astype(BF16),
        "g_kv": g_kv[l].reshape(1, KV_LORA), "w_k": w_k.astype(BF16), "w_v": w_v.astype(BF16),
        "w_mla_o": w_mla_o[l].astype(BF16), "w_hg_o": w_hg_o[l].astype(BF16),
        "w_out": w_out[l].astype(BF16), "g_ffn": g_ffn[l].reshape(1, d),
        "w_ffn_in": w_ffn_in[l].astype(BF16),
        "conv_w": conv_w[l], "conv_b": conv_b[l].reshape(1, D_FF),
        "w_ffn_out": w_ffn_out[l].astype(BF16), "g_final": g_final.reshape(1, d),
        "g_hg": g_hg[l].reshape(1, HG_WIDTH),
    }


def _trunk(x, mod, tabs, cache, pw, lb_param, *, layer, batch, seq_len, mod_row0, per_seq_mod,
           emit_cache, tm, tq, tag):
    n_tok = batch * seq_len
    hw = MLA_HEADS * LANES
    outs = _proj_call(x, mod, tabs, pw, seq_len=seq_len, mod_row0=mod_row0, per_seq_mod=per_seq_mod,
                      emit_cache=emit_cache, tm=tm)
    q, k, v, hgin, gate = outs[:5]
    hgin = hgin.reshape(batch, seq_len, 5 * HG_WIDTH)
    kvs = [(k.reshape(batch, seq_len, hw), v.reshape(batch, seq_len, hw))]
    s0 = None
    if cache is not None:
        ckv_c, kr_c, s0f, s0b = cache
        past = ckv_c.shape[1]
        kc, vc = _cachekv_call(ckv_c.reshape(batch * past, KV_LORA), kr_c.reshape(batch * past, QK_ROPE),
                               pw, tm=min(512, batch * past))
        kvs.append((kc.reshape(batch, past, hw), vc.reshape(batch, past, hw)))
        s0 = (s0f, s0b)
    full_seq = seq_len == tq and s0 is None
    o_mla, *hg_out = _attn_hg_call(q.reshape(batch, seq_len, hw), kvs, hgin, lb_param, pw["g_hg"], tq=tq,
                                   tk=min(512, seq_len),
                                   heads=MLA_HEADS if seq_len == tq else ATTN_HEADS_PER_STEP,
                                   layer=layer, full_seq=full_seq, name="attn_hg_" + tag)
    if full_seq:
        o_hg, s_f, s_b = hg_out
    else:
        o_hg, s_f, s_b = _hg_scan_call(*hg_out, hgin, pw["g_hg"], s0, name="hg_scan_" + tag)
    common = dict(seq_len=seq_len, mod_row0=mod_row0, per_seq_mod=per_seq_mod, tm=tm)
    y = _mixffn_call(o_mla.reshape(n_tok, MLA_HEADS * V_DIM), o_hg.reshape(n_tok, HG_WIDTH), gate, x,
                     mod, pw, name="mixffn_" + tag, **common)
    extra = outs[5:] if emit_cache else None
    return y, extra, s_f, s_b


def kernel(x_prompt, x_sample, cache_ckv, cache_krope, state_hgrn_fwd, state_hgrn_bwd, c, c_ctx, w_mod,
           b_mod, g_mix, w_in, g_q, w_uq, g_kv, w_ukv, w_mla_o, lb_param, g_hg, w_hg_o, w_out, g_ffn,
           w_ffn_in, conv_w, conv_b, w_ffn_out, g_final):
    batch, seq, d = x_prompt.shape
    dec_batch, dec_seq, _ = x_sample.shape
    depth = w_in.shape[0]
    assert depth == 1, "single trunk layer"
    l = 0
    cvec = jnp.concatenate([c_ctx[None, :], c, jnp.zeros((8 - 1 - dec_batch, d), F32)], axis=0)
    mod = _mod_call(cvec, w_mod[l], b_mod[l]).reshape(8, 6, d)
    pw = _prep_weights(l, g_mix, w_in, g_q, w_uq, g_kv, w_ukv, w_mla_o, g_hg, w_hg_o, w_out, g_ffn,
                       w_ffn_in, conv_w, conv_b, w_ffn_out, g_final)
    tabs = _rope_tables(dec_seq)

    y_ctx, cache_out, s_f, s_b = _trunk(
        x_prompt.reshape(batch * seq, d), mod, None, None, pw, lb_param, layer=l, batch=batch,
        seq_len=seq, mod_row0=0, per_seq_mod=False, emit_cache=True, tm=TOKEN_TILE, tq=min(Q_TILE, seq),
        tag="ctx")
    cache_l = (cache_ckv[:, l], cache_krope[:, l], state_hgrn_fwd[:, l], state_hgrn_bwd[:, l])
    y_lat, _, _, _ = _trunk(
        x_sample.reshape(dec_batch * dec_seq, d), mod, tabs, cache_l, pw, lb_param, layer=l,
        batch=dec_batch, seq_len=dec_seq, mod_row0=1, per_seq_mod=True, emit_cache=False, tm=TOKEN_TILE,
        tq=min(Q_TILE, dec_seq), tag="lat")

    ckv_new, kr_new = cache_out
    return (y_ctx.reshape(batch, seq, d), y_lat.reshape(dec_batch, dec_seq, d),
            ckv_new.reshape(batch, 1, seq, KV_LORA), kr_new.reshape(batch, 1, seq, QK_ROPE),
            s_f.reshape(batch, 1, HG_HEADS, HG_DK, HG_DV), s_b.reshape(batch, 1, HG_HEADS, HG_DK, HG_DV))
```

```python
import functools

import jax
import jax.numpy as jnp
from jax import lax
from jax.experimental import pallas as pl
from jax.experimental.pallas import tpu as pltpu

F32 = jnp.float32
BF16 = jnp.bfloat16

D_MODEL = 1024
GRID_W = 64
MLA_HEADS = 8
QK_NOPE = 64
QK_ROPE = 32
V_DIM = 64
Q_LORA = 384
KV_LORA = 256
ROPE_BASE = 10000.0
HG_HEADS = 4
HG_DK = 128
HG_DV = 128
HG_WIDTH = HG_HEADS * HG_DV
D_FF = 2816
EPS = 1e-6

LANES = 128
SUBLANES = 8
BF16_SUBLANES = 16
HG_CHUNK = 128
HG_LEVELS = 7
HG_PAIR = 2 * LANES
HG_UNROLL = 4
FF_CHUNK = 256
TOKEN_TILE = 512
Q_TILE = 512
ATTN_HEADS_PER_STEP = 8
VMEM_LIMIT = 56 * 1024 * 1024

C_PQ = 0
C_PKV = Q_LORA
C_PKR = C_PKV + KV_LORA
C_HQ = C_PKR + QK_ROPE
C_ZF = C_HQ + HG_WIDTH
C_HG = C_ZF + 3 * HG_WIDTH
C_GA = C_HG + HG_WIDTH
C_END = C_GA + 2 * D_MODEL
KR_LANE0 = LANES - QK_ROPE

ATTN_SCALE = (QK_NOPE + QK_ROPE) ** -0.5 * 1.4426950408889634

NT = (((1,), (1,)), ((), ()))
TN = (((0,), (0,)), ((), ()))


def _rms(x, g):
    return x * lax.rsqrt(jnp.mean(x * x, axis=-1, keepdims=True) + EPS) * g


def _const_spec(shape):
    n = len(shape)
    return pl.BlockSpec(shape, lambda *_: (0,) * n, pipeline_mode=pl.Buffered(1))


def _params(sem):
    return pltpu.CompilerParams(dimension_semantics=sem, vmem_limit_bytes=VMEM_LIMIT)


def _mod_kernel(c_ref, w_ref, b_ref, o_ref):
    c = c_ref[...]
    a = c * jax.nn.sigmoid(c)
    o_ref[...] = jnp.dot(a, w_ref[...], precision=lax.Precision.HIGHEST,
                         preferred_element_type=F32) + b_ref[...]


def _mod_call(cvec, w_mod, b_mod):
    n = w_mod.shape[1]
    tn = 1024
    return pl.pallas_call(
        _mod_kernel,
        grid=(n // tn,),
        in_specs=[pl.BlockSpec((8, D_MODEL), lambda j: (0, 0)),
                  pl.BlockSpec((D_MODEL, tn), lambda j: (0, j)),
                  pl.BlockSpec((1, tn), lambda j: (0, j))],
        out_specs=pl.BlockSpec((8, tn), lambda j: (0, j)),
        out_shape=jax.ShapeDtypeStruct((8, n), F32),
        compiler_params=_params(("arbitrary",)),
        name="mod",
    )(cvec, w_mod, b_mod.reshape(1, n))


def _place_rope_key(k_nope, kr):
    lane = lax.broadcasted_iota(jnp.int32, (1, LANES), 1)
    kr = jnp.where(lane >= QK_NOPE, jnp.where(lane < QK_NOPE + QK_ROPE, kr, 0.0), 0.0)
    return k_nope + jnp.concatenate([kr] * MLA_HEADS, axis=1)


def _proj_kernel(*refs, rope, emit_cache):
    it = iter(refs)
    x_ref = next(it)
    mod_ref = next(it)
    tab_ref = next(it) if rope else None
    gmix_ref, win_ref, gq_ref, wuq_ref, gkv_ref, wk_ref, wv_ref = (next(it) for _ in range(7))
    q_ref, k_ref, v_ref, hg_ref, gate_ref = (next(it) for _ in range(5))
    if emit_cache:
        ckv_ref, kr_ref = next(it), next(it)

    sh1 = mod_ref[0, 0:1, :]
    sc1 = mod_ref[0, 1:2, :]
    hb = (_rms(x_ref[...], gmix_ref[...]) * (1.0 + sc1) + sh1).astype(BF16)

    def proj(c0, c1):
        return lax.dot_general(hb, win_ref[c0:c1, :], NT, preferred_element_type=F32)

    if rope:
        cs1, sa1, sb1 = tab_ref[0], tab_ref[1], tab_ref[2]

        def rot(x, reps):
            w = x.shape[1]
            cs = jnp.concatenate([cs1] * reps, axis=1) if reps > 1 else cs1
            sa = jnp.concatenate([sa1] * reps, axis=1) if reps > 1 else sa1
            sb = jnp.concatenate([sb1] * reps, axis=1) if reps > 1 else sb1
            half = QK_ROPE // 2
            return x * cs + pltpu.roll(x, w - half, 1) * sa + pltpu.roll(x, half, 1) * sb

    pq = proj(C_PQ, C_PKV)
    pkv = proj(C_PKV, C_PKR)
    pkr = proj(C_HQ - LANES, C_HQ)

    gate_ref[...] = jax.nn.sigmoid(proj(C_GA, C_END)).astype(BF16)

    hq = proj(C_HQ, C_ZF)
    hg_ref[:, 0:HG_WIDTH] = hq * jax.nn.sigmoid(hq)
    og = proj(C_HG, C_GA)
    hg_ref[:, 4 * HG_WIDTH:5 * HG_WIDTH] = og * jax.nn.sigmoid(og)

    qn = _rms(pq, gq_ref[...]).astype(BF16)
    q = jnp.dot(qn, wuq_ref[...], preferred_element_type=F32)
    q = rot(q, MLA_HEADS) if rope else q * ATTN_SCALE
    q_ref[...] = q.astype(BF16)

    ckv = _rms(pkv, gkv_ref[...])
    if emit_cache:
        ckv_ref[...] = ckv
        kr_ref[...] = pkr[:, KR_LANE0:]
    ckv_b = ckv.astype(BF16)
    kr = rot(pkr, 1) if rope else pkr
    k_ref[...] = _place_rope_key(jnp.dot(ckv_b, wk_ref[...], preferred_element_type=F32),
                                 pltpu.roll(kr, LANES - QK_ROPE, 1)).astype(BF16)
    lane = lax.broadcasted_iota(jnp.int32, (1, MLA_HEADS * LANES), 1)
    ones = jnp.where((lane & (LANES - 1)) >= V_DIM, 1.0, 0.0)
    v_ref[...] = (jnp.dot(ckv_b, wv_ref[...], preferred_element_type=F32) + ones).astype(BF16)

    hg_ref[:, HG_WIDTH:4 * HG_WIDTH] = proj(C_ZF, C_HG)


def _proj_call(x, mod, tabs, pw, *, seq_len, mod_row0, per_seq_mod, emit_cache, tm):
    n_tok = x.shape[0]
    tps = max(seq_len // tm, 1)
    rope = tabs is not None
    if per_seq_mod:
        mod_map = lambda t: (mod_row0 + t // tps, 0, 0)
    else:
        mod_map = lambda t: (mod_row0, 0, 0)
    row = lambda w: pl.BlockSpec((tm, w), lambda t: (t, 0))
    in_specs = [row(D_MODEL), pl.BlockSpec((1, 6, D_MODEL), mod_map)]
    args = [x, mod]
    if rope:
        in_specs.append(pl.BlockSpec((3, tm, LANES), lambda t: (0, t % tps, 0)))
        args.append(tabs)
    for name in ("g_mix", "w_in", "g_q", "w_uq", "g_kv", "w_k", "w_v"):
        in_specs.append(_const_spec(pw[name].shape))
        args.append(pw[name])
    hw = MLA_HEADS * LANES
    out_shape = [jax.ShapeDtypeStruct((n_tok, hw), BF16),
                 jax.ShapeDtypeStruct((n_tok, hw), BF16),
                 jax.ShapeDtypeStruct((n_tok, hw), BF16),
                 jax.ShapeDtypeStruct((n_tok, 5 * HG_WIDTH), F32),
                 jax.ShapeDtypeStruct((n_tok, 2 * D_MODEL), BF16)]
    out_specs = [row(hw), row(hw), row(hw), row(5 * HG_WIDTH), row(2 * D_MODEL)]
    if emit_cache:
        out_shape += [jax.ShapeDtypeStruct((n_tok, KV_LORA), F32),
                      jax.ShapeDtypeStruct((n_tok, QK_ROPE), F32)]
        out_specs += [row(KV_LORA), row(QK_ROPE)]
    return pl.pallas_call(
        functools.partial(_proj_kernel, rope=rope, emit_cache=emit_cache),
        grid=(n_tok // tm,),
        in_specs=in_specs,
        out_specs=out_specs,
        out_shape=out_shape,
        compiler_params=_params(("arbitrary",)),
        name="proj_lat" if rope else "proj_ctx",
    )(*args)


def _cachekv_kernel(ckv_ref, kr_ref, wk_ref, wv_ref, k_ref, v_ref):
    ckv_b = ckv_ref[...].astype(BF16)
    tm = ckv_b.shape[0]
    kr = jnp.concatenate([jnp.zeros((tm, QK_NOPE), F32), kr_ref[...],
                          jnp.zeros((tm, LANES - QK_NOPE - QK_ROPE), F32)], axis=1)
    k_ref[...] = _place_rope_key(jnp.dot(ckv_b, wk_ref[...], preferred_element_type=F32), kr).astype(BF16)
    lane = lax.broadcasted_iota(jnp.int32, (1, MLA_HEADS * LANES), 1)
    ones = jnp.where((lane & (LANES - 1)) >= V_DIM, 1.0, 0.0)
    v_ref[...] = (jnp.dot(ckv_b, wv_ref[...], preferred_element_type=F32) + ones).astype(BF16)


def _cachekv_call(ckv, kr, pw, tm):
    n_tok = ckv.shape[0]
    hw = MLA_HEADS * LANES
    row = lambda w: pl.BlockSpec((tm, w), lambda t: (t, 0))
    return pl.pallas_call(
        _cachekv_kernel,
        grid=(n_tok // tm,),
        in_specs=[row(KV_LORA), row(QK_ROPE), _const_spec(pw["w_k"].shape), _const_spec(pw["w_v"].shape)],
        out_specs=[row(hw), row(hw)],
        out_shape=[jax.ShapeDtypeStruct((n_tok, hw), BF16)] * 2,
        compiler_params=_params(("arbitrary",)),
        name="cache_kv",
    )(ckv, kr, pw["w_k"], pw["w_v"])


def _hg_consts():
    c_len, sub = HG_CHUNK, SUBLANES
    row8 = lax.broadcasted_iota(jnp.int32, (1, sub, HG_PAIR), 1)
    upper8 = [(row8 & (1 << j)) != 0 for j in range(3)]
    ti = lax.broadcasted_iota(jnp.int32, (c_len, c_len), 0)
    si = lax.broadcasted_iota(jnp.int32, (c_len, c_len), 1)
    txs = ti ^ si
    lvl = jnp.full((c_len, c_len), -1, jnp.int32)
    for j in range(HG_LEVELS):
        lvl = lvl + jnp.where(txs >= (1 << j), 1, 0)
    return upper8, jnp.concatenate([lvl, lvl], axis=1)


def _hg_lower_bounds(lbp, layer):
    e = jnp.exp(lbp - jnp.max(lbp, axis=0, keepdims=True))
    return jnp.sum(e[:layer + 1], axis=0) / jnp.sum(e, axis=0)


def _hg_gates(z, lbv):
    f = lbv + (1.0 - lbv) * jax.nn.sigmoid(z)
    return jnp.log2(f), 1.0 - f


def _hg_rep(slab, n_rows):
    return slab if n_rows == SUBLANES else jnp.concatenate([slab] * (n_rows // SUBLANES), axis=0)


def _hg_halves(j):
    h = 1 << j
    return [(slice(b, b + h), slice(b + h, b + 2 * h)) for b in range(0, HG_CHUNK, 2 * h)]


def _hg_sel8(upper8, j, a, b):
    shape3 = (HG_CHUNK // SUBLANES, SUBLANES, a.shape[1])
    return jnp.where(upper8[j], a.reshape(shape3), b.reshape(shape3)).reshape(a.shape)


def _hg_blockdiag(x):
    z = jnp.zeros((x.shape[0], LANES), x.dtype)
    return jnp.concatenate([jnp.concatenate([x[:, :LANES], z], axis=1),
                            jnp.concatenate([z, x[:, LANES:]], axis=1)], axis=0)


def _hg_upsweep(x, inclusive, upper8):
    c_len, sub, width = HG_CHUNK, SUBLANES, x.shape[1]
    x3 = x.reshape(c_len // sub, sub, width)
    pre3 = [x3 if inclusive else jnp.zeros_like(x3)]
    tot3 = [x3]
    for j in range(3):
        h = 1 << j
        t = tot3[-1]
        if 2 * h == sub:
            sib = pltpu.roll(t, h, 1)
        else:
            sib = jnp.where(upper8[j], pltpu.roll(t, h, 1), pltpu.roll(t, sub - h, 1))
        tot3.append(t + sib)
        pre3.append(pre3[-1] + jnp.where(upper8[j], sib, 0.0))
    pre = [p.reshape(c_len, width) for p in pre3]
    tot = [t.reshape(c_len, width) for t in tot3]
    slabs = {3: [tot[3][r:r + sub] for r in range(0, c_len, sub)]}
    for j in range(3, HG_LEVELS):
        h = 1 << j
        p, s = pre[-1], slabs[j]
        parts = []
        for i, (lo, up) in enumerate(_hg_halves(j)):
            parts += [p[lo], p[up] + _hg_rep(s[2 * i], h)]
        pre.append(jnp.concatenate(parts, axis=0))
        slabs[j + 1] = [s[2 * i] + s[2 * i + 1] for i in range(len(s) // 2)]
    return pre, tot, slabs


def _hg_chunk_terms(q, zf, zb, v, lb_f, lb_b, upper8, lvl):
    c_len, top = HG_CHUNK, HG_LEVELS
    vb = v.astype(BF16)
    lgf, kf = _hg_gates(zf, lb_f)
    lgb, kb = _hg_gates(zb, lb_b)
    pf, tf, sf = _hg_upsweep(lgf, True, upper8)
    xb, tb, sb = _hg_upsweep(lgb, False, upper8)
    a = lax.dot_general(q.astype(BF16), _hg_blockdiag((kf + kb).astype(BF16)), NT,
                        preferred_element_type=F32)
    a = jnp.where(lvl == -1, a, 0.0)
    for j in range(HG_LEVELS):
        h = 1 << j
        if j < 3:
            eq = _hg_sel8(upper8, j, pf[j], tb[j] - xb[j])
            ek = _hg_sel8(upper8, j, xb[j], tf[j] - pf[j])
            ksel = _hg_sel8(upper8, j, kb, kf)
        else:
            eqs, eks, kss = [], [], []
            for i, (lo, up) in enumerate(_hg_halves(j)):
                eqs += [_hg_rep(sb[j][2 * i], h) - xb[j][lo], pf[j][up]]
                eks += [_hg_rep(sf[j][2 * i], h) - pf[j][lo], xb[j][up]]
                kss += [kf[lo], kb[up]]
            eq, ek, ksel = (jnp.concatenate(p, axis=0) for p in (eqs, eks, kss))
        qs = (q * jnp.exp2(eq)).astype(BF16)
        ks = (ksel * jnp.exp2(ek)).astype(BF16)
        aj = lax.dot_general(qs, _hg_blockdiag(ks), NT, preferred_element_type=F32)
        a = jnp.where(lvl == j, aj, a)
    o_intra = jnp.dot(a.astype(BF16), _hg_blockdiag(vb), preferred_element_type=F32)
    tot_f, tot_b = sf[top][0], sb[top][0]
    qhf = (q * jnp.exp2(pf[top])).astype(BF16)
    qhb = (q * jnp.exp2(_hg_rep(tot_b, c_len) - xb[top])).astype(BF16)
    khf = (kf * jnp.exp2(_hg_rep(tot_f, c_len) - pf[top])).astype(BF16)
    khb = (kb * jnp.exp2(xb[top])).astype(BF16)
    uts = []
    for hh in range(2):
        hl = slice(hh * LANES, (hh + 1) * LANES)
        uts.append(lax.dot_general(vb[:, hl], jnp.concatenate([khf[:, hl], khb[:, hl]], axis=1), TN,
                                   preferred_element_type=F32))
    return o_intra, qhf, qhb, uts, jnp.exp2(tot_f[0:1, :]), jnp.exp2(tot_b[0:1, :])


def _attn_hg_kernel(*refs, nseg, tk, layer, full_seq):
    it = iter(refs)
    q_ref = next(it)
    kv = [next(it) for _ in range(2 * nseg)]
    hq_ref, zf_ref, zb_ref, hv_ref, lbp_ref = (next(it) for _ in range(5))
    if full_seq:
        g_ref, ghg_ref = next(it), next(it)
        o_ref, ohg_ref, sf_ref, sb_ref = (next(it) for _ in range(4))
    else:
        o_ref, oi_ref, qhf_ref, qhb_ref, utf_ref, utb_ref, df_ref, db_ref = (next(it) for _ in range(8))
    s_ref = next(it)
    tq = q_ref.shape[1]
    heads = q_ref.shape[2] // LANES
    hg_heads = hq_ref.shape[2] // LANES
    lane = lax.broadcasted_iota(jnp.int32, (1, LANES), 1)
    chunks = [(kv[2 * seg], kv[2 * seg + 1], c * tk)
              for seg in range(nseg) for c in range(kv[2 * seg].shape[1] // tk)]

    upper8, lvl = _hg_consts()
    lb = _hg_lower_bounds(lbp_ref[...], layer)
    local = [(ci, p) for ci in range(tq // HG_CHUNK) for p in range(hg_heads // 2)]
    kept = {}

    def hg_local(ci, p):
        rs = slice(ci * HG_CHUNK, (ci + 1) * HG_CHUNK)
        cs = slice(p * HG_PAIR, (p + 1) * HG_PAIR)
        oi, qhf, qhb, uts, dec_f, dec_b = _hg_chunk_terms(
            hq_ref[0, rs, cs], zf_ref[0, rs, cs], zb_ref[0, rs, cs], hv_ref[0, rs, cs],
            lb[0:1, cs], lb[1:2, cs], upper8, lvl)
        for hh in range(2):
            h = 2 * p + hh
            hl = slice(hh * LANES, (hh + 1) * LANES)
            utf, utb = uts[hh][:, :LANES], uts[hh][:, LANES:]
            if full_seq:
                kept[(ci, h)] = (oi[:, hl], qhf[:, hl], qhb[:, hl], utf, utb, dec_f[:, hl], dec_b[:, hl])
            else:
                utf_ref[0, ci, h] = utf
                utb_ref[0, ci, h] = utb
        if not full_seq:
            oi_ref[0, rs, cs] = oi
            qhf_ref[0, rs, cs] = qhf
            qhb_ref[0, rs, cs] = qhb
            df_ref[0, ci, :, cs] = dec_f
            db_ref[0, ci, :, cs] = dec_b

    outs = []
    for h in range(heads):
        hs = slice(h * LANES, (h + 1) * LANES)
        qh = q_ref[0, :, hs]
        mx = jnp.full((tq, LANES), -jnp.inf, F32)
        for i, (k_ref, _, off) in enumerate(chunks):
            s = lax.dot_general(qh, k_ref[0, off:off + tk, hs], NT, preferred_element_type=F32)
            s_ref[:, i * tk:(i + 1) * tk] = s
            for w in range(tk // LANES):
                mx = jnp.maximum(mx, s[:, w * LANES:(w + 1) * LANES])
        m = jnp.broadcast_to(jnp.max(mx, axis=1, keepdims=True), (tq, LANES))
        mb = jnp.concatenate([m] * (tk // LANES), axis=1)
        acc = jnp.zeros((tq, LANES), F32)
        for i, (_, v_ref, off) in enumerate(chunks):
            p = jnp.exp2(s_ref[:, i * tk:(i + 1) * tk] - mb).astype(BF16)
            acc = acc + jnp.dot(p, v_ref[0, off:off + tk, hs], preferred_element_type=F32)
        outs.append(acc / pltpu.roll(acc, V_DIM, 1))
    for ci, p in local:
        hg_local(ci, p)
    for pair in range(heads // 2):
        a = outs[2 * pair]
        b = pltpu.roll(outs[2 * pair + 1], V_DIM, 1)
        o_ref[0, :, pair * LANES:(pair + 1) * LANES] = jnp.where(lane < V_DIM, a, b).astype(BF16)

    if full_seq:
        n = tq // HG_CHUNK
        for h in range(hg_heads):
            cs = slice(h * LANES, (h + 1) * LANES)
            enter_b, st = [None] * n, None
            for c in reversed(range(n)):
                enter_b[c] = st
                _, _, _, _, utb, _, dec_b = kept[(c, h)]
                st = utb if st is None else dec_b * st + utb
            sb_ref[0, h] = st.T
            st = None
            for c in range(n):
                rs = slice(c * HG_CHUNK, (c + 1) * HG_CHUNK)
                oi, qhf, qhb, utf, _, dec_f, _ = kept[(c, h)]
                if st is not None:
                    oi = oi + lax.dot_general(qhf, st.astype(BF16), NT, preferred_element_type=F32)
                if enter_b[c] is not None:
                    oi = oi + lax.dot_general(qhb, enter_b[c].astype(BF16), NT, preferred_element_type=F32)
                ohg_ref[0, rs, cs] = (_rms(oi, ghg_ref[:, cs]) * g_ref[0, rs, cs]).astype(BF16)
                st = utf if st is None else dec_f * st + utf
            sf_ref[0, h] = st.T


def _attn_hg_call(q, kvs, hgin, lb_param, g_hg, *, tq, tk, heads, layer, full_seq, name):
    b, t, _ = q.shape
    n_chunks = t // HG_CHUNK
    cpt = tq // HG_CHUNK
    groups = MLA_HEADS // heads
    hgh = HG_HEADS // groups
    assert heads % 2 == 0 and hgh % 2 == 0
    aw, gw = heads * LANES, hgh * LANES
    in_specs = [pl.BlockSpec((1, tq, aw), lambda i, p, j: (i, j, p))]
    args = [q]
    n_keys = 0
    for k, v in kvs:
        n_keys += k.shape[1]
        for a in (k, v):
            mode = dict(pipeline_mode=pl.Buffered(1)) if groups == 1 and t // tq > 1 else {}
            in_specs.append(pl.BlockSpec((1, a.shape[1], aw), lambda i, p, j: (i, 0, p), **mode))
            args.append(a)
    for g in range(4):
        in_specs.append(pl.BlockSpec((1, tq, gw), lambda i, p, j, g=g: (i, j, g * groups + p)))
        args.append(hgin)
    in_specs.append(pl.BlockSpec((lb_param.shape[0], 2, gw), lambda i, p, j: (0, 0, p)))
    args.append(lb_param)
    rows = lambda w: pl.BlockSpec((1, tq, w), lambda i, p, j: (i, j, p))
    o_shape = jax.ShapeDtypeStruct((b, t, MLA_HEADS * V_DIM), BF16)
    if full_seq:
        assert t == tq
        in_specs += [pl.BlockSpec((1, tq, gw), lambda i, p, j: (i, j, 4 * groups + p)),
                     pl.BlockSpec((1, gw), lambda i, p, j: (0, p))]
        args += [hgin, g_hg]
        st_spec = pl.BlockSpec((1, hgh, HG_DK, HG_DV), lambda i, p, j: (i, p, 0, 0))
        st_shape = jax.ShapeDtypeStruct((b, HG_HEADS, HG_DK, HG_DV), F32)
        out_specs = [rows(heads * V_DIM), rows(gw), st_spec, st_spec]
        out_shape = [o_shape, jax.ShapeDtypeStruct((b, t, HG_WIDTH), BF16), st_shape, st_shape]
    else:
        ut_spec = pl.BlockSpec((1, cpt, hgh, HG_DV, HG_DK), lambda i, p, j: (i, j, p, 0, 0))
        dec_spec = pl.BlockSpec((1, cpt, 1, gw), lambda i, p, j: (i, j, 0, p))
        ut_shape = jax.ShapeDtypeStruct((b, n_chunks, HG_HEADS, HG_DV, HG_DK), F32)
        dec_shape = jax.ShapeDtypeStruct((b, n_chunks, 1, HG_WIDTH), F32)
        out_specs = [rows(heads * V_DIM), rows(gw), rows(gw), rows(gw),
                     ut_spec, ut_spec, dec_spec, dec_spec]
        out_shape = [o_shape,
                     jax.ShapeDtypeStruct((b, t, HG_WIDTH), F32),
                     jax.ShapeDtypeStruct((b, t, HG_WIDTH), BF16),
                     jax.ShapeDtypeStruct((b, t, HG_WIDTH), BF16),
                     ut_shape, ut_shape, dec_shape, dec_shape]
    return pl.pallas_call(
        functools.partial(_attn_hg_kernel, nseg=len(kvs), tk=tk, layer=layer, full_seq=full_seq),
        grid=(b, groups, t // tq),
        in_specs=in_specs,
        out_specs=out_specs,
        out_shape=out_shape,
        scratch_shapes=[pltpu.VMEM((tq, n_keys), F32)],
        compiler_params=_params(("arbitrary", "arbitrary", "arbitrary")),
        name=name,
    )(*args)


def _hg_scan_kernel(*refs, has_s0):
    it = iter(refs)
    oi_ref, qhf_ref, qhb_ref, utf_ref, utb_ref, df_ref, db_ref, g_ref, ghg_ref = (next(it) for _ in range(9))
    if has_s0:
        s0f_ref, s0b_ref = next(it), next(it)
    o_ref, sf_ref, sb_ref, sbst_ref = (next(it) for _ in range(4))
    c_len = HG_CHUNK
    n_chunks = utf_ref.shape[1]
    unroll = min(HG_UNROLL, n_chunks)

    def rows_of(c):
        return pl.ds(pl.multiple_of(c * c_len, c_len), c_len)

    def bwd_body(i, st):
        c = n_chunks - 1 - i
        sbst_ref[c] = st.astype(BF16)
        return db_ref[0, c] * st + utb_ref[0, c, 0]

    st_b0 = s0b_ref[0, 0].T if has_s0 else jnp.zeros((HG_DV, HG_DK), F32)
    st_b = lax.fori_loop(0, n_chunks, bwd_body, st_b0, unroll=unroll)
    sb_ref[0, 0] = st_b.T

    ghg = ghg_ref[...]

    def fwd_body(c, st):
        qcat = jnp.concatenate([qhf_ref[0, rows_of(c), :], qhb_ref[0, rows_of(c), :]], axis=1)
        scat = jnp.concatenate([st.astype(BF16), sbst_ref[c]], axis=1)
        o = oi_ref[0, rows_of(c), :] + lax.dot_general(qcat, scat, NT, preferred_element_type=F32)
        on = _rms(o, ghg) * g_ref[0, rows_of(c), :]
        o_ref[0, rows_of(c), :] = on.astype(BF16)
        return df_ref[0, c] * st + utf_ref[0, c, 0]

    st_f0 = s0f_ref[0, 0].T if has_s0 else jnp.zeros((HG_DV, HG_DK), F32)
    st_f = lax.fori_loop(0, n_chunks, fwd_body, st_f0, unroll=unroll)
    sf_ref[0, 0] = st_f.T


def _hg_scan_call(oi, qhf, qhb, utf, utb, dec_f, dec_b, hgin, g_hg, s0, *, name):
    b, t, _ = oi.shape
    n_chunks = t // HG_CHUNK
    has_s0 = s0 is not None
    col = pl.BlockSpec((1, t, LANES), lambda i, h: (i, 0, h))
    ut_spec = pl.BlockSpec((1, n_chunks, 1, HG_DV, HG_DK), lambda i, h: (i, 0, h, 0, 0))
    dec_spec = pl.BlockSpec((1, n_chunks, 1, LANES), lambda i, h: (i, 0, 0, h))
    gate_col = pl.BlockSpec((1, t, LANES), lambda i, h: (i, 0, 4 * HG_HEADS + h))
    in_specs = [col, col, col, ut_spec, ut_spec, dec_spec, dec_spec, gate_col,
                pl.BlockSpec((1, LANES), lambda i, h: (0, h))]
    args = [oi, qhf, qhb, utf, utb, dec_f, dec_b, hgin, g_hg]
    st_spec = pl.BlockSpec((1, 1, HG_DK, HG_DV), lambda i, h: (i, h, 0, 0))
    if has_s0:
        in_specs += [st_spec, st_spec]
        args += list(s0)
    st_shape = jax.ShapeDtypeStruct((b, HG_HEADS, HG_DK, HG_DV), F32)
    return pl.pallas_call(
        functools.partial(_hg_scan_kernel, has_s0=has_s0),
        grid=(b, HG_HEADS),
        in_specs=in_specs,
        out_specs=[col, st_spec, st_spec],
        out_shape=[jax.ShapeDtypeStruct((b, t, HG_WIDTH), BF16), st_shape, st_shape],
        scratch_shapes=[pltpu.VMEM((n_chunks, HG_DV, HG_DK), BF16)],
        compiler_params=_params(("arbitrary", "arbitrary")),
        name=name,
    )(*args)


def _gelu_tanh(x):
    return 0.5 * x * (1.0 + jnp.tanh(0.7978845608028654 * (x + 0.044715 * (x * x * x))))


def _mixffn_kernel(om_ref, omp_ref, omn_ref, oh_ref, ohp_ref, ohn_ref, gt_ref, gtp_ref, gtn_ref,
                   x_ref, xp_ref, xn_ref, mod_ref, wmo_ref, who_ref, wout_ref, gffn_ref,
                   wi_ref, cw_ref, cb_ref, wo_ref, gfin_ref, y_ref, gsc_ref, u_ref, *, tm, seq_len):
    halo = BF16_SUBLANES
    t = pl.program_id(0)

    def ext(p_ref, m_ref, n_ref):
        return jnp.concatenate([p_ref[...], m_ref[...], n_ref[...]], axis=0)

    a = jnp.dot(ext(omp_ref, om_ref, omn_ref), wmo_ref[...], preferred_element_type=F32)
    b = jnp.dot(ext(ohp_ref, oh_ref, ohn_ref), who_ref[...], preferred_element_type=F32)
    gate = ext(gtp_ref, gt_ref, gtn_ref)
    mix = gate[:, 0:D_MODEL] * a + gate[:, D_MODEL:2 * D_MODEL] * b
    y = jnp.dot(mix.astype(BF16), wout_ref[...], preferred_element_type=F32)
    gt1 = mod_ref[0, 2:3, :]
    sh2 = mod_ref[0, 3:4, :]
    sc2 = mod_ref[0, 4:5, :]
    x1 = ext(xp_ref, x_ref, xn_ref) + gt1 * y
    h2 = (_rms(x1, gffn_ref[...]) * (1.0 + sc2) + sh2).astype(BF16)
    h2_own = h2[halo:halo + tm]

    pos = (lax.broadcasted_iota(jnp.int32, (tm, 1), 0) + t * tm) & (seq_len - 1)
    has_prev = pos != 0
    has_next = pos != seq_len - 1
    for j in range(D_FF // FF_CHUNK):
        cs = slice(j * FF_CHUNK, (j + 1) * FF_CHUNK)
        vs = slice(D_FF + j * FF_CHUNK, D_FF + (j + 1) * FF_CHUNK)
        gsc_ref[...] = jnp.dot(h2, wi_ref[:, cs], preferred_element_type=F32)
        g_prev = jnp.where(has_prev, gsc_ref[halo - 1:halo - 1 + tm, :], 0.0)
        g_cur = gsc_ref[halo:halo + tm, :]
        g_next = jnp.where(has_next, gsc_ref[halo + 1:halo + 1 + tm, :], 0.0)
        gconv = cb_ref[:, cs] + g_prev * cw_ref[0:1, cs] + g_cur * cw_ref[1:2, cs] + g_next * cw_ref[2:3, cs]
        val = jnp.dot(h2_own, wi_ref[:, vs], preferred_element_type=F32)
        u_ref[:, cs] = (_gelu_tanh(gconv) * val).astype(BF16)
    f = jnp.dot(u_ref[...], wo_ref[...], preferred_element_type=F32)
    gt2 = mod_ref[0, 5:6, :]
    y_ref[...] = _rms(x1[halo:halo + tm] + gt2 * f, gfin_ref[...])


def _mixffn_call(om, oh, gate, x, mod, pw, *, seq_len, mod_row0, per_seq_mod, tm, name):
    n_tok = x.shape[0]
    tps = max(seq_len // tm, 1)
    halo = BF16_SUBLANES
    n_hblk = n_tok // halo
    hpt = tm // halo
    if per_seq_mod:
        mod_map = lambda t: (mod_row0 + t // tps, 0, 0)
    else:
        mod_map = lambda t: (mod_row0, 0, 0)

    def with_halo(w):
        return [pl.BlockSpec((tm, w), lambda t: (t, 0)),
                pl.BlockSpec((halo, w), lambda t: (jnp.maximum(t * hpt - 1, 0), 0)),
                pl.BlockSpec((halo, w), lambda t: (jnp.minimum((t + 1) * hpt, n_hblk - 1), 0))]

    ws = [pw[k] for k in ("w_mla_o", "w_hg_o", "w_out", "g_ffn", "w_ffn_in", "conv_w", "conv_b",
                          "w_ffn_out", "g_final")]
    return pl.pallas_call(
        functools.partial(_mixffn_kernel, tm=tm, seq_len=seq_len),
        grid=(n_tok // tm,),
        in_specs=(with_halo(MLA_HEADS * V_DIM) + with_halo(HG_WIDTH) + with_halo(2 * D_MODEL)
                  + with_halo(D_MODEL) + [pl.BlockSpec((1, 6, D_MODEL), mod_map)]
                  + [_const_spec(w.shape) for w in ws]),
        out_specs=pl.BlockSpec((tm, D_MODEL), lambda t: (t, 0)),
        out_shape=jax.ShapeDtypeStruct((n_tok, D_MODEL), F32),
        scratch_shapes=[pltpu.VMEM((tm + 2 * halo, FF_CHUNK), F32),
                        pltpu.VMEM((tm, D_FF), BF16)],
        compiler_params=_params(("arbitrary",)),
        name=name,
    )(om, om, om, oh, oh, oh, gate, gate, gate, x, x, x, mod, *ws)


def _rope_tables(seq_len):
    n_freq = QK_ROPE // 4
    inv = jnp.power(ROPE_BASE, -jnp.arange(n_freq, dtype=F32) / n_freq)
    z8 = jnp.zeros((n_freq,), F32)
    z64 = jnp.zeros((QK_NOPE,), F32)
    inv_r = jnp.concatenate([z64] + [inv, z8] * 4)
    inv_c = jnp.concatenate([z64] + [z8, inv] * 4)
    pos = jnp.arange(seq_len, dtype=jnp.int32)
    r_pos = (pos // GRID_W).astype(F32)[:, None]
    c_pos = (pos % GRID_W).astype(F32)[:, None]
    ang = r_pos * inv_r + c_pos * inv_c
    cos, sin = jnp.cos(ang), jnp.sin(ang)
    s = ATTN_SCALE
    half = QK_ROPE // 2
    o16, z16 = jnp.ones((half,), F32), jnp.zeros((half,), F32)
    c_cs = jnp.concatenate([jnp.full((QK_NOPE + QK_ROPE,), s, F32), o16, o16])
    c_sa = jnp.concatenate([z64, -s * o16, z16, -o16, z16])
    c_sb = jnp.concatenate([z64, z16, s * o16, z16, o16])
    return jnp.stack([cos * c_cs, sin * c_sa, sin * c_sb], axis=0)


def _prep_weights(l, g_mix, w_in, g_q, w_uq, g_kv, w_ukv, w_mla_o, g_hg, w_hg_o, w_out, g_ffn,
                  w_ffn_in, conv_w, conv_b, w_ffn_out, g_final):
    d = D_MODEL
    w_in_t = jnp.swapaxes(w_in[l], 0, 1).astype(BF16)
    hd = QK_NOPE + QK_ROPE
    w_uq_p = jnp.concatenate([w_uq[l].reshape(Q_LORA, MLA_HEADS, hd),
                              jnp.zeros((Q_LORA, MLA_HEADS, LANES - hd), F32)], axis=-1)
    wkv = w_ukv[l].reshape(KV_LORA, MLA_HEADS, QK_NOPE + V_DIM)
    zpad = jnp.zeros((KV_LORA, MLA_HEADS, LANES - QK_NOPE), F32)
    w_k = jnp.concatenate([wkv[..., :QK_NOPE], zpad], axis=-1).reshape(KV_LORA, MLA_HEADS * LANES)
    w_v =jnp.concatenate([wkv[..., QK_NOPE:], jnp.zeros((KV_LORA, MLA_HEADS, LANES - V_DIM), F32)],
                          axis=-1).reshape(KV_LORA, MLA_HEADS * LANES)
    return {
        "g_mix": g_mix[l].reshape(1, d), "w_in": w_in_t,
        "g_q": g_q[l].reshape(1, Q_LORA), "w_uq": w_uq_p.reshape(Q_LORA, MLA_HEADS * LANES).astype(BF16),
        "g_kv": g_kv[l].reshape(1, KV_LORA), "w_k": w_k.astype(BF16), "w_v": w_v.astype(BF16),
        "w_mla_o": w_mla_o[l].astype(BF16), "w_hg_o": w_hg_o[l].astype(BF16),
        "w_out": w_out[l].astype(BF16), "g_ffn": g_ffn[l].reshape(1, d),
        "w_ffn_in": w_ffn_in[l].astype(BF16),
        "conv_w": conv_w[l], "conv_b": conv_b[l].reshape(1, D_FF),
        "w_ffn_out": w_ffn_out[l].astype(BF16), "g_final": g_final.reshape(1, d),
        "g_hg": g_hg[l].reshape(1, HG_WIDTH),
    }


def _trunk(x, mod, tabs, cache, pw, lb_param, *, layer, batch, seq_len, mod_row0, per_seq_mod,
           emit_cache, tm, tq, tag):
    n_tok = batch * seq_len
    hw = MLA_HEADS * LANES
    outs = _proj_call(x, mod, tabs, pw, seq_len=seq_len, mod_row0=mod_row0, per_seq_mod=per_seq_mod,
                      emit_cache=emit_cache, tm=tm)
    q, k, v, hgin, gate = outs[:5]
    hgin = hgin.reshape(batch, seq_len, 5 * HG_WIDTH)
    kvs = [(k.reshape(batch, seq_len, hw), v.reshape(batch, seq_len, hw))]
    s0 = None
    if cache is not None:
        ckv_c, kr_c, s0f, s0b = cache
        past = ckv_c.shape[1]
        kc, vc = _cachekv_call(ckv_c.reshape(batch * past, KV_LORA), kr_c.reshape(batch * past, QK_ROPE),
                               pw, tm=min(512, batch * past))
        kvs.append((kc.reshape(batch, past, hw), vc.reshape(batch, past, hw)))
        s0 = (s0f, s0b)
    full_seq = seq_len == tq and s0 is None
    o_mla, *hg_out = _attn_hg_call(q.reshape(batch, seq_len, hw), kvs, hgin, lb_param, pw["g_hg"], tq=tq,
                                   tk=min(512, seq_len),
                                   heads=MLA_HEADS if seq_len == tq else ATTN_HEADS_PER_STEP,
                                   layer=layer, full_seq=full_seq, name="attn_hg_" + tag)
    if full_seq:
        o_hg, s_f, s_b = hg_out
    else:
        o_hg, s_f, s_b = _hg_scan_call(*hg_out, hgin, pw["g_hg"], s0, name="hg_scan_" + tag)
    common = dict(seq_len=seq_len, mod_row0=mod_row0, per_seq_mod=per_seq_mod, tm=tm)
    y = _mixffn_call(o_mla.reshape(n_tok, MLA_HEADS * V_DIM), o_hg.reshape(n_tok, HG_WIDTH), gate, x,
                     mod, pw, name="mixffn_" + tag, **common)
    extra = outs[5:] if emit_cache else None
    return y, extra, s_f, s_b


def kernel(x_prompt, x_sample, cache_ckv, cache_krope, state_hgrn_fwd, state_hgrn_bwd, c, c_ctx, w_mod,
           b_mod, g_mix, w_in, g_q, w_uq, g_kv, w_ukv, w_mla_o, lb_param, g_hg, w_hg_o, w_out, g_ffn,
           w_ffn_in, conv_w, conv_b, w_ffn_out, g_final):
    batch, seq, d = x_prompt.shape
    dec_batch, dec_seq, _ = x_sample.shape
    depth = w_in.shape[0]
    assert depth == 1, "single trunk layer"
    l = 0
    cvec = jnp.concatenate([c_ctx[None, :], c, jnp.zeros((8 - 1 - dec_batch, d), F32)], axis=0)
    mod = _mod_call(cvec, w_mod[l], b_mod[l]).reshape(8, 6, d)
    pw = _prep_weights(l, g_mix, w_in, g_q, w_uq, g_kv, w_ukv, w_mla_o, g_hg, w_hg_o, w_out, g_ffn,
                       w_ffn_in, conv_w, conv_b, w_ffn_out, g_final)
    tabs = _rope_tables(dec_seq)

    y_ctx, cache_out, s_f, s_b = _trunk(
        x_prompt.reshape(batch * seq, d), mod, None, None, pw, lb_param, layer=l, batch=batch,
        seq_len=seq, mod_row0=0, per_seq_mod=False, emit_cache=True, tm=TOKEN_TILE, tq=min(Q_TILE, seq),
        tag="ctx")
    cache_l = (cache_ckv[:, l], cache_krope[:, l], state_hgrn_fwd[:, l], state_hgrn_bwd[:, l])
    y_lat, _, _, _ = _trunk(
        x_sample.reshape(dec_batch * dec_seq, d), mod, tabs, cache_l, pw, lb_param, layer=l,
        batch=dec_batch, seq_len=dec_seq, mod_row0=1, per_seq_mod=True, emit_cache=False, tm=TOKEN_TILE,
        tq=min(Q_TILE, dec_seq), tag="lat")

    ckv_new, kr_new = cache_out
    return (y_ctx.reshape(batch, seq, d), y_lat.reshape(dec_batch, dec_seq, d),
            ckv_new.reshape(batch, 1, seq, KV_LORA), kr_new.reshape(batch, 1, seq, QK_ROPE),
            s_f.reshape(batch, 1, HG_HEADS, HG_DK, HG_DV), s_b.reshape(batch, 1, HG_HEADS, HG_DK, HG_DV))
```
